```python
import functools
import jax, jax.numpy as jnp
from jax import lax
import numpy as np

D_MODEL = 1024
BATCH = 16
SEQ = 2048
DEPTH = 1
DEC_BATCH = 128
DEC_SEQ = 4
PAST_LEN = 16384
PAGE_SIZE = 128

HEAD_DIM = 64
ATTN_WIDTH = D_MODEL // 2
N_Q_HEADS = ATTN_WIDTH // HEAD_DIM
N_KV_HEADS = N_Q_HEADS // 4
KV_WIDTH = N_KV_HEADS * HEAD_DIM
GQA_GROUP = N_Q_HEADS // N_KV_HEADS
RWKV_WIDTH = D_MODEL - ATTN_WIDTH
N_RWKV_HEADS = RWKV_WIDTH // HEAD_DIM
WINDOW = 128
BLOCK = WINDOW
SCALE = HEAD_DIM ** -0.5
DECAY_LORA = 64
AAA_LORA = 64
GATE_LORA = 128
D_FF = 4 * D_MODEL
RMS_EPS = 1e-6
GN_EPS = 64e-5
Q_OFF = 0
K_OFF = Q_OFF + ATTN_WIDTH
V_OFF = K_OFF + KV_WIDTH
RWKV_OFF = V_OFF + KV_WIDTH
RWKV_PROJ = 3 * RWKV_WIDTH + DECAY_LORA + AAA_LORA + GATE_LORA
RWKV_SPLITS = (RWKV_WIDTH, RWKV_WIDTH + DECAY_LORA, 2 * RWKV_WIDTH + DECAY_LORA,
               3 * RWKV_WIDTH + DECAY_LORA, 3 * RWKV_WIDTH + DECAY_LORA + AAA_LORA)
IN_WIDTH = RWKV_OFF + RWKV_PROJ

kernel_name = 'hymba_swa_rwkv7_step'


def _rms_norm(x, g):
    x32 = x.astype(jnp.float32)
    y = x32 * lax.rsqrt(jnp.mean(x32 * x32, axis=-1, keepdims=True) + RMS_EPS)
    return (y * g.astype(jnp.float32)).astype(x.dtype)


def _sink_softmax(scores, mask, sinks):
    s = jnp.where(mask, scores.astype(jnp.float32), -jnp.inf)
    sink = sinks.astype(jnp.float32)[..., None, None]
    m = jnp.maximum(jnp.max(s, axis=-1, keepdims=True), sink)
    p = jnp.exp(s - m)
    return p / (jnp.sum(p, axis=-1, keepdims=True) + jnp.exp(sink - m))


def _swa_prompt(q, k, v, sinks):
    B, S = q.shape[0], q.shape[1]
    nb = S // BLOCK
    qb = q.reshape(B, nb, BLOCK, N_KV_HEADS, GQA_GROUP, HEAD_DIM)
    kb = k.reshape(B, nb, BLOCK, N_KV_HEADS, HEAD_DIM)
    vb = v.reshape(B, nb, BLOCK, N_KV_HEADS, HEAD_DIM)

    def with_prev(t):
        prev = jnp.concatenate([jnp.zeros_like(t[:, :1]), t[:, :-1]], axis=1)
        return jnp.concatenate([prev, t], axis=2)

    kc, vc = with_prev(kb), with_prev(vb)
    blk = jnp.arange(nb)[:, None] * BLOCK
    qpos = blk + jnp.arange(BLOCK)[None, :]
    kpos = blk + jnp.arange(2 * BLOCK)[None, :] - BLOCK
    rel = qpos[:, :, None] - kpos[:, None, :]
    mask = (rel >= 0) & (rel < WINDOW) & (kpos[:, None, :] >= 0)
    scores = jnp.einsum('bnqhgd,bnkhd->bnhgqk', qb, kc) * SCALE
    probs = _sink_softmax(scores, mask[None, :, None, None], sinks.reshape(N_KV_HEADS, GQA_GROUP))
    out = jnp.einsum('bnhgqk,bnkhd->bnqhgd', probs.astype(v.dtype), vc)
    return out.reshape(B, S, ATTN_WIDTH), k[:, S - WINDOW:], v[:, S - WINDOW:]


def _swa_cached(cache_k, cache_v, q, k, v, sinks):
    B, T = q.shape[0], q.shape[1]
    wc = cache_k.shape[1]
    kall = jnp.concatenate([cache_k.astype(k.dtype), k], axis=1)
    vall = jnp.concatenate([cache_v.astype(v.dtype), v], axis=1)
    rel = jnp.arange(T)[:, None] - (jnp.arange(wc + T)[None, :] - wc)
    mask = (rel >= 0) & (rel < WINDOW)
    qg = q.reshape(B, T, N_KV_HEADS, GQA_GROUP, HEAD_DIM)
    scores = jnp.einsum('bthgd,bkhd->bhgtk', qg, kall) * SCALE
    probs = _sink_softmax(scores, mask, sinks.reshape(N_KV_HEADS, GQA_GROUP))
    out = jnp.einsum('bhgtk,bkhd->bthgd', probs.astype(v.dtype), vall)
    return out.reshape(B, T, ATTN_WIDTH), kall[:, T:], vall[:, T:]


def _wkv7_scan(r, decay, k, v, a_vec, b_vec, s0):
    def step(s, inp):
        r_t, d_t, k_t, v_t, a_t, b_t = inp
        sa = jnp.einsum('bhij,bhj->bhi', s, a_t)
        s = s * d_t[:, :, None, :] + sa[..., None] * b_t[:, :, None, :] + v_t[..., None] * k_t[:, :, None, :]
        return s, jnp.einsum('bhij,bhj->bhi', s, r_t)

    xs = tuple(jnp.swapaxes(t, 0, 1) for t in (r, decay, k, v, a_vec, b_vec))
    s_final, ys = lax.scan(step, s0.astype(jnp.float32), xs)
    return jnp.swapaxes(ys, 0, 1), s_final


def _rwkv7_mix(p, p_prev, s0, lp):
    B, T = p.shape[0], p.shape[1]
    f = lambda t: t.astype(jnp.float32)
    p = f(p)
    p_shift = jnp.concatenate([f(p_prev)[:, None], p[:, :-1]], axis=1)
    pm = p + (p_shift - p) * f(lp['rwkv_mu'])
    xr, xw, xk, xv, xa, xg = jnp.split(pm, RWKV_SPLITS, axis=-1)
    w_log = -jax.nn.softplus(-(f(lp['w_decay_0']) + jnp.tanh(xw) @ f(lp['w_decay_up']))) - 0.5
    decay = jnp.exp(-jnp.exp(w_log))
    a = jax.nn.sigmoid(f(lp['a_0']) + xa @ f(lp['a_up']))
    g = jax.nn.sigmoid(xg) @ f(lp['g_up'])
    hs = lambda t: t.reshape(B, T, N_RWKV_HEADS, HEAD_DIM)
    kk = hs(xk * f(lp['k_k']))
    kk = kk / jnp.maximum(jnp.sqrt(jnp.sum(kk * kk, axis=-1, keepdims=True)), 1e-12)
    a_h = hs(a)
    k_h = hs(xk * (1.0 + (a - 1.0) * f(lp['k_a'])))
    r_h, v_h = hs(xr), hs(xv)
    y, s = _wkv7_scan(r_h, hs(decay), k_h, v_h, -kk, kk * a_h, s0)
    mu = jnp.mean(y, axis=-1, keepdims=True)
    var = jnp.mean(jnp.square(y - mu), axis=-1, keepdims=True)
    yn = ((y - mu) * lax.rsqrt(var + GN_EPS)).reshape(B, T, RWKV_WIDTH) * f(lp['ln_x_g']) + f(lp['ln_x_b'])
    bonus = jnp.sum(r_h * k_h * f(lp['r_k']), axis=-1, keepdims=True) * v_h
    out = (yn + bonus.reshape(B, T, RWKV_WIDTH)) * g
    return out, s


def _layer(x, attn_fn, shift_prev, s0, lp):
    B, T = x.shape[0], x.shape[1]
    h = _rms_norm(x, lp['norm1_g'])
    proj = h @ lp['w_in']
    q = proj[..., Q_OFF:K_OFF].reshape(B, T, N_Q_HEADS, HEAD_DIM)
    k = proj[..., K_OFF:V_OFF].reshape(B, T, N_KV_HEADS, HEAD_DIM)
    v = proj[..., V_OFF:RWKV_OFF].reshape(B, T, N_KV_HEADS, HEAD_DIM)
    q = _rms_norm(q, lp['q_norm_g'])
    k = _rms_norm(k, lp['k_norm_g'])
    attn_out, new_k, new_v = attn_fn(q, k, v, lp['attn_sinks'])
    p_prev = shift_prev.astype(h.dtype) @ lp['w_in'][:, RWKV_OFF:]
    rwkv_out, s_new = _rwkv7_mix(proj[..., RWKV_OFF:], p_prev, s0, lp)
    mix = jnp.concatenate([attn_out, rwkv_out.astype(x.dtype)], axis=-1)
    x = x + mix @ lp['w_out']
    h2 = _rms_norm(x, lp['norm2_g'])
    x = x + jnp.square(jax.nn.relu(h2 @ lp['w_ff_up'])) @ lp['w_ff_down']
    return x, new_k, new_v, s_new, h[:, -1]


def setup_inputs(seed: int = 0) -> dict:
    key = jax.random.key(seed)
    ks = jax.random.split(key, 32)
    f32 = jnp.float32
    nrm = lambda i, shape, s: jax.random.normal(ks[i], shape, f32) * s
    cache_rows = min(WINDOW, PAST_LEN)
    L = DEPTH
    return {
        'x_prompt': nrm(0, (BATCH, SEQ, D_MODEL), 1.0),
        'x_sample': nrm(1, (DEC_BATCH, DEC_SEQ, D_MODEL), 1.0),
        'cache_k': nrm(2, (L, DEC_BATCH, cache_rows, N_KV_HEADS, HEAD_DIM), 1.0),
        'cache_v': nrm(3, (L, DEC_BATCH, cache_rows, N_KV_HEADS, HEAD_DIM), 1.0),
        'state_wkv': nrm(4, (L, DEC_BATCH, N_RWKV_HEADS, HEAD_DIM, HEAD_DIM), 0.1),
        'state_shift': nrm(5, (L, DEC_BATCH, D_MODEL), 1.0),
        'norm1_g': 1.0 + nrm(6, (L, D_MODEL), 0.02),
        'w_in': nrm(7, (L, D_MODEL, IN_WIDTH), D_MODEL ** -0.5),
        'q_norm_g': 1.0 + nrm(8, (L, HEAD_DIM), 0.02),
        'k_norm_g': 1.0 + nrm(9, (L, HEAD_DIM), 0.02),
        'attn_sinks': nrm(10, (L, N_Q_HEADS), 0.5),
        'rwkv_mu': jax.random.uniform(ks[11], (L, RWKV_PROJ), f32),
        'w_decay_0': jax.random.uniform(ks[12], (L, RWKV_WIDTH), f32, -4.0, 1.0),
        'w_decay_up': nrm(13, (L, DECAY_LORA, RWKV_WIDTH), 0.1),
        'a_0': nrm(14, (L, RWKV_WIDTH), 0.1),
        'a_up': nrm(15, (L, AAA_LORA, RWKV_WIDTH), 0.5 * AAA_LORA ** -0.5),
        'g_up': nrm(16, (L, GATE_LORA, RWKV_WIDTH), GATE_LORA ** -0.5),
        'k_k': 0.85 + nrm(17, (L, RWKV_WIDTH), 0.02),
        'k_a': 1.0 + nrm(18, (L, RWKV_WIDTH), 0.02),
        'r_k': nrm(19, (L, N_RWKV_HEADS, HEAD_DIM), 0.1),
        'ln_x_g': 1.0 + nrm(20, (L, RWKV_WIDTH), 0.02),
        'ln_x_b': nrm(21, (L, RWKV_WIDTH), 0.02),
        'w_out': nrm(22, (L, D_MODEL, D_MODEL), D_MODEL ** -0.5),
        'norm2_g': 1.0 + nrm(23, (L, D_MODEL), 0.02),
        'w_ff_up': nrm(24, (L, D_MODEL, D_FF), D_MODEL ** -0.5),
        'w_ff_down': nrm(25, (L, D_FF, D_MODEL), D_FF ** -0.5),
    }


def reference(x_prompt, x_sample, cache_k, cache_v, state_wkv, state_shift, norm1_g, w_in, q_norm_g,
              k_norm_g, attn_sinks, rwkv_mu, w_decay_0, w_decay_up, a_0, a_up, g_up, k_k, k_a, r_k,
              ln_x_g, ln_x_b, w_out, norm2_g, w_ff_up, w_ff_down):
    yp, ys = x_prompt, x_sample
    pk, pv, pw, psh, sk, sv, sw, ssh = [], [], [], [], [], [], [], []
    for l in range(DEPTH):
        lp = {'norm1_g': norm1_g[l], 'w_in': w_in[l], 'q_norm_g': q_norm_g[l], 'k_norm_g': k_norm_g[l],
              'attn_sinks': attn_sinks[l], 'rwkv_mu': rwkv_mu[l], 'w_decay_0': w_decay_0[l],
              'w_decay_up': w_decay_up[l], 'a_0': a_0[l], 'a_up': a_up[l], 'g_up': g_up[l],
              'k_k': k_k[l], 'k_a': k_a[l], 'r_k': r_k[l], 'ln_x_g': ln_x_g[l], 'ln_x_b': ln_x_b[l],
              'w_out': w_out[l], 'norm2_g': norm2_g[l], 'w_ff_up': w_ff_up[l], 'w_ff_down': w_ff_down[l]}
        bp = yp.shape[0]
        zero_shift = jnp.zeros((bp, D_MODEL), yp.dtype)
        zero_wkv = jnp.zeros((bp, N_RWKV_HEADS, HEAD_DIM, HEAD_DIM), jnp.float32)
        yp, k1, v1, w1, s1 = _layer(yp, _swa_prompt, zero_shift, zero_wkv, lp)
        ys, k2, v2, w2, s2 = _layer(ys, functools.partial(_swa_cached, cache_k[l], cache_v[l]),
                                    state_shift[l], state_wkv[l], lp)
        pk.append(k1); pv.append(v1); pw.append(w1); psh.append(s1)
        sk.append(k2); sv.append(v2); sw.append(w2); ssh.append(s2)
    return (yp, ys, jnp.stack(pk), jnp.stack(pv), jnp.stack(pw), jnp.stack(psh),
            jnp.stack(sk), jnp.stack(sv), jnp.stack(sw), jnp.stack(ssh))
```

```python
import functools

import jax
import jax.numpy as jnp
from jax import lax
from jax.experimental import pallas as pl
from jax.experimental.pallas import tpu as pltpu

D_MODEL = 1024
HEAD_DIM = 64
ATTN_WIDTH = 512
N_Q_HEADS = 8
N_KV_HEADS = 2
KV_WIDTH = 128
RWKV_WIDTH = 512
N_RWKV_HEADS = 8
WINDOW = 128
SCALE = HEAD_DIM ** -0.5
DECAY_LORA = 64
AAA_LORA = 64
GATE_LORA = 128
D_FF = 4 * D_MODEL
RMS_EPS = 1e-6
GN_EPS = 64e-5
RWKV_OFF = ATTN_WIDTH + 2 * KV_WIDTH
RWKV_PROJ = 3 * RWKV_WIDTH + DECAY_LORA + AAA_LORA + GATE_LORA
IN_WIDTH = RWKV_OFF + RWKV_PROJ
LANES = 128
SUBLANES = 8
VMEM_LIMIT = 56 * 1024 * 1024

F32 = jnp.float32
BF16 = jnp.bfloat16


def _seg_sum(x, ones_blk):
    hi = x.astype(BF16)
    lo = (x - hi.astype(F32)).astype(BF16)
    return (jnp.dot(hi, ones_blk, preferred_element_type=F32)
            + jnp.dot(lo, ones_blk, preferred_element_type=F32))


def _pre_body(x_ref, pprev_ref, g1_ref, win_ref, mu_ref, qg_ref, kg_ref, j512_ref, j128_ref,
              wcomb_ref, w0_ref, a0_ref, gup_ref, kk_ref, ka_ref, rk_ref,
              q_o, k_o, v_o, r_o, d_o, kh_o, vh_o, a_o, b_o, g_o, bon_o, h_o,
              carry, *, tm, tiles_per_seq, seq_len):
    x = x_ref[...]
    ms = jnp.mean(x * x, axis=-1, keepdims=True)
    h = x * lax.rsqrt(ms + RMS_EPS) * g1_ref[...]
    proj = jnp.dot(h.astype(BF16), win_ref[...], preferred_element_type=F32)

    j512 = j512_ref[...]
    q = proj[:, 0:ATTN_WIDTH]
    qs = _seg_sum(q * q, j512) * (1.0 / HEAD_DIM)
    q_o[...] = q * lax.rsqrt(qs + RMS_EPS) * qg_ref[...]
    k = proj[:, ATTN_WIDTH:ATTN_WIDTH + KV_WIDTH]
    ks = _seg_sum(k * k, j128_ref[...]) * (1.0 / HEAD_DIM)
    k_o[...] = k * lax.rsqrt(ks + RMS_EPS) * kg_ref[...]
    v_o[...] = proj[:, ATTN_WIDTH + KV_WIDTH:RWKV_OFF]

    p = proj[:, RWKV_OFF:]
    rolled = pltpu.roll(p, shift=1, axis=0)
    row = lax.broadcasted_iota(jnp.int32, (tm, 1), 0)
    if tiles_per_seq is not None:
        t = pl.program_id(0) % tiles_per_seq
        first = jnp.where(t == 0, pprev_ref[0], carry[...])
        p_shift = jnp.where(row == 0, first, rolled)
        carry[...] = p[tm - 1:tm, :]

        @pl.when(t == tiles_per_seq - 1)
        def _():
            h_o[0] = h[tm - 1:tm, :]
    else:
        p_shift = jnp.where(row % seq_len == 0, pprev_ref[...], rolled)
        h_o[...] = h
    pm = p + (p_shift - p) * mu_ref[...]

    xr = pm[:, 0:512]
    xk = pm[:, 512:1024]
    xv = pm[:, 1024:1536]
    xwa = pm[:, 1536:1664]
    xg = pm[:, 1664:1792]
    lane = lax.broadcasted_iota(jnp.int32, (1, LANES), 1)
    wa_in = jnp.where(lane < DECAY_LORA, jnp.tanh(xwa), xwa)
    pre = jnp.dot(wa_in.astype(BF16), wcomb_ref[...], preferred_element_type=F32)
    w_pre = pre[:, 0:512] + w0_ref[...]
    a_pre = pre[:, 512:1024] + a0_ref[...]
    z = -w_pre
    softplus = jnp.maximum(z, 0.0) + jnp.log(1.0 + jnp.exp(-jnp.abs(z)))
    w_log = -softplus - 0.5
    d_o[...] = jnp.exp(-jnp.exp(w_log))
    a_gate = 1.0 / (1.0 + jnp.exp(-a_pre))
    sg = 1.0 / (1.0 + jnp.exp(-xg))
    g_o[...] = jnp.dot(sg.astype(BF16), gup_ref[...], preferred_element_type=F32)
    kkv = xk * kk_ref[...]
    n2 = _seg_sum(kkv * kkv, j512)
    kkn = kkv / jnp.maximum(jnp.sqrt(n2), 1e-12)
    k_h = xk * (1.0 + (a_gate - 1.0) * ka_ref[...])
    r_o[...] = xr
    kh_o[...] = k_h
    vh_o[...] = xv
    a_o[...] = -kkn
    b_o[...] = kkn * a_gate
    bon_o[...] = _seg_sum(xr * k_h * rk_ref[...], j512) * xv


def _pre_call(x2, pprev, consts, *, tm, seq_len):
    n = x2.shape[0]
    tm = min(tm, n)
    n_tiles = n // tm
    per_seq = seq_len >= tm
    tiles_per_seq = seq_len // tm if per_seq else None
    n_seq = n // seq_len
    full = lambda a: pl.BlockSpec(a.shape, lambda i: (0,) * a.ndim)
    tok = lambda w: pl.BlockSpec((tm, w), lambda i: (i, 0))
    if per_seq:
        pprev_spec = pl.BlockSpec((1, 1, RWKV_PROJ), lambda i: (i // tiles_per_seq, 0, 0))
        h_spec = pl.BlockSpec((1, 1, D_MODEL), lambda i: (i // tiles_per_seq, 0, 0))
        h_shape = jax.ShapeDtypeStruct((n_seq, 1, D_MODEL), F32)
    else:
        pprev_spec = tok(RWKV_PROJ)
        h_spec = tok(D_MODEL)
        h_shape = jax.ShapeDtypeStruct((n, D_MODEL), F32)
    widths = [ATTN_WIDTH, KV_WIDTH, KV_WIDTH] + [RWKV_WIDTH] * 8
    out_shape = [jax.ShapeDtypeStruct((n, w), F32) for w in widths] + [h_shape]
    out_specs = [tok(w) for w in widths] + [h_spec]
    body = functools.partial(_pre_body, tm=tm, tiles_per_seq=tiles_per_seq, seq_len=seq_len)
    return pl.pallas_call(
        body,
        grid=(n_tiles,),
        in_specs=[tok(D_MODEL), pprev_spec] + [full(c) for c in consts],
        out_specs=out_specs,
        out_shape=out_shape,
        scratch_shapes=[pltpu.VMEM((1, RWKV_PROJ), F32)],
        compiler_params=pltpu.CompilerParams(dimension_semantics=("arbitrary",), vmem_limit_bytes=VMEM_LIMIT),
        name="pre",
    )(x2, pprev, *consts)


def _matmul_body(x_ref, w_ref, o_ref):
    o_ref[...] = jnp.dot(x_ref[...].astype(BF16), w_ref[...], preferred_element_type=F32)


def _matmul_call(x, w):
    return pl.pallas_call(_matmul_body, out_shape=jax.ShapeDtypeStruct((x.shape[0], w.shape[1]), F32),
                          name="shift_proj")(x, w)


def _softmax_sink(s, valid, sink):
    s = jnp.where(valid, s, -jnp.inf)
    m = jnp.maximum(jnp.max(s, axis=-1, keepdims=True), sink)
    p = jnp.exp(s - m)
    den = jnp.sum(p, axis=-1, keepdims=True) + jnp.exp(sink - m)
    return p / den


def _attn_prompt_body(sink_ref, q_ref, kc_ref, kp_ref, vc_ref, vp_ref, o_ref):
    n = pl.program_id(1)
    q = q_ref[0]
    kcat = jnp.concatenate([kp_ref[0], kc_ref[0]], axis=0)
    vcat = jnp.concatenate([vp_ref[0], vc_ref[0]], axis=0)
    krot = pltpu.roll(kcat, shift=HEAD_DIM, axis=1)
    vrot = pltpu.roll(vcat, shift=HEAD_DIM, axis=1)
    lane = lax.broadcasted_iota(jnp.int32, (1, LANES), 1)
    lo = lane < HEAD_DIM
    row = lax.broadcasted_iota(jnp.int32, (WINDOW, 2 * WINDOW), 0)
    col = lax.broadcasted_iota(jnp.int32, (WINDOW, 2 * WINDOW), 1)
    valid = (col > row) & (col <= row + WINDOW) & ((col >= WINDOW) | (n > 0))
    for hk in range(N_KV_HEADS):
        src_lo, src_hi = (kcat, krot) if hk == 0 else (krot, kcat)
        k_pads = (jnp.where(lo, src_lo, 0.0).astype(BF16), jnp.where(lo, 0.0, src_hi).astype(BF16))
        vsrc_lo, vsrc_hi = (vcat, vrot) if hk == 0 else (vrot, vcat)
        v_pads = (jnp.where(lo, vsrc_lo, 0.0).astype(BF16), jnp.where(lo, 0.0, vsrc_hi).astype(BF16))
        for pr in range(2):
            pair = hk * 2 + pr
            qp = q[:, pair * LANES:(pair + 1) * LANES].astype(BF16)
            acc = jnp.zeros((WINDOW, LANES), F32)
            for e in range(2):
                s = lax.dot_general(qp, k_pads[e], (((1,), (1,)), ((), ())),
                                    preferred_element_type=F32) * SCALE
                probs = _softmax_sink(s, valid, sink_ref[pair * 2 + e])
                acc = acc + jnp.dot(probs.astype(BF16), v_pads[e], preferred_element_type=F32)
            o_ref[0, :, pair * LANES:(pair + 1) * LANES] = acc


def _attn_prompt_call(sinks, q, k, v):
    b, t, _ = q.shape
    nb = t // WINDOW
    cur = lambda w: pl.BlockSpec((1, WINDOW, w), lambda i, j: (i, j, 0))
    prev = lambda w: pl.BlockSpec((1, WINDOW, w), lambda i, j: (i, jnp.maximum(j - 1, 0), 0))
    return pl.pallas_call(
        _attn_prompt_body,
        grid=(b, nb),
        in_specs=[pl.BlockSpec(memory_space=pltpu.SMEM), cur(ATTN_WIDTH), cur(KV_WIDTH), prev(KV_WIDTH),
                  cur(KV_WIDTH), prev(KV_WIDTH)],
        out_specs=cur(ATTN_WIDTH),
        out_shape=jax.ShapeDtypeStruct((b, t, ATTN_WIDTH), F32),
        compiler_params=pltpu.CompilerParams(dimension_semantics=("arbitrary", "arbitrary")),
        name="attn_prompt",
    )(sinks, q, k, k, v, v)


def _attn_sample_body(sink_ref, q_ref, k_ref, v_ref, o_ref, *, bb, rows, keys, dec_seq):
    r = lax.broadcasted_iota(jnp.int32, (rows, keys), 0)
    c = lax.broadcasted_iota(jnp.int32, (rows, keys), 1)
    t = r % dec_seq
    valid = (c > t) & (c <= t + WINDOW)
    sink = sink_ref[0]

    def one(i, _):
        q = q_ref[i].astype(BF16)
        k = k_ref[i].astype(BF16)
        s = lax.dot_general(q, k, (((1,), (1,)), ((), ())), preferred_element_type=F32) * SCALE
        probs = _softmax_sink(s, valid, sink)
        o_ref[i] = jnp.dot(probs.astype(BF16), v_ref[i].astype(BF16), preferred_element_type=F32)
        return 0

    lax.fori_loop(0, bb, one, 0)


def _attn_sample_call(sink_rows, q, kall, vall, *, dec_seq, bb=32):
    nbh, rows, _ = q.shape
    keys = kall.shape[1]
    per_head = nbh // N_KV_HEADS
    blk = lambda r: pl.BlockSpec((bb, r, HEAD_DIM), lambda i: (i, 0, 0))
    body = functools.partial(_attn_sample_body, bb=bb, rows=rows, keys=keys, dec_seq=dec_seq)
    return pl.pallas_call(
        body,
        grid=(nbh // bb,),
        in_specs=[pl.BlockSpec((1, rows, 1), lambda i: (i * bb // per_head, 0, 0)), blk(rows), blk(keys), blk(keys)],
        out_specs=blk(rows),
        out_shape=jax.ShapeDtypeStruct((nbh, rows, HEAD_DIM), F32),
        compiler_params=pltpu.CompilerParams(dimension_semantics=("arbitrary",)),
        name="attn_sample",
    )(sink_rows, q, kall, vall)


IB_GROUP = 4


def _scan_body(r_ref, d_ref, k_ref, v_ref, a_ref, b_ref, s0_ref, y_ref, st_ref, state, sa, *, tb, n_tb):
    blk = pl.program_id(1)
    n_ib = HEAD_DIM // SUBLANES

    @pl.when(blk == 0)
    def _():
        state[...] = s0_ref[...]

    for ib in range(n_ib):
        rows = pl.ds(ib * SUBLANES, SUBLANES)

        def init_j(j, acc, rows=rows):
            return acc + state[j, rows, :] * a_ref[0, pl.ds(j, 1), :]

        sa[rows, :] = lax.fori_loop(0, HEAD_DIM, init_j, jnp.zeros((SUBLANES, LANES), F32), unroll=8)

    def step(t, _):
        tn = jnp.minimum(t + 1, tb - 1)
        for grp in range(n_ib // IB_GROUP):
            rows = [pl.ds((grp * IB_GROUP + u) * SUBLANES, SUBLANES) for u in range(IB_GROUP)]
            sav = [sa[rw, :] for rw in rows]
            vv = [v_ref[t, rw, :] for rw in rows]

            def jbody(j, carry, rows=rows, sav=sav, vv=vv):
                yacc, sacc = carry
                jr = pl.ds(j, 1)
                dj = d_ref[t, jr, :]
                bj = b_ref[t, jr, :]
                kj = k_ref[t, jr, :]
                rj = r_ref[t, jr, :]
                aj = a_ref[tn, jr, :]
                ny, ns = [], []
                for u in range(IB_GROUP):
                    s_new = state[j, rows[u], :] * dj + sav[u] * bj + vv[u] * kj
                    state[j, rows[u], :] = s_new
                    ny.append(yacc[u] + s_new * rj)
                    ns.append(sacc[u] + s_new * aj)
                return tuple(ny), tuple(ns)

            zero = tuple(jnp.zeros((SUBLANES, LANES), F32) for _ in range(IB_GROUP))
            yacc, sacc = lax.fori_loop(0, HEAD_DIM, jbody, (zero, zero), unroll=8)
            for u in range(IB_GROUP):
                y_ref[t, rows[u], :] = yacc[u]
                sa[rows[u], :] = sacc[u]
        return 0

    lax.fori_loop(0, tb, step, 0)

    @pl.when(blk == n_tb - 1)
    def _():
        st_ref[...] = state[...]


def _scan_call(r, d, k, v, a, b, s0, *, tb):
    t_len, _, c = r.shape
    n_tb = t_len // tb
    groups = c // LANES
    op = pl.BlockSpec((tb, HEAD_DIM, LANES), lambda g, i: (i, 0, g))
    st = pl.BlockSpec((HEAD_DIM, HEAD_DIM, LANES), lambda g, i: (0, 0, g))
    body = functools.partial(_scan_body, tb=tb, n_tb=n_tb)
    return pl.pallas_call(
        body,
        grid=(groups, n_tb),
        in_specs=[op] * 6 + [st],
        out_specs=[op, st],
        out_shape=[jax.ShapeDtypeStruct((t_len, HEAD_DIM, c), F32), jax.ShapeDtypeStruct((HEAD_DIM, HEAD_DIM, c), F32)],
        scratch_shapes=[pltpu.VMEM((HEAD_DIM, HEAD_DIM, LANES), F32), pltpu.VMEM((HEAD_DIM, LANES), F32)],
        compiler_params=pltpu.CompilerParams(dimension_semantics=("arbitrary", "arbitrary"),
                                             vmem_limit_bytes=VMEM_LIMIT),
        name="wkv_scan",
    )(r, d, k, v, a, b, s0)


def _post_body(x_ref, at_ref, y_ref, g_ref, bon_ref, j512_ref, lng_ref, lnb_ref, wout_ref, g2_ref, wup_ref,
               wdn_ref, o_ref, *, ff_chunk):
    j512 = j512_ref[...]
    y = y_ref[...]
    mu = _seg_sum(y, j512) * (1.0 / HEAD_DIM)
    yc = y - mu
    var = _seg_sum(yc * yc, j512) * (1.0 / HEAD_DIM)
    yn = yc * lax.rsqrt(var + GN_EPS) * lng_ref[...] + lnb_ref[...]
    rw = (yn + bon_ref[...]) * g_ref[...]
    x1 = (x_ref[...]
          + jnp.dot(at_ref[...].astype(BF16), wout_ref[0:ATTN_WIDTH, :], preferred_element_type=F32)
          + jnp.dot(rw.astype(BF16), wout_ref[ATTN_WIDTH:D_MODEL, :], preferred_element_type=F32))
    ms = jnp.mean(x1 * x1, axis=-1, keepdims=True)
    h2 = (x1 * lax.rsqrt(ms + RMS_EPS) * g2_ref[...]).astype(BF16)
    o_ref[...] = x1
    for c in range(D_FF // ff_chunk):
        u = jnp.dot(h2, wup_ref[:, c * ff_chunk:(c + 1) * ff_chunk], preferred_element_type=F32)
        u = jnp.maximum(u, 0.0)
        o_ref[...] += jnp.dot((u * u).astype(BF16), wdn_ref[c * ff_chunk:(c + 1) * ff_chunk, :],
                              preferred_element_type=F32)


def _post_call(x2, attn, y, g, bon, consts, *, tm, ff_chunk=1024):
    n = x2.shape[0]
    tm = min(tm, n)
    tok = lambda w: pl.BlockSpec((tm, w), lambda i: (i, 0))
    full = lambda a: pl.BlockSpec(a.shape, lambda i: (0,) * a.ndim, pipeline_mode=pl.Buffered(1))
    return pl.pallas_call(
        functools.partial(_post_body, ff_chunk=ff_chunk),
        grid=(n // tm,),
        in_specs=[tok(D_MODEL), tok(ATTN_WIDTH), tok(RWKV_WIDTH), tok(RWKV_WIDTH), tok(RWKV_WIDTH)]
        + [full(c) for c in consts],
        out_specs=tok(D_MODEL),
        out_shape=jax.ShapeDtypeStruct((n, D_MODEL), F32),
        compiler_params=pltpu.CompilerParams(dimension_semantics=("arbitrary",), vmem_limit_bytes=VMEM_LIMIT),
        name="post",
    )(x2, attn, y, g, bon, *consts)


def _to_chain_major(x, b, t):
    return x.reshape(b, t, N_RWKV_HEADS, HEAD_DIM).transpose(1, 3, 0, 2).reshape(t, HEAD_DIM, b * N_RWKV_HEADS)


def _from_chain_major(y, b, t):
    return y.reshape(t, HEAD_DIM, b, N_RWKV_HEADS).transpose(2, 0, 3, 1).reshape(b * t, RWKV_WIDTH)


def _layer(x, p_prev_rows, s0, attn_fn, w, *, tm_pre, tm_post, scan_tb):
    b, t, _ = x.shape
    x2 = x.reshape(b * t, D_MODEL)
    outs = _pre_call(x2, p_prev_rows, w["pre"], tm=tm_pre, seq_len=t)
    q, k, v, r, d, kh, vh, av, bv, g, bon, h_last = outs
    attn, new_k, new_v = attn_fn(q, k, v)
    chains = b * N_RWKV_HEADS
    if s0 is None:
        s0c = jnp.zeros((HEAD_DIM, HEAD_DIM, chains), F32)
    else:
        s0c = s0.astype(F32).transpose(3, 2, 0, 1).reshape(HEAD_DIM, HEAD_DIM, chains)
    ops = [_to_chain_major(a, b, t) for a in (r, d, kh, vh, av, bv)]
    y_c, st_c = _scan_call(*ops, s0c, tb=scan_tb)
    y = _from_chain_major(y_c, b, t)
    s_new = st_c.reshape(HEAD_DIM, HEAD_DIM, b, N_RWKV_HEADS).transpose(2, 3, 1, 0)
    out = _post_call(x2, attn, y, g, bon, w["post"], tm=tm_post)
    return out.reshape(b, t, D_MODEL), new_k, new_v, s_new, h_last


def _prep_weights(norm1_g, w_in, q_norm_g, k_norm_g, rwkv_mu, w_decay_0, w_decay_up, a_0, a_up, g_up, k_k, k_a,
                  r_k, ln_x_g, ln_x_b, w_out, norm2_g, w_ff_up, w_ff_down):
    ar = jnp.arange
    o = RWKV_OFF
    perm = jnp.concatenate([ar(0, o + 512), ar(o + 576, o + 1088), ar(o + 1088, o + 1600), ar(o + 512, o + 576),
                            ar(o + 1600, o + 1664), ar(o + 1664, o + 1792)])
    w_in_p = w_in[:, perm].astype(BF16)
    mu_p = rwkv_mu[perm[o:] - o].reshape(1, RWKV_PROJ)
    row = lambda a: a.reshape(1, -1).astype(F32)
    eye = lambda n: jnp.kron(jnp.eye(n, dtype=F32), jnp.ones((HEAD_DIM, HEAD_DIM), F32)).astype(BF16)
    j512, j128 = eye(N_RWKV_HEADS), eye(N_KV_HEADS)
    zeros = jnp.zeros((DECAY_LORA, RWKV_WIDTH), F32)
    wcomb = jnp.concatenate([jnp.concatenate([w_decay_up, zeros], axis=1),
                             jnp.concatenate([zeros, a_up], axis=1)], axis=0).astype(BF16)
    pre = [row(norm1_g), w_in_p, mu_p, row(jnp.tile(q_norm_g, N_Q_HEADS)), row(jnp.tile(k_norm_g, N_KV_HEADS)),
           j512, j128, wcomb, row(w_decay_0), row(a_0), g_up.astype(BF16), row(k_k), row(k_a), row(r_k)]
    post = [j512, row(ln_x_g), row(ln_x_b), w_out.astype(BF16), row(norm2_g), w_ff_up.astype(BF16),
            w_ff_down.astype(BF16)]
    return {"pre": pre, "post": post, "w_rwkv": w_in_p[:, RWKV_OFF:]}


def _forward(x_prompt, x_sample, cache_k, cache_v, state_wkv, state_shift, norm1_g, w_in, q_norm_g, k_norm_g,
             attn_sinks, rwkv_mu, w_decay_0, w_decay_up, a_0, a_up, g_up, k_k, k_a, r_k, ln_x_g, ln_x_b, w_out,
             norm2_g, w_ff_up, w_ff_down, *, tm_pre=256, tm_post=512, scan_tb=32):
    depth = norm1_g.shape[0]
    assert depth == 1
    l = 0
    w = _prep_weights(norm1_g[l], w_in[l], q_norm_g[l], k_norm_g[l], rwkv_mu[l], w_decay_0[l], w_decay_up[l],
                      a_0[l], a_up[l], g_up[l], k_k[l], k_a[l], r_k[l], ln_x_g[l], ln_x_b[l], w_out[l], norm2_g[l],
                      w_ff_up[l], w_ff_down[l])
    sinks = attn_sinks[l].astype(F32)
    bp, tp, _ = x_prompt.shape
    bs, ts, _ = x_sample.shape

    def attn_prompt(q, k, v):
        q3, k3, v3 = (a.reshape(bp, tp, -1) for a in (q, k, v))
        out = _attn_prompt_call(sinks, q3, k3, v3).reshape(bp * tp, ATTN_WIDTH)
        tail = lambda a: a[:, tp - WINDOW:].reshape(bp, WINDOW, N_KV_HEADS, HEAD_DIM)
        return out, tail(k3), tail(v3)

    pprev_p = jnp.zeros((bp, 1, RWKV_PROJ), F32)
    yp, k1, v1, w1, s1 = _layer(x_prompt, pprev_p, None, attn_prompt, w, tm_pre=tm_pre, tm_post=tm_post,
                                scan_tb=scan_tb)
    s1 = s1.reshape(bp, D_MODEL)

    ck, cv = cache_k[l].astype(F32), cache_v[l].astype(F32)
    rows = (N_Q_HEADS // N_KV_HEADS) * ts
    pad = (-ts) % SUBLANES

    def attn_sample(q, k, v):
        k4 = k.reshape(bs, ts, N_KV_HEADS, HEAD_DIM)
        v4 = v.reshape(bs, ts, N_KV_HEADS, HEAD_DIM)
        kall = jnp.concatenate([ck, k4], axis=1)
        vall = jnp.concatenate([cv, v4], axis=1)
        head_major = lambda a: jnp.pad(a, ((0, 0), (0, pad), (0, 0), (0, 0))).transpose(2, 0, 1, 3).reshape(
            N_KV_HEADS * bs, WINDOW + ts + pad, HEAD_DIM)
        q5 = q.reshape(bs, ts, N_KV_HEADS, N_Q_HEADS // N_KV_HEADS, HEAD_DIM).transpose(2, 0, 3, 1, 4)
        q5 = q5.reshape(N_KV_HEADS * bs, rows, HEAD_DIM)
        sink_rows = jnp.repeat(sinks.reshape(N_KV_HEADS, N_Q_HEADS // N_KV_HEADS), ts, axis=1)[..., None]
        o = _attn_sample_call(sink_rows, q5, head_major(kall), head_major(vall), dec_seq=ts)
        o = o.reshape(N_KV_HEADS, bs, N_Q_HEADS // N_KV_HEADS, ts, HEAD_DIM).transpose(1, 3, 0, 2, 4)
        return o.reshape(bs * ts, ATTN_WIDTH), kall[:, ts:], vall[:, ts:]

    p_prev = _matmul_call(state_shift[l], w["w_rwkv"])
    pprev_s = jnp.repeat(p_prev, ts, axis=0)
    ys, k2, v2, w2, h_all = _layer(x_sample, pprev_s, state_wkv[l], attn_sample, w, tm_pre=tm_pre,
                                   tm_post=tm_post, scan_tb=ts)
    s2 = h_all.reshape(bs, ts, D_MODEL)[:, -1]
    st = lambda a: a[None]
    return (yp, ys, st(k1), st(v1), st(w1), st(s1), st(k2), st(v2), st(w2), st(s2))


def kernel(x_prompt, x_sample, cache_k, cache_v, state_wkv, state_shift, norm1_g, w_in, q_norm_g, k_norm_g, attn_sinks, rwkv_mu, w_decay_0, w_decay_up, a_0, a_up, g_up, k_k, k_a, r_k, ln_x_g, ln_x_b, w_out, norm2_g, w_ff_up, w_ff_down):
    return _forward(x_prompt, x_sample, cache_k, cache_v, state_wkv, state_shift, norm1_g, w_in, q_norm_g, k_norm_g,
                    attn_sinks, rwkv_mu, w_decay_0, w_decay_up, a_0, a_up, g_up, k_k, k_a, r_k, ln_x_g, ln_x_b,
                    w_out, norm2_g, w_ff_up, w_ff_down)
```

```python
import functools

import jax
import jax.numpy as jnp
from jax import lax
from jax.experimental import pallas as pl
from jax.experimental.pallas import tpu as pltpu

D_MODEL = 1024
HEAD_DIM = 64
ATTN_WIDTH = 512
N_Q_HEADS = 8
N_KV_HEADS = 2
KV_WIDTH = 128
RWKV_WIDTH = 512
N_RWKV_HEADS = 8
WINDOW = 128
SCALE = HEAD_DIM ** -0.5
DECAY_LORA = 64
AAA_LORA = 64
GATE_LORA = 128
D_FF = 4 * D_MODEL
RMS_EPS = 1e-6
GN_EPS = 64e-5
RWKV_OFF = ATTN_WIDTH + 2 * KV_WIDTH
RWKV_PROJ = 3 * RWKV_WIDTH + DECAY_LORA + AAA_LORA + GATE_LORA
IN_WIDTH = RWKV_OFF + RWKV_PROJ
LANES = 128
SUBLANES = 8
VMEM_LIMIT = 56 * 1024 * 1024

F32 = jnp.float32
BF16 = jnp.bfloat16


def _seg_sum(x, ones_blk):
    hi = x.astype(BF16)
    lo = (x - hi.astype(F32)).astype(BF16)
    return (jnp.dot(hi, ones_blk, preferred_element_type=F32)
            + jnp.dot(lo, ones_blk, preferred_element_type=F32))


def _pre_body(x_ref, pprev_ref, g1_ref, win_ref, mu_ref, qg_ref, kg_ref, j512_ref, j128_ref,
              wcomb_ref, w0_ref, a0_ref, gup_ref, kk_ref, ka_ref, rk_ref,
              q_o, k_o, v_o, r_o, d_o, kh_o, vh_o, a_o, b_o, g_o, bon_o, h_o,
              carry, *, tm, tiles_per_seq, seq_len):
    x = x_ref[...]
    ms = jnp.mean(x * x, axis=-1, keepdims=True)
    h = x * lax.rsqrt(ms + RMS_EPS) * g1_ref[...]
    proj = jnp.dot(h.astype(BF16), win_ref[...], preferred_element_type=F32)

    j512 = j512_ref[...]
    q = proj[:, 0:ATTN_WIDTH]
    qs = _seg_sum(q * q, j512) * (1.0 / HEAD_DIM)
    q_o[...] = q * lax.rsqrt(qs + RMS_EPS) * qg_ref[...]
    k = proj[:, ATTN_WIDTH:ATTN_WIDTH + KV_WIDTH]
    ks = _seg_sum(k * k, j128_ref[...]) * (1.0 / HEAD_DIM)
    k_o[...] = k * lax.rsqrt(ks + RMS_EPS) * kg_ref[...]
    v_o[...] = proj[:, ATTN_WIDTH + KV_WIDTH:RWKV_OFF]

    p = proj[:, RWKV_OFF:]
    rolled = pltpu.roll(p, shift=1, axis=0)
    row = lax.broadcasted_iota(jnp.int32, (tm, 1), 0)
    if tiles_per_seq is not None:
        t = pl.program_id(0) % tiles_per_seq
        first = jnp.where(t == 0, pprev_ref[0], carry[...])
        p_shift = jnp.where(row == 0, first, rolled)
        carry[...] = p[tm - 1:tm, :]

        @pl.when(t == tiles_per_seq - 1)
        def _():
            h_o[0] = h[tm - 1:tm, :]
    else:
        p_shift = jnp.where(row % seq_len == 0, pprev_ref[...], rolled)
        h_o[...] = h
    pm = p + (p_shift - p) * mu_ref[...]

    xr = pm[:, 0:512]
    xk = pm[:, 512:1024]
    xv = pm[:, 1024:1536]
    xwa = pm[:, 1536:1664]
    xg = pm[:, 1664:1792]
    lane = lax.broadcasted_iota(jnp.int32, (1, LANES), 1)
    wa_in = jnp.where(lane < DECAY_LORA, jnp.tanh(xwa), xwa)
    pre = jnp.dot(wa_in.astype(BF16), wcomb_ref[...], preferred_element_type=F32)
    w_pre = pre[:, 0:512] + w0_ref[...]
    a_pre = pre[:, 512:1024] + a0_ref[...]
    z = -w_pre
    softplus = jnp.maximum(z, 0.0) + jnp.log(1.0 + jnp.exp(-jnp.abs(z)))
    w_log = -softplus - 0.5
    d_o[...] = jnp.exp(-jnp.exp(w_log))
    a_gate = 1.0 / (1.0 + jnp.exp(-a_pre))
    sg = 1.0 / (1.0 + jnp.exp(-xg))
    g_o[...] = jnp.dot(sg.astype(BF16), gup_ref[...], preferred_element_type=F32)
    kkv = xk * kk_ref[...]
    n2 = _seg_sum(kkv * kkv, j512)
    kkn = kkv / jnp.maximum(jnp.sqrt(n2), 1e-12)
    k_h = xk * (1.0 + (a_gate - 1.0) * ka_ref[...])
    r_o[...] = xr
    kh_o[...] = k_h
    vh_o[...] = xv
    a_o[...] = -kkn
    b_o[...] = kkn * a_gate
    bon_o[...] = _seg_sum(xr * k_h * rk_ref[...], j512) * xv


def _pre_call(x2, pprev, consts, *, tm, seq_len):
    n = x2.shape[0]
    tm = min(tm, n)
    n_tiles = n // tm
    per_seq = seq_len >= tm
    tiles_per_seq = seq_len // tm if per_seq else None
    n_seq = n // seq_len
    full = lambda a: pl.BlockSpec(a.shape, lambda i: (0,) * a.ndim)
    tok = lambda w: pl.BlockSpec((tm, w), lambda i: (i, 0))
    if per_seq:
        pprev_spec = pl.BlockSpec((1, 1, RWKV_PROJ), lambda i: (i // tiles_per_seq, 0, 0))
        h_spec = pl.BlockSpec((1, 1, D_MODEL), lambda i: (i // tiles_per_seq, 0, 0))
        h_shape = jax.ShapeDtypeStruct((n_seq, 1, D_MODEL), F32)
    else:
        pprev_spec = tok(RWKV_PROJ)
        h_spec = tok(D_MODEL)
        h_shape = jax.ShapeDtypeStruct((n, D_MODEL), F32)
    widths = [ATTN_WIDTH, KV_WIDTH, KV_WIDTH] + [RWKV_WIDTH] * 8
    out_shape = [jax.ShapeDtypeStruct((n, w), F32) for w in widths] + [h_shape]
    out_specs = [tok(w) for w in widths] + [h_spec]
    body = functools.partial(_pre_body, tm=tm, tiles_per_seq=tiles_per_seq, seq_len=seq_len)
    return pl.pallas_call(
        body,
        grid=(n_tiles,),
        in_specs=[tok(D_MODEL), pprev_spec] + [full(c) for c in consts],
        out_specs=out_specs,
        out_shape=out_shape,
        scratch_shapes=[pltpu.VMEM((1, RWKV_PROJ), F32)],
        compiler_params=pltpu.CompilerParams(dimension_semantics=("arbitrary",), vmem_limit_bytes=VMEM_LIMIT),
        name="pre",
    )(x2, pprev, *consts)


def _matmul_body(x_ref, w_ref, o_ref):
    o_ref[...] = jnp.dot(x_ref[...].astype(BF16), w_ref[...], preferred_element_type=F32)


def _matmul_call(x, w):
    return pl.pallas_call(_matmul_body, out_shape=jax.ShapeDtypeStruct((x.shape[0], w.shape[1]), F32),
                          name="shift_proj")(x, w)


def _softmax_sink(s, valid, sink):
    s = jnp.where(valid, s, -jnp.inf)
    m = jnp.maximum(jnp.max(s, axis=-1, keepdims=True), sink)
    p = jnp.exp(s - m)
    den = jnp.sum(p, axis=-1, keepdims=True) + jnp.exp(sink - m)
    return p / den


def _attn_prompt_body(sink_ref, q_ref, kc_ref, kp_ref, vc_ref, vp_ref, o_ref):
    n = pl.program_id(1)
    q = q_ref[0]
    kcat = jnp.concatenate([kp_ref[0], kc_ref[0]], axis=0)
    vcat = jnp.concatenate([vp_ref[0], vc_ref[0]], axis=0)
    krot = pltpu.roll(kcat, shift=HEAD_DIM, axis=1)
    vrot = pltpu.roll(vcat, shift=HEAD_DIM, axis=1)
    lane = lax.broadcasted_iota(jnp.int32, (1, LANES), 1)
    lo = lane < HEAD_DIM
    row = lax.broadcasted_iota(jnp.int32, (WINDOW, 2 * WINDOW), 0)
    col = lax.broadcasted_iota(jnp.int32, (WINDOW, 2 * WINDOW), 1)
    valid = (col > row) & (col <= row + WINDOW) & ((col >= WINDOW) | (n > 0))
    for hk in range(N_KV_HEADS):
        src_lo, src_hi = (kcat, krot) if hk == 0 else (krot, kcat)
        k_pads = (jnp.where(lo, src_lo, 0.0).astype(BF16), jnp.where(lo, 0.0, src_hi).astype(BF16))
        vsrc_lo, vsrc_hi = (vcat, vrot) if hk == 0 else (vrot, vcat)
        v_pads = (jnp.where(lo, vsrc_lo, 0.0).astype(BF16), jnp.where(lo, 0.0, vsrc_hi).astype(BF16))
        for pr in range(2):
            pair = hk * 2 + pr
            qp = q[:, pair * LANES:(pair + 1) * LANES].astype(BF16)
            acc = jnp.zeros((WINDOW, LANES), F32)
            for e in range(2):
                s = lax.dot_general(qp, k_pads[e], (((1,), (1,)), ((), ())),
                                    preferred_element_type=F32) * SCALE
                probs = _softmax_sink(s, valid, sink_ref[pair * 2 + e])
                acc = acc + jnp.dot(probs.astype(BF16), v_pads[e], preferred_element_type=F32)
            o_ref[0, :, pair * LANES:(pair + 1) * LANES] = acc


def _attn_prompt_call(sinks, q, k, v):
    b, t, _ = q.shape
    nb = t // WINDOW
    cur = lambda w: pl.BlockSpec((1, WINDOW, w), lambda i, j: (i, j, 0))
    prev = lambda w: pl.BlockSpec((1, WINDOW, w), lambda i, j: (i, jnp.maximum(j - 1, 0), 0))
    return pl.pallas_call(
        _attn_prompt_body,
        grid=(b, nb),
        in_specs=[pl.BlockSpec(memory_space=pltpu.SMEM), cur(ATTN_WIDTH), cur(KV_WIDTH), prev(KV_WIDTH),
                  cur(KV_WIDTH), prev(KV_WIDTH)],
        out_specs=cur(ATTN_WIDTH),
        out_shape=jax.ShapeDtypeStruct((b, t, ATTN_WIDTH), F32),
        compiler_params=pltpu.CompilerParams(dimension_semantics=("arbitrary", "arbitrary")),
        name="attn_prompt",
    )(sinks, q, k, k, v, v)


def _attn_sample_body(sink_ref, q_ref, k_ref, v_ref, o_ref, *, bb, rows, keys, dec_seq):
    r = lax.broadcasted_iota(jnp.int32, (rows, keys), 0)
    c = lax.broadcasted_iota(jnp.int32, (rows, keys), 1)
    t = r % dec_seq
    valid = (c > t) & (c <= t + WINDOW)
    sink = sink_ref[0]

    def one(i, _):
        q = q_ref[i].astype(BF16)
        k = k_ref[i].astype(BF16)
        s = lax.dot_general(q, k, (((1,), (1,)), ((), ())), preferred_element_type=F32) * SCALE
        probs = _softmax_sink(s, valid, sink)
        o_ref[i] = jnp.dot(probs.astype(BF16), v_ref[i].astype(BF16), preferred_element_type=F32)
        return 0

    lax.fori_loop(0, bb, one, 0)


def _attn_sample_call(sink_rows, q, kall, vall, *, dec_seq, bb=32):
    nbh, rows, _ = q.shape
    keys = kall.shape[1]
    per_head = nbh // N_KV_HEADS
    blk = lambda r: pl.BlockSpec((bb, r, HEAD_DIM), lambda i: (i, 0, 0))
    body = functools.partial(_attn_sample_body, bb=bb, rows=rows, keys=keys, dec_seq=dec_seq)
    return pl.pallas_call(
        body,
        grid=(nbh // bb,),
        in_specs=[pl.BlockSpec((1, rows, 1), lambda i: (i * bb // per_head, 0, 0)), blk(rows), blk(keys), blk(keys)],
        out_specs=blk(rows),
        out_shape=jax.ShapeDtypeStruct((nbh, rows, HEAD_DIM), F32),
        compiler_params=pltpu.CompilerParams(dimension_semantics=("arbitrary",)),
        name="attn_sample",
    )(sink_rows, q, kall, vall)


IB_GROUP = 4


def _scan_body(*refs, tb, n_tb, zero_init):
    n_in = 6 if zero_init else 7
    raw = refs[:6]
    s0_ref = None if zero_init else refs[6]
    y_out, st_ref = refs[n_in:n_in + 2]
    r_ref, d_ref, k_ref, v_ref, a_ref, b_ref, y_ref, state, sa = refs[n_in + 2:]
    blk = pl.program_id(1)
    n_ib = HEAD_DIM // SUBLANES
    chains = (LANES // N_RWKV_HEADS, N_RWKV_HEADS, HEAD_DIM)

    @pl.when(blk == 0)
    def _():
        if zero_init:
            state[...] = jnp.zeros(state.shape, F32)
        else:
            def load_i(i, _):
                state[:, i, :] = s0_ref[:, :, i, :].reshape(LANES, HEAD_DIM).T
                return 0

            lax.fori_loop(0, HEAD_DIM, load_i, 0)

    def relayout(t, _):
        for src, dst in zip(raw, (r_ref, d_ref, k_ref, v_ref, a_ref, b_ref)):
            dst[t] = src[:, t].reshape(LANES, HEAD_DIM).T
        return 0

    lax.fori_loop(0, tb, relayout, 0, unroll=min(tb, 4))

    for ib in range(n_ib):
        rows = pl.ds(ib * SUBLANES, SUBLANES)

        def init_j(j, acc, rows=rows):
            return acc + state[j, rows, :] * a_ref[0, pl.ds(j, 1), :]

        sa[rows, :] = lax.fori_loop(0, HEAD_DIM, init_j, jnp.zeros((SUBLANES, LANES), F32), unroll=8)

    def step(t, _):
        tn = jnp.minimum(t + 1, tb - 1)
        for grp in range(n_ib // IB_GROUP):
            rows = [pl.ds((grp * IB_GROUP + u) * SUBLANES, SUBLANES) for u in range(IB_GROUP)]
            sav = [sa[rw, :] for rw in rows]
            vv = [v_ref[t, rw, :] for rw in rows]

            def jbody(j, carry, rows=rows, sav=sav, vv=vv):
                yacc, sacc = carry
                jr = pl.ds(j, 1)
                dj = d_ref[t, jr, :]
                bj = b_ref[t, jr, :]
                kj = k_ref[t, jr, :]
                rj = r_ref[t, jr, :]
                aj = a_ref[tn, jr, :]
                ny, ns = [], []
                for u in range(IB_GROUP):
                    s_new = state[j, rows[u], :] * dj + sav[u] * bj + vv[u] * kj
                    state[j, rows[u], :] = s_new
                    ny.append(yacc[u] + s_new * rj)
                    ns.append(sacc[u] + s_new * aj)
                return tuple(ny), tuple(ns)

            zero = tuple(jnp.zeros((SUBLANES, LANES), F32) for _ in range(IB_GROUP))
            yacc, sacc = lax.fori_loop(0, HEAD_DIM, jbody, (zero, zero), unroll=8)
            for u in range(IB_GROUP):
                y_ref[t, rows[u], :] = yacc[u]
                sa[rows[u], :] = sacc[u]
        return 0

    lax.fori_loop(0, tb, step, 0)

    def writeback(t, _):
        y_out[:, t] = y_ref[t].T.reshape(chains)
        return 0

    lax.fori_loop(0, tb, writeback, 0, unroll=min(tb, 4))

    @pl.when(blk == n_tb - 1)
    def _():
        def store_i(i, _):
            st_ref[:, :, i, :] = state[:, i, :].T.reshape(chains)
            return 0

        lax.fori_loop(0, HEAD_DIM, store_i, 0)


def _scan_call(r, d, k, v, a, b, s0, *, tb):
    bsz, t_len = r.shape[0], r.shape[1]
    bg = LANES // N_RWKV_HEADS
    n_tb = t_len // tb
    op = pl.BlockSpec((bg, tb, N_RWKV_HEADS, HEAD_DIM), lambda g, i: (g, i, 0, 0))
    st = pl.BlockSpec((bg, N_RWKV_HEADS, HEAD_DIM, HEAD_DIM), lambda g, i: (g, 0, 0, 0))
    zero_init = s0 is None
    body = functools.partial(_scan_body, tb=tb, n_tb=n_tb, zero_init=zero_init)
    chain_major = pltpu.VMEM((tb, HEAD_DIM, LANES), F32)
    return pl.pallas_call(
        body,
        grid=(bsz // bg, n_tb),
        in_specs=[op] * 6 + ([] if zero_init else [st]),
        out_specs=[op, st],
        out_shape=[jax.ShapeDtypeStruct(r.shape, F32),
                   jax.ShapeDtypeStruct((bsz, N_RWKV_HEADS, HEAD_DIM, HEAD_DIM), F32)],
        scratch_shapes=[chain_major] * 7 + [pltpu.VMEM((HEAD_DIM, HEAD_DIM, LANES), F32),
                                            pltpu.VMEM((HEAD_DIM, LANES), F32)],
        compiler_params=pltpu.CompilerParams(dimension_semantics=("arbitrary", "arbitrary"),
                                             vmem_limit_bytes=VMEM_LIMIT),
        name="wkv_scan",
    )(r, d, k, v, a, b, *([] if zero_init else [s0]))


def _post_body(x_ref, at_ref, y_ref, g_ref, bon_ref, j512_ref, lng_ref, lnb_ref, wout_ref, g2_ref, wup_ref,
               wdn_ref, o_ref, *, ff_chunk):
    j512 = j512_ref[...]
    y = y_ref[...]
    mu = _seg_sum(y, j512) * (1.0 / HEAD_DIM)
    yc = y - mu
    var = _seg_sum(yc * yc, j512) * (1.0 / HEAD_DIM)
    yn = yc * lax.rsqrt(var + GN_EPS) * lng_ref[...] + lnb_ref[...]
    rw = (yn + bon_ref[...]) * g_ref[...]
    x1 = (x_ref[...]
          + jnp.dot(at_ref[...].astype(BF16), wout_ref[0:ATTN_WIDTH, :], preferred_element_type=F32)
          + jnp.dot(rw.astype(BF16), wout_ref[ATTN_WIDTH:D_MODEL, :], preferred_element_type=F32))
    ms = jnp.mean(x1 * x1, axis=-1, keepdims=True)
    h2 = (x1 * lax.rsqrt(ms + RMS_EPS) * g2_ref[...]).astype(BF16)
    o_ref[...] = x1
    for c in range(D_FF // ff_chunk):
        u = jnp.dot(h2, wup_ref[:, c * ff_chunk:(c + 1) * ff_chunk], preferred_element_type=F32)
        u = jnp.maximum(u, 0.0)
        o_ref[...] += jnp.dot((u * u).astype(BF16), wdn_ref[c * ff_chunk:(c + 1) * ff_chunk, :],
                              preferred_element_type=F32)


def _post_call(x2, attn, y, g, bon, consts, *, tm, ff_chunk=1024):
    n = x2.shape[0]
    tm = min(tm, n)
    tok = lambda w: pl.BlockSpec((tm, w), lambda i: (i, 0))
    full = lambda a: pl.BlockSpec(a.shape, lambda i: (0,) * a.ndim, pipeline_mode=pl.Buffered(1))
    return pl.pallas_call(
        functools.partial(_post_body, ff_chunk=ff_chunk),
        grid=(n // tm,),
        in_specs=[tok(D_MODEL), tok(ATTN_WIDTH), tok(RWKV_WIDTH), tok(RWKV_WIDTH), tok(RWKV_WIDTH)]
        + [full(c) for c in consts],
        out_specs=tok(D_MODEL),
        out_shape=jax.ShapeDtypeStruct((n, D_MODEL), F32),
        compiler_params=pltpu.CompilerParams(dimension_semantics=("arbitrary",), vmem_limit_bytes=VMEM_LIMIT),
        name="post",
    )(x2, attn, y, g, bon, *consts)


def _layer(x, p_prev_rows, s0, attn_fn, w, *, tm_pre, tm_post, scan_tb):
    b, t, _ = x.shape
    x2 = x.reshape(b * t, D_MODEL)
    outs = _pre_call(x2, p_prev_rows, w["pre"], tm=tm_pre, seq_len=t)
    q, k, v, r, d, kh, vh, av, bv, g, bon, h_last = outs
    attn, new_k, new_v = attn_fn(q, k, v)
    ops = [a.reshape(b, t, N_RWKV_HEADS, HEAD_DIM) for a in (r, d, kh, vh, av, bv)]
    y4, s_new = _scan_call(*ops, None if s0 is None else s0.astype(F32), tb=scan_tb)
    y = y4.reshape(b * t, RWKV_WIDTH)
    out = _post_call(x2, attn, y, g, bon, w["post"], tm=tm_post)
    return out.reshape(b, t, D_MODEL), new_k, new_v, s_new, h_last


def _prep_weights(norm1_g, w_in, q_norm_g, k_norm_g, rwkv_mu, w_decay_0, w_decay_up, a_0, a_up, g_up, k_k, k_a,
                  r_k, ln_x_g, ln_x_b, w_out, norm2_g, w_ff_up, w_ff_down):
    ar = jnp.arange
    o = RWKV_OFF
    perm = jnp.concatenate([ar(0, o + 512), ar(o + 576, o + 1088), ar(o + 1088, o + 1600), ar(o + 512, o + 576),
                            ar(o + 1600, o + 1664), ar(o + 1664, o + 1792)])
    w_in_p = w_in[:, perm].astype(BF16)
    mu_p = rwkv_mu[perm[o:] - o].reshape(1, RWKV_PROJ)
    row = lambda a: a.reshape(1, -1).astype(F32)
    eye = lambda n: jnp.kron(jnp.eye(n, dtype=F32), jnp.ones((HEAD_DIM, HEAD_DIM), F32)).astype(BF16)
    j512, j128 = eye(N_RWKV_HEADS), eye(N_KV_HEADS)
    zeros = jnp.zeros((DECAY_LORA, RWKV_WIDTH), F32)
    wcomb = jnp.concatenate([jnp.concatenate([w_decay_up, zeros], axis=1),
                             jnp.concatenate([zeros, a_up], axis=1)], axis=0).astype(BF16)
    pre = [row(norm1_g), w_in_p, mu_p, row(jnp.tile(q_norm_g, N_Q_HEADS)), row(jnp.tile(k_norm_g, N_KV_HEADS)),
           j512, j128, wcomb, row(w_decay_0), row(a_0), g_up.astype(BF16), row(k_k), row(k_a), row(r_k)]
    post = [j512, row(ln_x_g), row(ln_x_b), w_out.astype(BF16), row(norm2_g), w_ff_up.astype(BF16),
            w_ff_down.astype(BF16)]
    return {"pre": pre, "post": post, "w_rwkv": w_in_p[:, RWKV_OFF:]}


def _forward(x_prompt, x_sample, cache_k, cache_v, state_wkv, state_shift, norm1_g, w_in, q_norm_g, k_norm_g,
             attn_sinks, rwkv_mu, w_decay_0, w_decay_up, a_0, a_up, g_up, k_k, k_a, r_k, ln_x_g, ln_x_b, w_out,
             norm2_g, w_ff_up, w_ff_down, *, tm_pre=256, tm_post=512, scan_tb=16):
    depth = norm1_g.shape[0]
    assert depth == 1
    l = 0
    w = _prep_weights(norm1_g[l], w_in[l], q_norm_g[l], k_norm_g[l], rwkv_mu[l], w_decay_0[l], w_decay_up[l],
                      a_0[l], a_up[l], g_up[l], k_k[l], k_a[l], r_k[l], ln_x_g[l], ln_x_b[l], w_out[l], norm2_g[l],
                      w_ff_up[l], w_ff_down[l])
    sinks = attn_sinks[l].astype(F32)
    bp, tp, _ = x_prompt.shape
    bs, ts, _ = x_sample.shape

    def attn_prompt(q, k, v):
        q3, k3, v3 = (a.reshape(bp, tp, -1) for a in (q, k, v))
        out = _attn_prompt_call(sinks, q3, k3, v3).reshape(bp * tp, ATTN_WIDTH)
        tail = lambda a: a[:, tp - WINDOW:].reshape(bp, WINDOW, N_KV_HEADS, HEAD_DIM)
        return out, tail(k3), tail(v3)

    pprev_p = jnp.zeros((bp, 1, RWKV_PROJ), F32)
    yp, k1, v1, w1, s1 = _layer(x_prompt, pprev_p, None, attn_prompt, w, tm_pre=tm_pre, tm_post=tm_post,
                                scan_tb=scan_tb)
    s1 = s1.reshape(bp, D_MODEL)

    ck, cv = cache_k[l].astype(F32), cache_v[l].astype(F32)
    rows = (N_Q_HEADS // N_KV_HEADS) * ts
    pad = (-ts) % SUBLANES

    def attn_sample(q, k, v):
        k4 = k.reshape(bs, ts, N_KV_HEADS, HEAD_DIM)
        v4 = v.reshape(bs, ts, N_KV_HEADS, HEAD_DIM)
        kall = jnp.concatenate([ck, k4], axis=1)
        vall = jnp.concatenate([cv, v4], axis=1)
        head_major = lambda a: jnp.pad(a, ((0, 0), (0, pad), (0, 0), (0, 0))).transpose(2, 0, 1, 3).reshape(
            N_KV_HEADS * bs, WINDOW + ts + pad, HEAD_DIM)
        q5 = q.reshape(bs, ts, N_KV_HEADS, N_Q_HEADS // N_KV_HEADS, HEAD_DIM).transpose(2, 0, 3, 1, 4)
        q5 = q5.reshape(N_KV_HEADS * bs, rows, HEAD_DIM)
        sink_rows = jnp.repeat(sinks.reshape(N_KV_HEADS, N_Q_HEADS // N_KV_HEADS), ts, axis=1)[..., None]
        o = _attn_sample_call(sink_rows, q5, head_major(kall), head_major(vall), dec_seq=ts)
        o = o.reshape(N_KV_HEADS, bs, N_Q_HEADS // N_KV_HEADS, ts, HEAD_DIM).transpose(1, 3, 0, 2, 4)
        return o.reshape(bs * ts, ATTN_WIDTH), kall[:, ts:], vall[:, ts:]

    p_prev = _matmul_call(state_shift[l], w["w_rwkv"])
    pprev_s = jnp.repeat(p_prev, ts, axis=0)
    ys, k2, v2, w2, h_all = _layer(x_sample, pprev_s, state_wkv[l], attn_sample, w, tm_pre=tm_pre,
                                   tm_post=tm_post, scan_tb=ts)
    s2 = h_all.reshape(bs, ts, D_MODEL)[:, -1]
    st = lambda a: a[None]
    return (yp, ys, st(k1), st(v1), st(w1), st(s1), st(k2), st(v2), st(w2), st(s2))


def kernel(x_prompt, x_sample, cache_k, cache_v, state_wkv, state_shift, norm1_g, w_in, q_norm_g, k_norm_g, attn_sinks, rwkv_mu, w_decay_0, w_decay_up, a_0, a_up, g_up, k_k, k_a, r_k, ln_x_g, ln_x_b, w_out, norm2_g, w_ff_up, w_ff_down):
    return _forward(x_prompt, x_sample, cache_k, cache_v, state_wkv, state_shift, norm1_g, w_in, q_norm_g, k_norm_g,
                    attn_sinks, rwkv_mu, w_decay_0, w_decay_up, a_0, a_up, g_up, k_k, k_a, r_k, ln_x_g, ln_x_b,
                    w_out, norm2_g, w_ff_up, w_ff_down)
```

```python
import functools

import jax
import jax.numpy as jnp
from jax import lax
from jax.experimental import pallas as pl
from jax.experimental.pallas import tpu as pltpu

D_MODEL = 1024
HEAD_DIM = 64
ATTN_WIDTH = 512
N_Q_HEADS = 8
N_KV_HEADS = 2
KV_WIDTH = 128
RWKV_WIDTH = 512
N_RWKV_HEADS = 8
WINDOW = 128
SCALE = HEAD_DIM ** -0.5
DECAY_LORA = 64
AAA_LORA = 64
GATE_LORA = 128
D_FF = 4 * D_MODEL
RMS_EPS = 1e-6
GN_EPS = 64e-5
RWKV_OFF = ATTN_WIDTH + 2 * KV_WIDTH
RWKV_PROJ = 3 * RWKV_WIDTH + DECAY_LORA + AAA_LORA + GATE_LORA
IN_WIDTH = RWKV_OFF + RWKV_PROJ
LANES = 128
SUBLANES = 8
VMEM_LIMIT = 56 * 1024 * 1024

F32 = jnp.float32
BF16 = jnp.bfloat16


def _seg_sum(x, ones_blk):
    hi = x.astype(BF16)
    lo = (x - hi.astype(F32)).astype(BF16)
    return (jnp.dot(hi, ones_blk, preferred_element_type=F32)
            + jnp.dot(lo, ones_blk, preferred_element_type=F32))


def _pre_body(x_ref, pprev_ref, g1_ref, win_ref, mu_ref, qg_ref, kg_ref, j512_ref, j128_ref,
              wcomb_ref, w0_ref, a0_ref, gup_ref, kk_ref, ka_ref, rk_ref,
              q_o, k_o, v_o, r_o, d_o, kh_o, vh_o, a_o, b_o, g_o, bon_o, h_o,
              carry, *, tm, tiles_per_seq, seq_len):
    x = x_ref[...]
    ms = jnp.mean(x * x, axis=-1, keepdims=True)
    h = x * lax.rsqrt(ms + RMS_EPS) * g1_ref[...]
    proj = jnp.dot(h.astype(BF16), win_ref[...], preferred_element_type=F32)

    j512 = j512_ref[...]
    q = proj[:, 0:ATTN_WIDTH]
    qs = _seg_sum(q * q, j512) * (1.0 / HEAD_DIM)
    q_o[...] = q * lax.rsqrt(qs + RMS_EPS) * qg_ref[...]
    k = proj[:, ATTN_WIDTH:ATTN_WIDTH + KV_WIDTH]
    ks = _seg_sum(k * k, j128_ref[...]) * (1.0 / HEAD_DIM)
    k_o[...] = k * lax.rsqrt(ks + RMS_EPS) * kg_ref[...]
    v_o[...] = proj[:, ATTN_WIDTH + KV_WIDTH:RWKV_OFF]

    p = proj[:, RWKV_OFF:]
    rolled = pltpu.roll(p, shift=1, axis=0)
    row = lax.broadcasted_iota(jnp.int32, (tm, 1), 0)
    if tiles_per_seq is not None:
        t = pl.program_id(0) % tiles_per_seq
        first = jnp.where(t == 0, pprev_ref[0], carry[...])
        p_shift = jnp.where(row == 0, first, rolled)
        carry[...] = p[tm - 1:tm, :]

        @pl.when(t == tiles_per_seq - 1)
        def _():
            h_o[0] = h[tm - 1:tm, :]
    else:
        p_shift = jnp.where(row % seq_len == 0, pprev_ref[...], rolled)
        h_o[...] = h
    pm = p + (p_shift - p) * mu_ref[...]

    xr = pm[:, 0:512]
    xk = pm[:, 512:1024]
    xv = pm[:, 1024:1536]
    xwa = pm[:, 1536:1664]
    xg = pm[:, 1664:1792]
    lane = lax.broadcasted_iota(jnp.int32, (1, LANES), 1)
    wa_in = jnp.where(lane < DECAY_LORA, jnp.tanh(xwa), xwa)
    pre = jnp.dot(wa_in.astype(BF16), wcomb_ref[...], preferred_element_type=F32)
    w_pre = pre[:, 0:512] + w0_ref[...]
    a_pre = pre[:, 512:1024] + a0_ref[...]
    z = -w_pre
    softplus = jnp.maximum(z, 0.0) + jnp.log(1.0 + jnp.exp(-jnp.abs(z)))
    w_log = -softplus - 0.5
    d_o[...] = jnp.exp(-jnp.exp(w_log))
    a_gate = 1.0 / (1.0 + jnp.exp(-a_pre))
    sg = 1.0 / (1.0 + jnp.exp(-xg))
    g_o[...] = jnp.dot(sg.astype(BF16), gup_ref[...], preferred_element_type=F32)
    kkv = xk * kk_ref[...]
    n2 = _seg_sum(kkv * kkv, j512)
    kkn = kkv / jnp.maximum(jnp.sqrt(n2), 1e-12)
    k_h = xk * (1.0 + (a_gate - 1.0) * ka_ref[...])
    r_o[...] = xr
    kh_o[...] = k_h
    vh_o[...] = xv
    a_o[...] = -kkn
    b_o[...] = kkn * a_gate
    bon_o[...] = _seg_sum(xr * k_h * rk_ref[...], j512) * xv


def _pre_call(x2, pprev, consts, *, tm, seq_len):
    n = x2.shape[0]
    tm = min(tm, n)
    n_tiles = n // tm
    per_seq = seq_len >= tm
    tiles_per_seq = seq_len // tm if per_seq else None
    n_seq = n // seq_len
    full = lambda a: pl.BlockSpec(a.shape, lambda i: (0,) * a.ndim)
    tok = lambda w: pl.BlockSpec((tm, w), lambda i: (i, 0))
    if per_seq:
        pprev_spec = pl.BlockSpec((1, 1, RWKV_PROJ), lambda i: (i // tiles_per_seq, 0, 0))
        h_spec = pl.BlockSpec((1, 1, D_MODEL), lambda i: (i // tiles_per_seq, 0, 0))
        h_shape = jax.ShapeDtypeStruct((n_seq, 1, D_MODEL), F32)
    else:
        pprev_spec = tok(RWKV_PROJ)
        h_spec = tok(D_MODEL)
        h_shape = jax.ShapeDtypeStruct((n, D_MODEL), F32)
    widths = [ATTN_WIDTH, KV_WIDTH, KV_WIDTH] + [RWKV_WIDTH] * 8
    out_shape = [jax.ShapeDtypeStruct((n, w), F32) for w in widths] + [h_shape]
    out_specs = [tok(w) for w in widths] + [h_spec]
    body = functools.partial(_pre_body, tm=tm, tiles_per_seq=tiles_per_seq, seq_len=seq_len)
    return pl.pallas_call(
        body,
        grid=(n_tiles,),
        in_specs=[tok(D_MODEL), pprev_spec] + [full(c) for c in consts],
        out_specs=out_specs,
        out_shape=out_shape,
        scratch_shapes=[pltpu.VMEM((1, RWKV_PROJ), F32)],
        compiler_params=pltpu.CompilerParams(dimension_semantics=("arbitrary",), vmem_limit_bytes=VMEM_LIMIT),
        name="pre",
    )(x2, pprev, *consts)


def _matmul_body(x_ref, w_ref, o_ref):
    o_ref[...] = jnp.dot(x_ref[...].astype(BF16), w_ref[...], preferred_element_type=F32)


def _matmul_call(x, w):
    return pl.pallas_call(_matmul_body, out_shape=jax.ShapeDtypeStruct((x.shape[0], w.shape[1]), F32),
                          name="shift_proj")(x, w)


def _softmax_sink(s, valid, sink):
    s = jnp.where(valid, s, -jnp.inf)
    m = jnp.maximum(jnp.max(s, axis=-1, keepdims=True), sink)
    p = jnp.exp(s - m)
    den = jnp.sum(p, axis=-1, keepdims=True) + jnp.exp(sink - m)
    return p / den


def _attn_prompt_body(sink_ref, q_ref, kc_ref, kp_ref, vc_ref, vp_ref, o_ref):
    n = pl.program_id(1)
    q = q_ref[0]
    kcat = jnp.concatenate([kp_ref[0], kc_ref[0]], axis=0)
    vcat = jnp.concatenate([vp_ref[0], vc_ref[0]], axis=0)
    krot = pltpu.roll(kcat, shift=HEAD_DIM, axis=1)
    vrot = pltpu.roll(vcat, shift=HEAD_DIM, axis=1)
    lane = lax.broadcasted_iota(jnp.int32, (1, LANES), 1)
    lo = lane < HEAD_DIM
    row = lax.broadcasted_iota(jnp.int32, (WINDOW, 2 * WINDOW), 0)
    col = lax.broadcasted_iota(jnp.int32, (WINDOW, 2 * WINDOW), 1)
    valid = (col > row) & (col <= row + WINDOW) & ((col >= WINDOW) | (n > 0))
    for hk in range(N_KV_HEADS):
        src_lo, src_hi = (kcat, krot) if hk == 0 else (krot, kcat)
        k_pads = (jnp.where(lo, src_lo, 0.0).astype(BF16), jnp.where(lo, 0.0, src_hi).astype(BF16))
        vsrc_lo, vsrc_hi = (vcat, vrot) if hk == 0 else (vrot, vcat)
        v_pads = (jnp.where(lo, vsrc_lo, 0.0).astype(BF16), jnp.where(lo, 0.0, vsrc_hi).astype(BF16))
        for pr in range(2):
            pair = hk * 2 + pr
            qp = q[:, pair * LANES:(pair + 1) * LANES].astype(BF16)
            acc = jnp.zeros((WINDOW, LANES), F32)
            for e in range(2):
                s = lax.dot_general(qp, k_pads[e], (((1,), (1,)), ((), ())),
                                    preferred_element_type=F32) * SCALE
                probs = _softmax_sink(s, valid, sink_ref[pair * 2 + e])
                acc = acc + jnp.dot(probs.astype(BF16), v_pads[e], preferred_element_type=F32)
            o_ref[0, :, pair * LANES:(pair + 1) * LANES] = acc


def _attn_prompt_call(sinks, q, k, v):
    b, t, _ = q.shape
    nb = t // WINDOW
    cur = lambda w: pl.BlockSpec((1, WINDOW, w), lambda i, j: (i, j, 0))
    prev = lambda w: pl.BlockSpec((1, WINDOW, w), lambda i, j: (i, jnp.maximum(j - 1, 0), 0))
    return pl.pallas_call(
        _attn_prompt_body,
        grid=(b, nb),
        in_specs=[pl.BlockSpec(memory_space=pltpu.SMEM), cur(ATTN_WIDTH), cur(KV_WIDTH), prev(KV_WIDTH),
                  cur(KV_WIDTH), prev(KV_WIDTH)],
        out_specs=cur(ATTN_WIDTH),
        out_shape=jax.ShapeDtypeStruct((b, t, ATTN_WIDTH), F32),
        compiler_params=pltpu.CompilerParams(dimension_semantics=("arbitrary", "arbitrary")),
        name="attn_prompt",
    )(sinks, q, k, k, v, v)


def _attn_sample_body(sink_ref, q_ref, k_ref, v_ref, o_ref, *, bb, rows, keys, dec_seq):
    r = lax.broadcasted_iota(jnp.int32, (rows, keys), 0)
    c = lax.broadcasted_iota(jnp.int32, (rows, keys), 1)
    t = r % dec_seq
    valid = (c > t) & (c <= t + WINDOW)
    sink = sink_ref[0]

    def one(i, _):
        q = q_ref[i].astype(BF16)
        k = k_ref[i].astype(BF16)
        s = lax.dot_general(q, k, (((1,), (1,)), ((), ())), preferred_element_type=F32) * SCALE
        probs = _softmax_sink(s, valid, sink)
        o_ref[i] = jnp.dot(probs.astype(BF16), v_ref[i].astype(BF16), preferred_element_type=F32)
        return 0

    lax.fori_loop(0, bb, one, 0)


def _attn_sample_call(sink_rows, q, kall, vall, *, dec_seq, bb=32):
    nbh, rows, _ = q.shape
    keys = kall.shape[1]
    per_head = nbh // N_KV_HEADS
    blk = lambda r: pl.BlockSpec((bb, r, HEAD_DIM), lambda i: (i, 0, 0))
    body = functools.partial(_attn_sample_body, bb=bb, rows=rows, keys=keys, dec_seq=dec_seq)
    return pl.pallas_call(
        body,
        grid=(nbh // bb,),
        in_specs=[pl.BlockSpec((1, rows, 1), lambda i: (i * bb // per_head, 0, 0)), blk(rows), blk(keys), blk(keys)],
        out_specs=blk(rows),
        out_shape=jax.ShapeDtypeStruct((nbh, rows, HEAD_DIM), F32),
        compiler_params=pltpu.CompilerParams(dimension_semantics=("arbitrary",)),
        name="attn_sample",
    )(sink_rows, q, kall, vall)


IB_GROUP = 4


BATCH_GROUP = LANES // N_RWKV_HEADS
HEAD_PAIRS = N_RWKV_HEADS // 2


def _to_chains(x):
    m = jnp.concatenate([x[:, hp * LANES:(hp + 1) * LANES] for hp in range(HEAD_PAIRS)], axis=0)
    mt = m.T
    return jnp.concatenate([mt[0:HEAD_DIM, :], mt[HEAD_DIM:2 * HEAD_DIM, :]], axis=1)


def _from_chains(y):
    m = jnp.concatenate([y[:, 0:HEAD_DIM], y[:, HEAD_DIM:2 * HEAD_DIM]], axis=0)
    mt = m.T
    return jnp.concatenate([mt[hp * BATCH_GROUP:(hp + 1) * BATCH_GROUP, :] for hp in range(HEAD_PAIRS)], axis=1)


def _scan_body(*refs, tb, n_tb, zero_init):
    n_in = 6 if zero_init else 7
    raw = refs[:6]
    s0_ref = None if zero_init else refs[6]
    y_out, st_ref = refs[n_in:n_in + 2]
    r_ref, d_ref, k_ref, v_ref, a_ref, b_ref, y_ref, state, sa = refs[n_in + 2:]
    blk = pl.program_id(1)
    n_ib = HEAD_DIM // SUBLANES

    @pl.when(blk == 0)
    def _():
        if zero_init:
            state[...] = jnp.zeros(state.shape, F32)
        else:
            def load_i(i, _):
                state[:, i, :] = _to_chains(s0_ref[:, i, :])
                return 0

            lax.fori_loop(0, HEAD_DIM, load_i, 0)

    def relayout(t, _):
        for src, dst in zip(raw, (r_ref, d_ref, k_ref, v_ref, a_ref, b_ref)):
            dst[t] = _to_chains(src[:, t, :])
        return 0

    lax.fori_loop(0, tb, relayout, 0, unroll=min(tb, 4))

    for ib in range(n_ib):
        rows = pl.ds(ib * SUBLANES, SUBLANES)

        def init_j(j, acc, rows=rows):
            return acc + state[j, rows, :] * a_ref[0, pl.ds(j, 1), :]

        sa[rows, :] = lax.fori_loop(0, HEAD_DIM, init_j, jnp.zeros((SUBLANES, LANES), F32), unroll=8)

    def step(t, _):
        tn = jnp.minimum(t + 1, tb - 1)
        for grp in range(n_ib // IB_GROUP):
            rows = [pl.ds((grp * IB_GROUP + u) * SUBLANES, SUBLANES) for u in range(IB_GROUP)]
            sav = [sa[rw, :] for rw in rows]
            vv = [v_ref[t, rw, :] for rw in rows]

            def jbody(j, carry, rows=rows, sav=sav, vv=vv):
                yacc, sacc = carry
                jr = pl.ds(j, 1)
                dj = d_ref[t, jr, :]
                bj = b_ref[t, jr, :]
                kj = k_ref[t, jr, :]
                rj = r_ref[t, jr, :]
                aj = a_ref[tn, jr, :]
                ny, ns = [], []
                for u in range(IB_GROUP):
                    s_new = state[j, rows[u], :] * dj + sav[u] * bj + vv[u] * kj
                    state[j, rows[u], :] = s_new
                    ny.append(yacc[u] + s_new * rj)
                    ns.append(sacc[u] + s_new * aj)
                return tuple(ny), tuple(ns)

            zero = tuple(jnp.zeros((SUBLANES, LANES), F32) for _ in range(IB_GROUP))
            yacc, sacc = lax.fori_loop(0, HEAD_DIM, jbody, (zero, zero), unroll=8)
            for u in range(IB_GROUP):
                y_ref[t, rows[u], :] = yacc[u]
                sa[rows[u], :] = sacc[u]
        return 0

    lax.fori_loop(0, tb, step, 0)

    def writeback(t, _):
        y_out[:, t, :] = _from_chains(y_ref[t])
        return 0

    lax.fori_loop(0, tb, writeback, 0, unroll=min(tb, 4))

    @pl.when(blk == n_tb - 1)
    def _():
        def store_i(i, _):
            st_ref[:, i, :] = _from_chains(state[:, i, :])
            return 0

        lax.fori_loop(0, HEAD_DIM, store_i, 0)


def _scan_call(r, d, k, v, a, b, s0, *, tb):
    bsz, t_len = r.shape[0], r.shape[1]
    bg = BATCH_GROUP
    n_tb = t_len // tb
    op = pl.BlockSpec((bg, tb, RWKV_WIDTH), lambda g, i: (g, i, 0))
    st = pl.BlockSpec((bg, HEAD_DIM, RWKV_WIDTH), lambda g, i: (g, 0, 0))
    zero_init = s0 is None
    body = functools.partial(_scan_body, tb=tb, n_tb=n_tb, zero_init=zero_init)
    chain_major = pltpu.VMEM((tb, HEAD_DIM, LANES), F32)
    return pl.pallas_call(
        body,
        grid=(bsz // bg, n_tb),
        in_specs=[op] * 6 + ([] if zero_init else [st]),
        out_specs=[op, st],
        out_shape=[jax.ShapeDtypeStruct(r.shape, F32),
                   jax.ShapeDtypeStruct((bsz, HEAD_DIM, RWKV_WIDTH), F32)],
        scratch_shapes=[chain_major] * 7 + [pltpu.VMEM((HEAD_DIM, HEAD_DIM, LANES), F32),
                                            pltpu.VMEM((HEAD_DIM, LANES), F32)],
        compiler_params=pltpu.CompilerParams(dimension_semantics=("arbitrary", "arbitrary"),
                                             vmem_limit_bytes=VMEM_LIMIT),
        name="wkv_scan",
    )(r, d, k, v, a, b, *([] if zero_init else [s0]))


def _post_body(x_ref, at_ref, y_ref, g_ref, bon_ref, j512_ref, lng_ref, lnb_ref, wout_ref, g2_ref, wup_ref,
               wdn_ref, o_ref, *, ff_chunk):
    j512 = j512_ref[...]
    y = y_ref[...]
    mu = _seg_sum(y, j512) * (1.0 / HEAD_DIM)
    yc = y - mu
    var = _seg_sum(yc * yc, j512) * (1.0 / HEAD_DIM)
    yn = yc * lax.rsqrt(var + GN_EPS) * lng_ref[...] + lnb_ref[...]
    rw = (yn + bon_ref[...]) * g_ref[...]
    x1 = (x_ref[...]
          + jnp.dot(at_ref[...].astype(BF16), wout_ref[0:ATTN_WIDTH, :], preferred_element_type=F32)
          + jnp.dot(rw.astype(BF16), wout_ref[ATTN_WIDTH:D_MODEL, :], preferred_element_type=F32))
    ms = jnp.mean(x1 * x1, axis=-1, keepdims=True)
    h2 = (x1 * lax.rsqrt(ms + RMS_EPS) * g2_ref[...]).astype(BF16)
    o_ref[...] = x1
    for c in range(D_FF // ff_chunk):
        u = jnp.dot(h2, wup_ref[:, c * ff_chunk:(c + 1) * ff_chunk], preferred_element_type=F32)
        u = jnp.maximum(u, 0.0)
        o_ref[...] += jnp.dot((u * u).astype(BF16), wdn_ref[c * ff_chunk:(c + 1) * ff_chunk, :],
                              preferred_element_type=F32)


def _post_call(x2, attn, y, g, bon, consts, *, tm, ff_chunk=1024):
    n = x2.shape[0]
    tm = min(tm, n)
    tok = lambda w: pl.BlockSpec((tm, w), lambda i: (i, 0))
    full = lambda a: pl.BlockSpec(a.shape, lambda i: (0,) * a.ndim, pipeline_mode=pl.Buffered(1))
    return pl.pallas_call(
        functools.partial(_post_body, ff_chunk=ff_chunk),
        grid=(n // tm,),
        in_specs=[tok(D_MODEL), tok(ATTN_WIDTH), tok(RWKV_WIDTH), tok(RWKV_WIDTH), tok(RWKV_WIDTH)]
        + [full(c) for c in consts],
        out_specs=tok(D_MODEL),
        out_shape=jax.ShapeDtypeStruct((n, D_MODEL), F32),
        compiler_params=pltpu.CompilerParams(dimension_semantics=("arbitrary",), vmem_limit_bytes=VMEM_LIMIT),
        name="post",
    )(x2, attn, y, g, bon, *consts)


def _layer(x, p_prev_rows, s0, attn_fn, w, *, tm_pre, tm_post, scan_tb):
    b, t, _ = x.shape
    x2 = x.reshape(b * t, D_MODEL)
    outs = _pre_call(x2, p_prev_rows, w["pre"], tm=tm_pre, seq_len=t)
    q, k, v, r, d, kh, vh, av, bv, g, bon, h_last = outs
    attn, new_k, new_v = attn_fn(q, k, v)
    ops = [a.reshape(b, t, RWKV_WIDTH) for a in (r, d, kh, vh, av, bv)]
    s0r = None if s0 is None else s0.astype(F32).transpose(0, 2, 1, 3).reshape(b, HEAD_DIM, RWKV_WIDTH)
    y3, s_fin = _scan_call(*ops, s0r, tb=scan_tb)
    y = y3.reshape(b * t, RWKV_WIDTH)
    s_new = s_fin.reshape(b, HEAD_DIM, N_RWKV_HEADS, HEAD_DIM).transpose(0, 2, 1, 3)
    out = _post_call(x2, attn, y, g, bon, w["post"], tm=tm_post)
    return out.reshape(b, t, D_MODEL), new_k, new_v, s_new, h_last


def _prep_weights(norm1_g, w_in, q_norm_g, k_norm_g, rwkv_mu, w_decay_0, w_decay_up, a_0, a_up, g_up, k_k, k_a,
                  r_k, ln_x_g, ln_x_b, w_out, norm2_g, w_ff_up, w_ff_down):
    ar = jnp.arange
    o = RWKV_OFF
    perm = jnp.concatenate([ar(0, o + 512), ar(o + 576, o + 1088), ar(o + 1088, o + 1600), ar(o + 512, o + 576),
                            ar(o + 1600, o + 1664), ar(o + 1664, o + 1792)])
    w_in_p = w_in[:, perm].astype(BF16)
    mu_p = rwkv_mu[perm[o:] - o].reshape(1, RWKV_PROJ)
    row = lambda a: a.reshape(1, -1).astype(F32)
    eye = lambda n: jnp.kron(jnp.eye(n, dtype=F32), jnp.ones((HEAD_DIM, HEAD_DIM), F32)).astype(BF16)
    j512, j128 = eye(N_RWKV_HEADS), eye(N_KV_HEADS)
    zeros = jnp.zeros((DECAY_LORA, RWKV_WIDTH), F32)
    wcomb = jnp.concatenate([jnp.concatenate([w_decay_up, zeros], axis=1),
                             jnp.concatenate([zeros, a_up], axis=1)], axis=0).astype(BF16)
    pre = [row(norm1_g), w_in_p, mu_p, row(jnp.tile(q_norm_g, N_Q_HEADS)), row(jnp.tile(k_norm_g, N_KV_HEADS)),
           j512, j128, wcomb, row(w_decay_0), row(a_0), g_up.astype(BF16), row(k_k), row(k_a), row(r_k)]
    post = [j512, row(ln_x_g), row(ln_x_b), w_out.astype(BF16), row(norm2_g), w_ff_up.astype(BF16),
            w_ff_down.astype(BF16)]
    return {"pre": pre, "post": post, "w_rwkv": w_in_p[:, RWKV_OFF:]}


def _forward(x_prompt, x_sample, cache_k, cache_v, state_wkv, state_shift, norm1_g, w_in, q_norm_g, k_norm_g,
             attn_sinks, rwkv_mu, w_decay_0, w_decay_up, a_0, a_up, g_up, k_k, k_a, r_k, ln_x_g, ln_x_b, w_out,
             norm2_g, w_ff_up, w_ff_down, *, tm_pre=256, tm_post=512, scan_tb=16):
    depth = norm1_g.shape[0]
    assert depth == 1
    l = 0
    w = _prep_weights(norm1_g[l], w_in[l], q_norm_g[l], k_norm_g[l], rwkv_mu[l], w_decay_0[l], w_decay_up[l],
                      a_0[l], a_up[l], g_up[l], k_k[l], k_a[l], r_k[l], ln_x_g[l], ln_x_b[l], w_out[l], norm2_g[l],
                      w_ff_up[l], w_ff_down[l])
    sinks = attn_sinks[l].astype(F32)
    bp, tp, _ = x_prompt.shape
    bs, ts, _ = x_sample.shape

    def attn_prompt(q, k, v):
        q3, k3, v3 = (a.reshape(bp, tp, -1) for a in (q, k, v))
        out = _attn_prompt_call(sinks, q3, k3, v3).reshape(bp * tp, ATTN_WIDTH)
        tail = lambda a: a[:, tp - WINDOW:].reshape(bp, WINDOW, N_KV_HEADS, HEAD_DIM)
        return out, tail(k3), tail(v3)

    pprev_p = jnp.zeros((bp, 1, RWKV_PROJ), F32)
    yp, k1, v1, w1, s1 = _layer(x_prompt, pprev_p, None, attn_prompt, w, tm_pre=tm_pre, tm_post=tm_post,
                                scan_tb=scan_tb)
    s1 = s1.reshape(bp, D_MODEL)

    ck, cv = cache_k[l].astype(F32), cache_v[l].astype(F32)
    rows = (N_Q_HEADS // N_KV_HEADS) * ts
    pad = (-ts) % SUBLANES

    def attn_sample(q, k, v):
        k4 = k.reshape(bs, ts, N_KV_HEADS, HEAD_DIM)
        v4 = v.reshape(bs, ts, N_KV_HEADS, HEAD_DIM)
        kall = jnp.concatenate([ck, k4], axis=1)
        vall = jnp.concatenate([cv, v4], axis=1)
        head_major = lambda a: jnp.pad(a, ((0, 0), (0, pad), (0, 0), (0, 0))).transpose(2, 0, 1, 3).reshape(
            N_KV_HEADS * bs, WINDOW + ts + pad, HEAD_DIM)
        q5 = q.reshape(bs, ts, N_KV_HEADS, N_Q_HEADS // N_KV_HEADS, HEAD_DIM).transpose(2, 0, 3, 1, 4)
        q5 = q5.reshape(N_KV_HEADS * bs, rows, HEAD_DIM)
        sink_rows = jnp.repeat(sinks.reshape(N_KV_HEADS, N_Q_HEADS // N_KV_HEADS), ts, axis=1)[..., None]
        o = _attn_sample_call(sink_rows, q5, head_major(kall), head_major(vall), dec_seq=ts)
        o = o.reshape(N_KV_HEADS, bs, N_Q_HEADS // N_KV_HEADS, ts, HEAD_DIM).transpose(1, 3, 0, 2, 4)
        return o.reshape(bs * ts, ATTN_WIDTH), kall[:, ts:], vall[:, ts:]

    p_prev = _matmul_call(state_shift[l], w["w_rwkv"])
    pprev_s = jnp.repeat(p_prev, ts, axis=0)
    ys, k2, v2, w2, h_all = _layer(x_sample, pprev_s, state_wkv[l], attn_sample, w, tm_pre=tm_pre,
                                   tm_post=tm_post, scan_tb=ts)
    s2 = h_all.reshape(bs, ts, D_MODEL)[:, -1]
    st = lambda a: a[None]
    return (yp, ys, st(k1), st(v1), st(w1), st(s1), st(k2), st(v2), st(w2), st(s2))


def kernel(x_prompt, x_sample, cache_k, cache_v, state_wkv, state_shift, norm1_g, w_in, q_norm_g, k_norm_g, attn_sinks, rwkv_mu, w_decay_0, w_decay_up, a_0, a_up, g_up, k_k, k_a, r_k, ln_x_g, ln_x_b, w_out, norm2_g, w_ff_up, w_ff_down):
    return _forward(x_prompt, x_sample, cache_k, cache_v, state_wkv, state_shift, norm1_g, w_in, q_norm_g, k_norm_g,
                    attn_sinks, rwkv_mu, w_decay_0, w_decay_up, a_0, a_up, g_up, k_k, k_a, r_k, ln_x_g, ln_x_b,
                    w_out, norm2_g, w_ff_up, w_ff_down)
```

```python
import functools

import jax
import jax.numpy as jnp
from jax import lax
from jax.experimental import pallas as pl
from jax.experimental.pallas import tpu as pltpu

D_MODEL = 1024
HEAD_DIM = 64
ATTN_WIDTH = 512
N_Q_HEADS = 8
N_KV_HEADS = 2
KV_WIDTH = 128
RWKV_WIDTH = 512
N_RWKV_HEADS = 8
WINDOW = 128
SCALE = HEAD_DIM ** -0.5
DECAY_LORA = 64
AAA_LORA = 64
GATE_LORA = 128
D_FF = 4 * D_MODEL
RMS_EPS = 1e-6
GN_EPS = 64e-5
RWKV_OFF = ATTN_WIDTH + 2 * KV_WIDTH
RWKV_PROJ = 3 * RWKV_WIDTH + DECAY_LORA + AAA_LORA + GATE_LORA
IN_WIDTH = RWKV_OFF + RWKV_PROJ
LANES = 128
SUBLANES = 8
VMEM_LIMIT = 56 * 1024 * 1024

F32 = jnp.float32
BF16 = jnp.bfloat16


def _seg_sum(x, ones_blk):
    hi = x.astype(BF16)
    lo = (x - hi.astype(F32)).astype(BF16)
    return (jnp.dot(hi, ones_blk, preferred_element_type=F32)
            + jnp.dot(lo, ones_blk, preferred_element_type=F32))


def _pre_body(x_ref, pprev_ref, g1_ref, win_ref, mu_ref, qg_ref, kg_ref, j512_ref, j128_ref,
              wcomb_ref, w0_ref, a0_ref, gup_ref, kk_ref, ka_ref, rk_ref,
              q_o, k_o, v_o, r_o, d_o, kh_o, vh_o, a_o, b_o, g_o, bon_o, h_o,
              carry, *, tm, tiles_per_seq, seq_len):
    x = x_ref[...]
    ms = jnp.mean(x * x, axis=-1, keepdims=True)
    h = x * lax.rsqrt(ms + RMS_EPS) * g1_ref[...]
    proj = jnp.dot(h.astype(BF16), win_ref[...], preferred_element_type=F32)

    j512 = j512_ref[...]
    q = proj[:, 0:ATTN_WIDTH]
    qs = _seg_sum(q * q, j512) * (1.0 / HEAD_DIM)
    q_o[...] = q * lax.rsqrt(qs + RMS_EPS) * qg_ref[...]
    k = proj[:, ATTN_WIDTH:ATTN_WIDTH + KV_WIDTH]
    ks = _seg_sum(k * k, j128_ref[...]) * (1.0 / HEAD_DIM)
    k_o[...] = k * lax.rsqrt(ks + RMS_EPS) * kg_ref[...]
    v_o[...] = proj[:, ATTN_WIDTH + KV_WIDTH:RWKV_OFF]

    p = proj[:, RWKV_OFF:]
    rolled = pltpu.roll(p, shift=1, axis=0)
    row = lax.broadcasted_iota(jnp.int32, (tm, 1), 0)
    if tiles_per_seq is not None:
        t = pl.program_id(0) % tiles_per_seq
        first = jnp.where(t == 0, pprev_ref[0], carry[...])
        p_shift = jnp.where(row == 0, first, rolled)
        carry[...] = p[tm - 1:tm, :]

        @pl.when(t == tiles_per_seq - 1)
        def _():
            h_o[0] = h[tm - 1:tm, :]
    else:
        p_shift = jnp.where(row % seq_len == 0, pprev_ref[...], rolled)
        h_o[...] = h
    pm = p + (p_shift - p) * mu_ref[...]

    xr = pm[:, 0:512]
    xk = pm[:, 512:1024]
    xv = pm[:, 1024:1536]
    xwa = pm[:, 1536:1664]
    xg = pm[:, 1664:1792]
    lane = lax.broadcasted_iota(jnp.int32, (1, LANES), 1)
    wa_in = jnp.where(lane < DECAY_LORA, jnp.tanh(xwa), xwa)
    pre = jnp.dot(wa_in.astype(BF16), wcomb_ref[...], preferred_element_type=F32)
    w_pre = pre[:, 0:512] + w0_ref[...]
    a_pre = pre[:, 512:1024] + a0_ref[...]
    z = -w_pre
    softplus = jnp.maximum(z, 0.0) + jnp.log(1.0 + jnp.exp(-jnp.abs(z)))
    w_log = -softplus - 0.5
    d_o[...] = jnp.exp(-jnp.exp(w_log))
    a_gate = 1.0 / (1.0 + jnp.exp(-a_pre))
    sg = 1.0 / (1.0 + jnp.exp(-xg))
    g_o[...] = jnp.dot(sg.astype(BF16), gup_ref[...], preferred_element_type=F32)
    kkv = xk * kk_ref[...]
    n2 = _seg_sum(kkv * kkv, j512)
    kkn = kkv / jnp.maximum(jnp.sqrt(n2), 1e-12)
    k_h = xk * (1.0 + (a_gate - 1.0) * ka_ref[...])
    r_o[...] = xr
    kh_o[...] = k_h
    vh_o[...] = xv
    a_o[...] = -kkn
    b_o[...] = kkn * a_gate
    bon_o[...] = _seg_sum(xr * k_h * rk_ref[...], j512) * xv


def _pre_call(x2, pprev, consts, *, tm, seq_len):
    n = x2.shape[0]
    tm = min(tm, n)
    n_tiles = n // tm
    per_seq = seq_len >= tm
    tiles_per_seq = seq_len // tm if per_seq else None
    n_seq = n // seq_len
    full = lambda a: pl.BlockSpec(a.shape, lambda i: (0,) * a.ndim)
    tok = lambda w: pl.BlockSpec((tm, w), lambda i: (i, 0))
    if per_seq:
        pprev_spec = pl.BlockSpec((1, 1, RWKV_PROJ), lambda i: (i // tiles_per_seq, 0, 0))
        h_spec = pl.BlockSpec((1, 1, D_MODEL), lambda i: (i // tiles_per_seq, 0, 0))
        h_shape = jax.ShapeDtypeStruct((n_seq, 1, D_MODEL), F32)
    else:
        pprev_spec = tok(RWKV_PROJ)
        h_spec = tok(D_MODEL)
        h_shape = jax.ShapeDtypeStruct((n, D_MODEL), F32)
    widths = [ATTN_WIDTH, KV_WIDTH, KV_WIDTH] + [RWKV_WIDTH] * 8
    out_shape = [jax.ShapeDtypeStruct((n, w), F32) for w in widths] + [h_shape]
    out_specs = [tok(w) for w in widths] + [h_spec]
    body = functools.partial(_pre_body, tm=tm, tiles_per_seq=tiles_per_seq, seq_len=seq_len)
    return pl.pallas_call(
        body,
        grid=(n_tiles,),
        in_specs=[tok(D_MODEL), pprev_spec] + [full(c) for c in consts],
        out_specs=out_specs,
        out_shape=out_shape,
        scratch_shapes=[pltpu.VMEM((1, RWKV_PROJ), F32)],
        compiler_params=pltpu.CompilerParams(dimension_semantics=("arbitrary",), vmem_limit_bytes=VMEM_LIMIT),
        name="pre",
    )(x2, pprev, *consts)


def _matmul_body(x_ref, w_ref, o_ref):
    o_ref[...] = jnp.dot(x_ref[...].astype(BF16), w_ref[...], preferred_element_type=F32)


def _matmul_call(x, w):
    return pl.pallas_call(_matmul_body, out_shape=jax.ShapeDtypeStruct((x.shape[0], w.shape[1]), F32),
                          name="shift_proj")(x, w)


def _softmax_sink(s, valid, sink):
    s = jnp.where(valid, s, -jnp.inf)
    m = jnp.maximum(jnp.max(s, axis=-1, keepdims=True), sink)
    p = jnp.exp(s - m)
    den = jnp.sum(p, axis=-1, keepdims=True) + jnp.exp(sink - m)
    return p / den


def _attn_prompt_body(sink_ref, q_ref, kc_ref, kp_ref, vc_ref, vp_ref, o_ref):
    n = pl.program_id(1)
    q = q_ref[0]
    kcat = jnp.concatenate([kp_ref[0], kc_ref[0]], axis=0)
    vcat = jnp.concatenate([vp_ref[0], vc_ref[0]], axis=0)
    krot = pltpu.roll(kcat, shift=HEAD_DIM, axis=1)
    vrot = pltpu.roll(vcat, shift=HEAD_DIM, axis=1)
    lane = lax.broadcasted_iota(jnp.int32, (1, LANES), 1)
    lo = lane < HEAD_DIM
    row = lax.broadcasted_iota(jnp.int32, (WINDOW, WINDOW), 0)
    col = lax.broadcasted_iota(jnp.int32, (WINDOW, WINDOW), 1)
    upper = col > row
    no_prev = upper & (n == 0)
    for hk in range(N_KV_HEADS):
        src_lo, src_hi = (kcat, krot) if hk == 0 else (krot, kcat)
        k_pads = (jnp.where(lo, src_lo, 0.0).astype(BF16), jnp.where(lo, 0.0, src_hi).astype(BF16))
        vsrc_lo, vsrc_hi = (vcat, vrot) if hk == 0 else (vrot, vcat)
        v_pads = (jnp.where(lo, vsrc_lo, 0.0).astype(BF16), jnp.where(lo, 0.0, vsrc_hi).astype(BF16))
        for pr in range(2):
            pair = hk * 2 + pr
            qp = (q[:, pair * LANES:(pair + 1) * LANES] * SCALE).astype(BF16)
            acc = jnp.zeros((WINDOW, LANES), F32)
            inv = []
            for e in range(2):
                s = lax.dot_general(qp, k_pads[e], (((1,), (1,)), ((), ())),
                                    preferred_element_type=F32)
                s = jnp.where(upper, s[:, 0:WINDOW], s[:, WINDOW:2 * WINDOW])
                s = jnp.where(no_prev, -jnp.inf, s)
                sink = sink_ref[pair * 2 + e]
                m = jnp.maximum(jnp.max(s, axis=-1, keepdims=True), sink)
                p = jnp.exp(s - m)
                inv.append(1.0 / (jnp.sum(p, axis=-1, keepdims=True) + jnp.exp(sink - m)))
                p2 = jnp.concatenate([jnp.where(upper, p, 0.0), jnp.where(upper, 0.0, p)], axis=1)
                acc = acc + jnp.dot(p2.astype(BF16), v_pads[e], preferred_element_type=F32)
            o_ref[0, :, pair * LANES:(pair + 1) * LANES] = acc * jnp.where(lo, inv[0], inv[1])


def _attn_prompt_call(sinks, q, k, v):
    b, t, _ = q.shape
    nb = t // WINDOW
    cur = lambda w: pl.BlockSpec((1, WINDOW, w), lambda i, j: (i, j, 0))
    prev = lambda w: pl.BlockSpec((1, WINDOW, w), lambda i, j: (i, jnp.maximum(j - 1, 0), 0))
    return pl.pallas_call(
        _attn_prompt_body,
        grid=(b, nb),
        in_specs=[pl.BlockSpec(memory_space=pltpu.SMEM), cur(ATTN_WIDTH), cur(KV_WIDTH), prev(KV_WIDTH),
                  cur(KV_WIDTH), prev(KV_WIDTH)],
        out_specs=cur(ATTN_WIDTH),
        out_shape=jax.ShapeDtypeStruct((b, t, ATTN_WIDTH), F32),
        compiler_params=pltpu.CompilerParams(dimension_semantics=("arbitrary", "arbitrary")),
        name="attn_prompt",
    )(sinks, q, k, k, v, v)


def _attn_sample_body(sink_ref, q_ref, k_ref, v_ref, o_ref, *, bb, rows, keys, dec_seq):
    r = lax.broadcasted_iota(jnp.int32, (rows, keys), 0)
    c = lax.broadcasted_iota(jnp.int32, (rows, keys), 1)
    t = r % dec_seq
    valid = (c > t) & (c <= t + WINDOW)
    sink = sink_ref[0]

    def one(i, _):
        q = q_ref[i].astype(BF16)
        k = k_ref[i].astype(BF16)
        s = lax.dot_general(q, k, (((1,), (1,)), ((), ())), preferred_element_type=F32) * SCALE
        probs = _softmax_sink(s, valid, sink)
        o_ref[i] = jnp.dot(probs.astype(BF16), v_ref[i].astype(BF16), preferred_element_type=F32)
        return 0

    lax.fori_loop(0, bb, one, 0, unroll=8)


def _attn_sample_call(sink_rows, q, kall, vall, *, dec_seq, bb=32):
    nbh, rows, _ = q.shape
    keys = kall.shape[1]
    per_head = nbh // N_KV_HEADS
    blk = lambda r: pl.BlockSpec((bb, r, HEAD_DIM), lambda i: (i, 0, 0))
    body = functools.partial(_attn_sample_body, bb=bb, rows=rows, keys=keys, dec_seq=dec_seq)
    return pl.pallas_call(
        body,
        grid=(nbh // bb,),
        in_specs=[pl.BlockSpec((1, rows, 1), lambda i: (i * bb // per_head, 0, 0)), blk(rows), blk(keys), blk(keys)],
        out_specs=blk(rows),
        out_shape=jax.ShapeDtypeStruct((nbh, rows, HEAD_DIM), F32),
        compiler_params=pltpu.CompilerParams(dimension_semantics=("arbitrary",)),
        name="attn_sample",
    )(sink_rows, q, kall, vall)


IB_GROUP = 4


BATCH_GROUP = LANES // N_RWKV_HEADS
HEAD_PAIRS = N_RWKV_HEADS // 2


def _to_chains(x):
    m = jnp.concatenate([x[:, hp * LANES:(hp + 1) * LANES] for hp in range(HEAD_PAIRS)], axis=0)
    mt = m.T
    return jnp.concatenate([mt[0:HEAD_DIM, :], mt[HEAD_DIM:2 * HEAD_DIM, :]], axis=1)


def _from_chains(y):
    m = jnp.concatenate([y[:, 0:HEAD_DIM], y[:, HEAD_DIM:2 * HEAD_DIM]], axis=0)
    mt = m.T
    return jnp.concatenate([mt[hp * BATCH_GROUP:(hp + 1) * BATCH_GROUP, :] for hp in range(HEAD_PAIRS)], axis=1)


def _zero_like_dep(x):
    u = lax.bitcast_convert_type(x, jnp.uint32)
    z = lax.shift_right_logical(lax.shift_right_logical(u, jnp.uint32(16)), jnp.uint32(16))
    return lax.bitcast_convert_type(z, F32)


def _after(x, zero):
    return x + zero


def _scan_body(*refs, tb, n_tb, zero_init):
    n_in = 6 if zero_init else 7
    raw = refs[:6]
    s0_ref = None if zero_init else refs[6]
    y_out, st_ref = refs[n_in:n_in + 2]
    scratch = refs[n_in + 2:]
    slots = (scratch[0:6], scratch[6:12])
    state, sa = scratch[12:]
    s = pl.program_id(1)
    n_ib = HEAD_DIM // SUBLANES

    @pl.when(s == 0)
    def _():
        for buf in slots[1]:
            buf[...] = jnp.zeros(buf.shape, F32)
        state[...] = jnp.zeros(state.shape, F32)

    if not zero_init:
        @pl.when(s == 1)
        def _():
            def load_i(i, _):
                state[:, i, :] = _to_chains(s0_ref[:, i, :])
                return 0

            lax.fori_loop(0, HEAD_DIM, load_i, 0, unroll=8)

    def run_block(rd, wr):
        r_ref, d_ref, k_ref, v_ref, a_ref, b_ref = rd
        assert n_ib // IB_GROUP == 2

        for ib in range(n_ib):
            rows = pl.ds(ib * SUBLANES, SUBLANES)

            def init_j(j, acc, rows=rows):
                return acc + state[j, rows, :] * a_ref[0, pl.ds(j, 1), :]

            sa[rows, :] = lax.fori_loop(0, HEAD_DIM, init_j, jnp.zeros((SUBLANES, LANES), F32), unroll=8)

        def step(t, y_prev):
            y_tok = _from_chains(y_prev)
            y_out[:, jnp.maximum(t - 1, 0), :] = y_tok
            gates = {(0, 8): _zero_like_dep(y_tok[0:1, 0:LANES])}
            for n, (src, dst) in enumerate(zip(raw, wr)):
                c = _to_chains(src[:, t, :])
                dst[t] = c
                gates[(n // 3, 24 + 16 * (n % 3))] = _zero_like_dep(c[0:1, :])
            y_new = []
            tn = jnp.minimum(t + 1, tb - 1)
            for grp in range(n_ib // IB_GROUP):
                rows = [pl.ds((grp * IB_GROUP + u) * SUBLANES, SUBLANES) for u in range(IB_GROUP)]
                sav = [sa[rw, :] for rw in rows]
                vv = [v_ref[t, rw, :] for rw in rows]
                yacc = [jnp.zeros((SUBLANES, LANES), F32) for _ in range(IB_GROUP)]
                sacc = [jnp.zeros((SUBLANES, LANES), F32) for _ in range(IB_GROUP)]
                for j in range(HEAD_DIM):
                    jr = pl.ds(j, 1)
                    dj = d_ref[t, jr, :]
                    if (grp, j) in gates:
                        dj = _after(dj, gates[(grp, j)])
                    bj = b_ref[t, jr, :]
                    kj = k_ref[t, jr, :]
                    rj = r_ref[t, jr, :]
                    aj = a_ref[tn, jr, :]
                    for u in range(IB_GROUP):
                        s_new = state[j, rows[u], :] * dj + sav[u] * bj + vv[u] * kj
                        state[j, rows[u], :] = s_new
                        yacc[u] = yacc[u] + s_new * rj
                        sacc[u] = sacc[u] + s_new * aj
                for u in range(IB_GROUP):
                    sa[rows[u], :] = sacc[u]
                y_new.extend(yacc)
            return jnp.concatenate(y_new, axis=0)

        y_last = lax.fori_loop(0, tb, step, jnp.zeros((HEAD_DIM, LANES), F32))
        y_out[:, tb - 1, :] = _from_chains(y_last)

    @pl.when(s % 2 == 0)
    def _():
        run_block(slots[1], slots[0])

    @pl.when(s % 2 == 1)
    def _():
        run_block(slots[0], slots[1])

    @pl.when(s == n_tb)
    def _():
        def store_i(i, _):
            st_ref[:, i, :] = _from_chains(state[:, i, :])
            return 0

        lax.fori_loop(0, HEAD_DIM, store_i, 0, unroll=8)


def _scan_call(r, d, k, v, a, b, s0, *, tb):
    bsz, t_len = r.shape[0], r.shape[1]
    bg = BATCH_GROUP
    n_tb = t_len // tb
    op_in = pl.BlockSpec((bg, tb, RWKV_WIDTH), lambda g, s: (g, jnp.minimum(s, n_tb - 1), 0))
    op_out = pl.BlockSpec((bg, tb, RWKV_WIDTH), lambda g, s: (g, jnp.maximum(s - 1, 0), 0))
    st = pl.BlockSpec((bg, HEAD_DIM, RWKV_WIDTH), lambda g, s: (g, 0, 0))
    zero_init = s0 is None
    body = functools.partial(_scan_body, tb=tb, n_tb=n_tb, zero_init=zero_init)
    return pl.pallas_call(
        body,
        grid=(bsz // bg, n_tb + 1),
        in_specs=[op_in] * 6 + ([] if zero_init else [st]),
        out_specs=[op_out, st],
        out_shape=[jax.ShapeDtypeStruct(r.shape, F32),
                   jax.ShapeDtypeStruct((bsz, HEAD_DIM, RWKV_WIDTH), F32)],
        scratch_shapes=[pltpu.VMEM((tb, HEAD_DIM, LANES), F32)] * 12
        + [pltpu.VMEM((HEAD_DIM, HEAD_DIM, LANES), F32), pltpu.VMEM((HEAD_DIM, LANES), F32)],
        compiler_params=pltpu.CompilerParams(dimension_semantics=("arbitrary", "arbitrary"),
                                             vmem_limit_bytes=VMEM_LIMIT),
        name="wkv_scan",
    )(r, d, k, v, a, b, *([] if zero_init else [s0]))


def _post_body(x_ref, at_ref, y_ref, g_ref, bon_ref, j512_ref, lng_ref, lnb_ref, wout_ref, g2_ref, wup_ref,
               wdn_ref, o_ref, *, ff_chunk):
    j512 = j512_ref[...]
    y = y_ref[...]
    mu = _seg_sum(y, j512) * (1.0 / HEAD_DIM)
    yc = y - mu
    var = _seg_sum(yc * yc, j512) * (1.0 / HEAD_DIM)
    yn = yc * lax.rsqrt(var + GN_EPS) * lng_ref[...] + lnb_ref[...]
    rw = (yn + bon_ref[...]) * g_ref[...]
    x1 = (x_ref[...]
          + jnp.dot(at_ref[...].astype(BF16), wout_ref[0:ATTN_WIDTH, :], preferred_element_type=F32)
          + jnp.dot(rw.astype(BF16), wout_ref[ATTN_WIDTH:D_MODEL, :], preferred_element_type=F32))
    ms = jnp.mean(x1 * x1, axis=-1, keepdims=True)
    h2 = (x1 * lax.rsqrt(ms + RMS_EPS) * g2_ref[...]).astype(BF16)
    o_ref[...] = x1
    for c in range(D_FF // ff_chunk):
        u = jnp.dot(h2, wup_ref[:, c * ff_chunk:(c + 1) * ff_chunk], preferred_element_type=F32)
        u = jnp.maximum(u, 0.0)
        o_ref[...] += jnp.dot((u * u).astype(BF16), wdn_ref[c * ff_chunk:(c + 1) * ff_chunk, :],
                              preferred_element_type=F32)


def _post_call(x2, attn, y, g, bon, consts, *, tm, ff_chunk=1024):
    n = x2.shape[0]
    tm = min(tm, n)
    tok = lambda w: pl.BlockSpec((tm, w), lambda i: (i, 0))
    full = lambda a: pl.BlockSpec(a.shape, lambda i: (0,) * a.ndim, pipeline_mode=pl.Buffered(1))
    return pl.pallas_call(
        functools.partial(_post_body, ff_chunk=ff_chunk),
        grid=(n // tm,),
        in_specs=[tok(D_MODEL), tok(ATTN_WIDTH), tok(RWKV_WIDTH), tok(RWKV_WIDTH), tok(RWKV_WIDTH)]
        + [full(c) for c in consts],
        out_specs=tok(D_MODEL),
        out_shape=jax.ShapeDtypeStruct((n, D_MODEL), F32),
        compiler_params=pltpu.CompilerParams(dimension_semantics=("arbitrary",), vmem_limit_bytes=VMEM_LIMIT),
        name="post",
    )(x2, attn, y, g, bon, *consts)


def _layer(x, p_prev_rows, s0, attn_fn, w, *, tm_pre, tm_post, scan_tb):
    b, t, _ = x.shape
    x2 = x.reshape(b * t, D_MODEL)
    outs = _pre_call(x2, p_prev_rows, w["pre"], tm=tm_pre, seq_len=t)
    q, k, v, r, d, kh, vh, av, bv, g, bon, h_last = outs
    attn, new_k, new_v = attn_fn(q, k, v)
    ops = [a.reshape(b, t, RWKV_WIDTH) for a in (r, d, kh, vh, av, bv)]
    s0r = None if s0 is None else s0.astype(F32).transpose(0, 2, 1, 3).reshape(b, HEAD_DIM, RWKV_WIDTH)
    y3, s_fin = _scan_call(*ops, s0r, tb=scan_tb)
    y = y3.reshape(b * t, RWKV_WIDTH)
    s_new = s_fin.reshape(b, HEAD_DIM, N_RWKV_HEADS, HEAD_DIM).transpose(0, 2, 1, 3)
    out = _post_call(x2, attn, y, g, bon, w["post"], tm=tm_post)
    return out.reshape(b, t, D_MODEL), new_k, new_v, s_new, h_last


def _prep_weights(norm1_g, w_in, q_norm_g, k_norm_g, rwkv_mu, w_decay_0, w_decay_up, a_0, a_up, g_up, k_k, k_a,
                  r_k, ln_x_g, ln_x_b, w_out, norm2_g, w_ff_up, w_ff_down):
    ar = jnp.arange
    o = RWKV_OFF
    perm = jnp.concatenate([ar(0, o + 512), ar(o + 576, o + 1088), ar(o + 1088, o + 1600), ar(o + 512, o + 576),
                            ar(o + 1600, o + 1664), ar(o + 1664, o + 1792)])
    w_in_p = w_in[:, perm].astype(BF16)
    mu_p = rwkv_mu[perm[o:] - o].reshape(1, RWKV_PROJ)
    row = lambda a: a.reshape(1, -1).astype(F32)
    eye = lambda n: jnp.kron(jnp.eye(n, dtype=F32), jnp.ones((HEAD_DIM, HEAD_DIM), F32)).astype(BF16)
    j512, j128 = eye(N_RWKV_HEADS), eye(N_KV_HEADS)
    zeros = jnp.zeros((DECAY_LORA, RWKV_WIDTH), F32)
    wcomb = jnp.concatenate([jnp.concatenate([w_decay_up, zeros], axis=1),
                             jnp.concatenate([zeros, a_up], axis=1)], axis=0).astype(BF16)
    pre = [row(norm1_g), w_in_p, mu_p, row(jnp.tile(q_norm_g, N_Q_HEADS)), row(jnp.tile(k_norm_g, N_KV_HEADS)),
           j512, j128, wcomb, row(w_decay_0), row(a_0), g_up.astype(BF16), row(k_k), row(k_a), row(r_k)]
    post = [j512, row(ln_x_g), row(ln_x_b), w_out.astype(BF16), row(norm2_g), w_ff_up.astype(BF16),
            w_ff_down.astype(BF16)]
    return {"pre": pre, "post": post, "w_rwkv": w_in_p[:, RWKV_OFF:]}


def _forward(x_prompt, x_sample, cache_k, cache_v, state_wkv, state_shift, norm1_g, w_in, q_norm_g, k_norm_g,
             attn_sinks, rwkv_mu, w_decay_0, w_decay_up, a_0, a_up, g_up, k_k, k_a, r_k, ln_x_g, ln_x_b, w_out,
             norm2_g, w_ff_up, w_ff_down, *, tm_pre=256, tm_post=512, scan_tb=16):
    depth = norm1_g.shape[0]
    assert depth == 1
    l = 0
    w = _prep_weights(norm1_g[l], w_in[l], q_norm_g[l], k_norm_g[l], rwkv_mu[l], w_decay_0[l], w_decay_up[l],
                      a_0[l], a_up[l], g_up[l], k_k[l], k_a[l], r_k[l], ln_x_g[l], ln_x_b[l], w_out[l], norm2_g[l],
                      w_ff_up[l], w_ff_down[l])
    sinks = attn_sinks[l].astype(F32)
    bp, tp, _ = x_prompt.shape
    bs, ts, _ = x_sample.shape

    def attn_prompt(q, k, v):
        q3, k3, v3 = (a.reshape(bp, tp, -1) for a in (q, k, v))
        out = _attn_prompt_call(sinks, q3, k3, v3).reshape(bp * tp, ATTN_WIDTH)
        tail = lambda a: a[:, tp - WINDOW:].reshape(bp, WINDOW, N_KV_HEADS, HEAD_DIM)
        return out, tail(k3), tail(v3)

    pprev_p = jnp.zeros((bp, 1, RWKV_PROJ), F32)
    yp, k1, v1, w1, s1 = _layer(x_prompt, pprev_p, None, attn_prompt, w, tm_pre=tm_pre, tm_post=tm_post,
                                scan_tb=scan_tb)
    s1 = s1.reshape(bp, D_MODEL)

    ck, cv = cache_k[l].astype(F32), cache_v[l].astype(F32)
    rows = (N_Q_HEADS // N_KV_HEADS) * ts
    pad = (-ts) % SUBLANES

    def attn_sample(q, k, v):
        k4 = k.reshape(bs, ts, N_KV_HEADS, HEAD_DIM)
        v4 = v.reshape(bs, ts, N_KV_HEADS, HEAD_DIM)
        kall = jnp.concatenate([ck, k4], axis=1)
        vall = jnp.concatenate([cv, v4], axis=1)
        head_major = lambda a: jnp.pad(a, ((0, 0), (0, pad), (0, 0), (0, 0))).transpose(2, 0, 1, 3).reshape(
            N_KV_HEADS * bs, WINDOW + ts + pad, HEAD_DIM)
        q5 = q.reshape(bs, ts, N_KV_HEADS, N_Q_HEADS // N_KV_HEADS, HEAD_DIM).transpose(2, 0, 3, 1, 4)
        q5 = q5.reshape(N_KV_HEADS * bs, rows, HEAD_DIM)
        sink_rows = jnp.repeat(sinks.reshape(N_KV_HEADS, N_Q_HEADS // N_KV_HEADS), ts, axis=1)[..., None]
        o = _attn_sample_call(sink_rows, q5, head_major(kall), head_major(vall), dec_seq=ts)
        o = o.reshape(N_KV_HEADS, bs, N_Q_HEADS // N_KV_HEADS, ts, HEAD_DIM).transpose(1, 3, 0, 2, 4)
        return o.reshape(bs * ts, ATTN_WIDTH), kall[:, ts:], vall[:, ts:]

    p_prev = _matmul_call(state_shift[l], w["w_rwkv"])
    pprev_s = jnp.repeat(p_prev, ts, axis=0)
    ys, k2, v2, w2, h_all = _layer(x_sample, pprev_s, state_wkv[l], attn_sample, w, tm_pre=tm_pre,
                                   tm_post=tm_post, scan_tb=ts)
    s2 = h_all.reshape(bs, ts, D_MODEL)[:, -1]
    st = lambda a: a[None]
    return (yp, ys, st(k1), st(v1), st(w1), st(s1), st(k2), st(v2), st(w2), st(s2))


def kernel(x_prompt, x_sample, cache_k, cache_v, state_wkv, state_shift, norm1_g, w_in, q_norm_g, k_norm_g, attn_sinks, rwkv_mu, w_decay_0, w_decay_up, a_0, a_up, g_up, k_k, k_a, r_k, ln_x_g, ln_x_b, w_out, norm2_g, w_ff_up, w_ff_down):
    return _forward(x_prompt, x_sample, cache_k, cache_v, state_wkv, state_shift, norm1_g, w_in, q_norm_g, k_norm_g,
                    attn_sinks, rwkv_mu, w_decay_0, w_decay_up, a_0, a_up, g_up, k_k, k_a, r_k, ln_x_g, ln_x_b,
                    w_out, norm2_g, w_ff_up, w_ff_down)
```

```python
import functools

import jax
import jax.numpy as jnp
from jax import lax
from jax.experimental import pallas as pl
from jax.experimental.pallas import tpu as pltpu

D_MODEL = 1024
HEAD_DIM = 64
ATTN_WIDTH = 512
N_Q_HEADS = 8
N_KV_HEADS = 2
KV_WIDTH = 128
RWKV_WIDTH = 512
N_RWKV_HEADS = 8
WINDOW = 128
SCALE = HEAD_DIM ** -0.5
DECAY_LORA = 64
AAA_LORA = 64
GATE_LORA = 128
D_FF = 4 * D_MODEL
RMS_EPS = 1e-6
GN_EPS = 64e-5
RWKV_OFF = ATTN_WIDTH + 2 * KV_WIDTH
RWKV_PROJ = 3 * RWKV_WIDTH + DECAY_LORA + AAA_LORA + GATE_LORA
IN_WIDTH = RWKV_OFF + RWKV_PROJ
LANES = 128
SUBLANES = 8
VMEM_LIMIT = 56 * 1024 * 1024

F32 = jnp.float32
BF16 = jnp.bfloat16


def _seg_sum(x, ones_blk):
    return jnp.dot(x.astype(BF16), ones_blk, preferred_element_type=F32)


def _pre_body(x_ref, pprev_ref, g1_ref, win_ref, mu_ref, qg_ref, kg_ref, j512_ref, j128_ref,
              wcomb_ref, w0_ref, a0_ref, gup_ref, kk_ref, ka_ref, rk_ref,
              q_o, k_o, v_o, r_o, d_o, kh_o, vh_o, a_o, b_o, g_o, bon_o, h_o,
              carry, *, tm, tiles_per_seq, seq_len):
    x = x_ref[...]
    ms = jnp.mean(x * x, axis=-1, keepdims=True)
    h = x * lax.rsqrt(ms + RMS_EPS) * g1_ref[...]
    proj = jnp.dot(h.astype(BF16), win_ref[...], preferred_element_type=F32)

    j512 = j512_ref[...]
    q = proj[:, 0:ATTN_WIDTH]
    qs = _seg_sum(q * q, j512) * (1.0 / HEAD_DIM)
    q_o[...] = q * lax.rsqrt(qs + RMS_EPS) * qg_ref[...]
    k = proj[:, ATTN_WIDTH:ATTN_WIDTH + KV_WIDTH]
    ks = _seg_sum(k * k, j128_ref[...]) * (1.0 / HEAD_DIM)
    k_o[...] = k * lax.rsqrt(ks + RMS_EPS) * kg_ref[...]
    v_o[...] = proj[:, ATTN_WIDTH + KV_WIDTH:RWKV_OFF]

    p = proj[:, RWKV_OFF:]
    rolled = pltpu.roll(p, shift=1, axis=0)
    row = lax.broadcasted_iota(jnp.int32, (tm, 1), 0)
    if tiles_per_seq is not None:
        t = pl.program_id(0) % tiles_per_seq
        first = jnp.where(t == 0, pprev_ref[0], carry[...])
        p_shift = jnp.where(row == 0, first, rolled)
        carry[...] = p[tm - 1:tm, :]

        @pl.when(t == tiles_per_seq - 1)
        def _():
            h_o[0] = h[tm - 1:tm, :]
    else:
        p_shift = jnp.where(row % seq_len == 0, pprev_ref[...], rolled)
        h_o[...] = h
    pm = p + (p_shift - p) * mu_ref[...]

    xr = pm[:, 0:512]
    xk = pm[:, 512:1024]
    xv = pm[:, 1024:1536]
    xwa = pm[:, 1536:1664]
    xg = pm[:, 1664:1792]
    lane = lax.broadcasted_iota(jnp.int32, (1, LANES), 1)
    wa_in = jnp.where(lane < DECAY_LORA, jnp.tanh(xwa), xwa)
    pre = jnp.dot(wa_in.astype(BF16), wcomb_ref[...], preferred_element_type=F32)
    w_pre = pre[:, 0:512] + w0_ref[...]
    a_pre = pre[:, 512:1024] + a0_ref[...]
    z = -w_pre
    softplus = jnp.maximum(z, 0.0) + jnp.log(1.0 + jnp.exp(-jnp.abs(z)))
    w_log = -softplus - 0.5
    d_o[...] = jnp.exp(-jnp.exp(w_log))
    a_gate = 1.0 / (1.0 + jnp.exp(-a_pre))
    sg = 1.0 / (1.0 + jnp.exp(-xg))
    g_o[...] = jnp.dot(sg.astype(BF16), gup_ref[...], preferred_element_type=F32)
    kkv = xk * kk_ref[...]
    n2 = _seg_sum(kkv * kkv, j512)
    kkn = kkv / jnp.maximum(jnp.sqrt(n2), 1e-12)
    k_h = xk * (1.0 + (a_gate - 1.0) * ka_ref[...])
    r_o[...] = xr
    kh_o[...] = k_h
    vh_o[...] = xv
    a_o[...] = -kkn
    b_o[...] = kkn * a_gate
    bon_o[...] = _seg_sum(xr * k_h * rk_ref[...], j512) * xv


def _pre_call(x2, pprev, consts, *, tm, seq_len):
    n = x2.shape[0]
    tm = min(tm, n)
    n_tiles = n // tm
    per_seq = seq_len >= tm
    tiles_per_seq = seq_len // tm if per_seq else None
    n_seq = n // seq_len
    full = lambda a: pl.BlockSpec(a.shape, lambda i: (0,) * a.ndim)
    tok = lambda w: pl.BlockSpec((tm, w), lambda i: (i, 0))
    if per_seq:
        pprev_spec = pl.BlockSpec((1, 1, RWKV_PROJ), lambda i: (i // tiles_per_seq, 0, 0))
        h_spec = pl.BlockSpec((1, 1, D_MODEL), lambda i: (i // tiles_per_seq, 0, 0))
        h_shape = jax.ShapeDtypeStruct((n_seq, 1, D_MODEL), F32)
    else:
        pprev_spec = tok(RWKV_PROJ)
        h_spec = tok(D_MODEL)
        h_shape = jax.ShapeDtypeStruct((n, D_MODEL), F32)
    widths = [ATTN_WIDTH, KV_WIDTH, KV_WIDTH] + [RWKV_WIDTH] * 8
    out_shape = [jax.ShapeDtypeStruct((n, w), F32) for w in widths] + [h_shape]
    out_specs = [tok(w) for w in widths] + [h_spec]
    body = functools.partial(_pre_body, tm=tm, tiles_per_seq=tiles_per_seq, seq_len=seq_len)
    return pl.pallas_call(
        body,
        grid=(n_tiles,),
        in_specs=[tok(D_MODEL), pprev_spec] + [full(c) for c in consts],
        out_specs=out_specs,
        out_shape=out_shape,
        scratch_shapes=[pltpu.VMEM((1, RWKV_PROJ), F32)],
        compiler_params=pltpu.CompilerParams(dimension_semantics=("arbitrary",), vmem_limit_bytes=VMEM_LIMIT),
        name="pre",
    )(x2, pprev, *consts)


def _matmul_body(x_ref, w_ref, o_ref):
    o_ref[...] = jnp.dot(x_ref[...].astype(BF16), w_ref[...], preferred_element_type=F32)


def _matmul_call(x, w):
    return pl.pallas_call(_matmul_body, out_shape=jax.ShapeDtypeStruct((x.shape[0], w.shape[1]), F32),
                          name="shift_proj")(x, w)


def _softmax_sink(s, valid, sink):
    s = jnp.where(valid, s, -jnp.inf)
    m = jnp.maximum(jnp.max(s, axis=-1, keepdims=True), sink)
    p = jnp.exp(s - m)
    den = jnp.sum(p, axis=-1, keepdims=True) + jnp.exp(sink - m)
    return p / den


def _attn_prompt_body(sink_ref, q_ref, kc_ref, kp_ref, vc_ref, vp_ref, o_ref):
    n = pl.program_id(1)
    q = q_ref[0]
    kcat = jnp.concatenate([kp_ref[0], kc_ref[0]], axis=0)
    vcat = jnp.concatenate([vp_ref[0], vc_ref[0]], axis=0)
    krot = pltpu.roll(kcat, shift=HEAD_DIM, axis=1)
    vrot = pltpu.roll(vcat, shift=HEAD_DIM, axis=1)
    lane = lax.broadcasted_iota(jnp.int32, (1, LANES), 1)
    lo = lane < HEAD_DIM
    row = lax.broadcasted_iota(jnp.int32, (WINDOW, WINDOW), 0)
    col = lax.broadcasted_iota(jnp.int32, (WINDOW, WINDOW), 1)
    upper = col > row
    no_prev = upper & (n == 0)
    for hk in range(N_KV_HEADS):
        src_lo, src_hi = (kcat, krot) if hk == 0 else (krot, kcat)
        k_pads = (jnp.where(lo, src_lo, 0.0).astype(BF16), jnp.where(lo, 0.0, src_hi).astype(BF16))
        vsrc_lo, vsrc_hi = (vcat, vrot) if hk == 0 else (vrot, vcat)
        v_pads = (jnp.where(lo, vsrc_lo, 0.0).astype(BF16), jnp.where(lo, 0.0, vsrc_hi).astype(BF16))
        for pr in range(2):
            pair = hk * 2 + pr
            lanes = slice(pair * LANES, (pair + 1) * LANES)
            qp = (q[:, lanes] * SCALE).astype(BF16)
            outs = []
            inv = []
            for e in range(2):
                s = lax.dot_general(qp, k_pads[e], (((1,), (1,)), ((), ())),
                                    preferred_element_type=F32)
                s = jnp.where(upper, s[:, 0:WINDOW], s[:, WINDOW:2 * WINDOW])
                s = jnp.where(no_prev, -jnp.inf, s)
                sink = sink_ref[pair * 2 + e]
                m = jnp.maximum(jnp.max(s, axis=-1, keepdims=True), sink)
                p = jnp.exp(s - m)
                inv.append(1.0 / (jnp.sum(p, axis=-1, keepdims=True) + jnp.exp(sink - m)))
                p2 = jnp.concatenate([jnp.where(upper, p, 0.0), jnp.where(upper, 0.0, p)], axis=1)
                outs.append(jnp.dot(p2.astype(BF16), v_pads[e], preferred_element_type=F32))
            o_ref[0, :, lanes] = (outs[0] + outs[1]) * jnp.where(lo, inv[0], inv[1])


def _attn_prompt_call(sinks, q, k, v):
    b, t, _ = q.shape
    nb = t // WINDOW
    cur = lambda w: pl.BlockSpec((1, WINDOW, w), lambda i, j: (i, j, 0))
    prev = lambda w: pl.BlockSpec((1, WINDOW, w), lambda i, j: (i, jnp.maximum(j - 1, 0), 0))
    return pl.pallas_call(
        _attn_prompt_body,
        grid=(b, nb),
        in_specs=[pl.BlockSpec(memory_space=pltpu.SMEM), cur(ATTN_WIDTH), cur(KV_WIDTH), prev(KV_WIDTH),
                  cur(KV_WIDTH), prev(KV_WIDTH)],
        out_specs=cur(ATTN_WIDTH),
        out_shape=jax.ShapeDtypeStruct((b, t, ATTN_WIDTH), F32),
        compiler_params=pltpu.CompilerParams(dimension_semantics=("arbitrary", "arbitrary")),
        name="attn_prompt",
    )(sinks, q, k, k, v, v)


def _attn_sample_body(sink_ref, q_ref, k_ref, v_ref, o_ref, *, bb, rows, keys, dec_seq):
    r = lax.broadcasted_iota(jnp.int32, (rows, keys), 0)
    c = lax.broadcasted_iota(jnp.int32, (rows, keys), 1)
    t = r % dec_seq
    valid = (c > t) & (c <= t + WINDOW)
    sink = sink_ref[...]

    def one(i, _):
        q = q_ref[i].astype(BF16)
        k = k_ref[i].astype(BF16)
        s = lax.dot_general(q, k, (((1,), (1,)), ((), ())), preferred_element_type=F32) * SCALE
        probs = _softmax_sink(s, valid, sink)
        o_ref[i] = jnp.dot(probs.astype(BF16), v_ref[i].astype(BF16), preferred_element_type=F32)
        return 0

    lax.fori_loop(0, bb, one, 0, unroll=8)


def _attn_sample_call(sink_rows, q, kall, vall, *, dec_seq, bb=16):
    bsz, rows, _ = q.shape
    keys = kall.shape[1]
    blk = lambda r: pl.BlockSpec((bb, r, KV_WIDTH), lambda i: (i, 0, 0))
    body = functools.partial(_attn_sample_body, bb=bb, rows=rows, keys=keys, dec_seq=dec_seq)
    return pl.pallas_call(
        body,
        grid=(bsz // bb,),
        in_specs=[pl.BlockSpec((rows, 1), lambda i: (0, 0)), blk(rows), blk(keys), blk(keys)],
        out_specs=blk(rows),
        out_shape=jax.ShapeDtypeStruct((bsz, rows, KV_WIDTH), F32),
        compiler_params=pltpu.CompilerParams(dimension_semantics=("arbitrary",)),
        name="attn_sample",
    )(sink_rows, q, kall, vall)


IB_GROUP = 4


BATCH_GROUP = LANES // N_RWKV_HEADS
HEAD_PAIRS = N_RWKV_HEADS // 2


def _to_chains(x):
    m = jnp.concatenate([x[:, hp * LANES:(hp + 1) * LANES] for hp in range(HEAD_PAIRS)], axis=0)
    mt = m.T
    return jnp.concatenate([mt[0:HEAD_DIM, :], mt[HEAD_DIM:2 * HEAD_DIM, :]], axis=1)


def _from_chains(y):
    m = jnp.concatenate([y[:, 0:HEAD_DIM], y[:, HEAD_DIM:2 * HEAD_DIM]], axis=0)
    mt = m.T
    return jnp.concatenate([mt[hp * BATCH_GROUP:(hp + 1) * BATCH_GROUP, :] for hp in range(HEAD_PAIRS)], axis=1)


def _zero_like_dep(x):
    u = lax.bitcast_convert_type(x, jnp.uint32)
    z = lax.shift_right_logical(lax.shift_right_logical(u, jnp.uint32(16)), jnp.uint32(16))
    return lax.bitcast_convert_type(z, F32)


def _after(x, zero):
    return x + zero


def _scan_body(*refs, tb, n_tb, zero_init):
    n_in = 6 if zero_init else 7
    raw = refs[:6]
    s0_ref = None if zero_init else refs[6]
    y_out, st_ref = refs[n_in:n_in + 2]
    scratch = refs[n_in + 2:]
    slots = (scratch[0:6], scratch[6:12])
    state, sa = scratch[12:]
    s = pl.program_id(1)
    n_ib = HEAD_DIM // SUBLANES

    @pl.when(s == 0)
    def _():
        for buf in slots[1]:
            buf[...] = jnp.zeros(buf.shape, F32)
        state[...] = jnp.zeros(state.shape, F32)

    if not zero_init:
        @pl.when(s == 1)
        def _():
            def load_i(i, _):
                state[:, i, :] = _to_chains(s0_ref[:, i, :])
                return 0

            lax.fori_loop(0, HEAD_DIM, load_i, 0, unroll=8)

    def run_block(rd, wr):
        r_ref, d_ref, k_ref, v_ref, a_ref, b_ref = rd
        assert n_ib // IB_GROUP == 2

        for ib in range(n_ib):
            rows = pl.ds(ib * SUBLANES, SUBLANES)

            def init_j(j, acc, rows=rows):
                return acc + state[j, rows, :] * a_ref[0, pl.ds(j, 1), :]

            sa[rows, :] = lax.fori_loop(0, HEAD_DIM, init_j, jnp.zeros((SUBLANES, LANES), F32), unroll=8)

        def step(t, y_prev):
            y_tok = _from_chains(y_prev)
            y_out[:, jnp.maximum(t - 1, 0), :] = y_tok
            gates = {(0, 8): _zero_like_dep(y_tok[0:1, 0:LANES])}
            for n, (src, dst) in enumerate(zip(raw, wr)):
                c = _to_chains(src[:, t, :])
                dst[t] = c
                gates[(n // 3, 24 + 16 * (n % 3))] = _zero_like_dep(c[0:1, :])
            y_new = []
            tn = jnp.minimum(t + 1, tb - 1)
            for grp in range(n_ib // IB_GROUP):
                rows = [pl.ds((grp * IB_GROUP + u) * SUBLANES, SUBLANES) for u in range(IB_GROUP)]
                sav = [sa[rw, :] for rw in rows]
                vv = [v_ref[t, rw, :] for rw in rows]
                yacc = [jnp.zeros((SUBLANES, LANES), F32) for _ in range(IB_GROUP)]
                sacc = [jnp.zeros((SUBLANES, LANES), F32) for _ in range(IB_GROUP)]
                for j in range(HEAD_DIM):
                    jr = pl.ds(j, 1)
                    dj = d_ref[t, jr, :]
                    if (grp, j) in gates:
                        dj = _after(dj, gates[(grp, j)])
                    bj = b_ref[t, jr, :]
                    kj = k_ref[t, jr, :]
                    rj = r_ref[t, jr, :]
                    aj = a_ref[tn, jr, :]
                    for u in range(IB_GROUP):
                        s_new = state[j, rows[u], :] * dj + sav[u] * bj + vv[u] * kj
                        state[j, rows[u], :] = s_new
                        yacc[u] = yacc[u] + s_new * rj
                        sacc[u] = sacc[u] + s_new * aj
                for u in range(IB_GROUP):
                    sa[rows[u], :] = sacc[u]
                y_new.extend(yacc)
            return jnp.concatenate(y_new, axis=0)

        y_last = lax.fori_loop(0, tb, step, jnp.zeros((HEAD_DIM, LANES), F32))
        y_out[:, tb - 1, :] = _from_chains(y_last)

    @pl.when(s % 2 == 0)
    def _():
        run_block(slots[1], slots[0])

    @pl.when(s % 2 == 1)
    def _():
        run_block(slots[0], slots[1])

    @pl.when(s == n_tb)
    def _():
        def store_i(i, _):
            st_ref[:, i, :] = _from_chains(state[:, i, :])
            return 0

        lax.fori_loop(0, HEAD_DIM, store_i, 0, unroll=8)


def _scan_call(r, d, k, v, a, b, s0, *, tb):
    bsz, t_len = r.shape[0], r.shape[1]
    bg = BATCH_GROUP
    n_tb = t_len // tb
    op_in = pl.BlockSpec((bg, tb, RWKV_WIDTH), lambda g, s: (g, jnp.minimum(s, n_tb - 1), 0))
    op_out = pl.BlockSpec((bg, tb, RWKV_WIDTH), lambda g, s: (g, jnp.maximum(s - 1, 0), 0))
    st = pl.BlockSpec((bg, HEAD_DIM, RWKV_WIDTH), lambda g, s: (g, 0, 0))
    zero_init = s0 is None
    body = functools.partial(_scan_body, tb=tb, n_tb=n_tb, zero_init=zero_init)
    return pl.pallas_call(
        body,
        grid=(bsz // bg, n_tb + 1),
        in_specs=[op_in] * 6 + ([] if zero_init else [st]),
        out_specs=[op_out, st],
        out_shape=[jax.ShapeDtypeStruct(r.shape, F32),
                   jax.ShapeDtypeStruct((bsz, HEAD_DIM, RWKV_WIDTH), F32)],
        scratch_shapes=[pltpu.VMEM((tb, HEAD_DIM, LANES), F32)] * 12
        + [pltpu.VMEM((HEAD_DIM, HEAD_DIM, LANES), F32), pltpu.VMEM((HEAD_DIM, LANES), F32)],
        compiler_params=pltpu.CompilerParams(dimension_semantics=("arbitrary", "arbitrary"),
                                             vmem_limit_bytes=VMEM_LIMIT),
        name="wkv_scan",
    )(r, d, k, v, a, b, *([] if zero_init else [s0]))


def _post_body(x_ref, at_ref, y_ref, g_ref, bon_ref, j512_ref, lng_ref, lnb_ref, wout_ref, g2_ref, wup_ref,
               wdn_ref, o_ref, *, ff_chunk):
    j512 = j512_ref[...]
    y = y_ref[...]
    mu = _seg_sum(y, j512) * (1.0 / HEAD_DIM)
    yc = y - mu
    var = _seg_sum(yc * yc, j512) * (1.0 / HEAD_DIM)
    yn = yc * lax.rsqrt(var + GN_EPS) * lng_ref[...] + lnb_ref[...]
    rw = (yn + bon_ref[...]) * g_ref[...]
    x1 = (x_ref[...]
          + jnp.dot(at_ref[...].astype(BF16), wout_ref[0:ATTN_WIDTH, :], preferred_element_type=F32)
          + jnp.dot(rw.astype(BF16), wout_ref[ATTN_WIDTH:D_MODEL, :], preferred_element_type=F32))
    ms = jnp.mean(x1 * x1, axis=-1, keepdims=True)
    h2 = (x1 * lax.rsqrt(ms + RMS_EPS) * g2_ref[...]).astype(BF16)
    o_ref[...] = x1
    for c in range(D_FF // ff_chunk):
        u = jnp.dot(h2, wup_ref[:, c * ff_chunk:(c + 1) * ff_chunk], preferred_element_type=F32)
        u = jnp.maximum(u, 0.0)
        o_ref[...] += jnp.dot((u * u).astype(BF16), wdn_ref[c * ff_chunk:(c + 1) * ff_chunk, :],
                              preferred_element_type=F32)


def _post_call(x2, attn, y, g, bon, consts, *, tm, ff_chunk=1024):
    n = x2.shape[0]
    tm = min(tm, n)
    tok = lambda w: pl.BlockSpec((tm, w), lambda i: (i, 0))
    full = lambda a: pl.BlockSpec(a.shape, lambda i: (0,) * a.ndim, pipeline_mode=pl.Buffered(1))
    return pl.pallas_call(
        functools.partial(_post_body, ff_chunk=ff_chunk),
        grid=(n // tm,),
        in_specs=[tok(D_MODEL), tok(ATTN_WIDTH), tok(RWKV_WIDTH), tok(RWKV_WIDTH), tok(RWKV_WIDTH)]
        + [full(c) for c in consts],
        out_specs=tok(D_MODEL),
        out_shape=jax.ShapeDtypeStruct((n, D_MODEL), F32),
        compiler_params=pltpu.CompilerParams(dimension_semantics=("arbitrary",), vmem_limit_bytes=VMEM_LIMIT),
        name="post",
    )(x2, attn, y, g, bon, *consts)


def _layer(x, p_prev_rows, s0, attn_fn, w, *, tm_pre, tm_post, scan_tb):
    b, t, _ = x.shape
    x2 = x.reshape(b * t, D_MODEL)
    outs = _pre_call(x2, p_prev_rows, w["pre"], tm=tm_pre, seq_len=t)
    q, k, v, r, d, kh, vh, av, bv, g, bon, h_last = outs
    attn, new_k, new_v = attn_fn(q, k, v)
    ops = [a.reshape(b, t, RWKV_WIDTH) for a in (r, d, kh, vh, av, bv)]
    s0r = None if s0 is None else s0.astype(F32).transpose(0, 2, 1, 3).reshape(b, HEAD_DIM, RWKV_WIDTH)
    y3, s_fin = _scan_call(*ops, s0r, tb=scan_tb)
    y = y3.reshape(b * t, RWKV_WIDTH)
    s_new = s_fin.reshape(b, HEAD_DIM, N_RWKV_HEADS, HEAD_DIM).transpose(0, 2, 1, 3)
    out = _post_call(x2, attn, y, g, bon, w["post"], tm=tm_post)
    return out.reshape(b, t, D_MODEL), new_k, new_v, s_new, h_last


def _prep_weights(norm1_g, w_in, q_norm_g, k_norm_g, rwkv_mu, w_decay_0, w_decay_up, a_0, a_up, g_up, k_k, k_a,
                  r_k, ln_x_g, ln_x_b, w_out, norm2_g, w_ff_up, w_ff_down):
    o = RWKV_OFF
    pieces = ((0, o + 512), (o + 576, o + 1088), (o + 1088, o + 1600), (o + 512, o + 576), (o + 1600, o + 1664),
              (o + 1664, o + 1792))
    w_in_p = jnp.concatenate([w_in[:, lo:hi] for lo, hi in pieces], axis=1).astype(BF16)
    mu_p = jnp.concatenate([rwkv_mu[max(lo - o, 0):hi - o] for lo, hi in pieces]).reshape(1, RWKV_PROJ)
    row = lambda a: a.reshape(1, -1).astype(F32)
    eye = lambda n: jnp.kron(jnp.eye(n, dtype=F32), jnp.ones((HEAD_DIM, HEAD_DIM), F32)).astype(BF16)
    j512, j128 = eye(N_RWKV_HEADS), eye(N_KV_HEADS)
    zeros = jnp.zeros((DECAY_LORA, RWKV_WIDTH), F32)
    wcomb = jnp.concatenate([jnp.concatenate([w_decay_up, zeros], axis=1),
                             jnp.concatenate([zeros, a_up], axis=1)], axis=0).astype(BF16)
    pre = [row(norm1_g), w_in_p, mu_p, row(jnp.tile(q_norm_g, N_Q_HEADS)), row(jnp.tile(k_norm_g, N_KV_HEADS)),
           j512, j128, wcomb, row(w_decay_0), row(a_0), g_up.astype(BF16), row(k_k), row(k_a), row(r_k)]
    post = [j512, row(ln_x_g), row(ln_x_b), w_out.astype(BF16), row(norm2_g), w_ff_up.astype(BF16),
            w_ff_down.astype(BF16)]
    return {"pre": pre, "post": post, "w_rwkv": w_in_p[:, RWKV_OFF:]}


def _forward(x_prompt, x_sample, cache_k, cache_v, state_wkv, state_shift, norm1_g, w_in, q_norm_g, k_norm_g,
             attn_sinks, rwkv_mu, w_decay_0, w_decay_up, a_0, a_up, g_up, k_k, k_a, r_k, ln_x_g, ln_x_b, w_out,
             norm2_g, w_ff_up, w_ff_down, *, tm_pre=256, tm_post=512, scan_tb=32):
    depth = norm1_g.shape[0]
    assert depth == 1
    l = 0
    w = _prep_weights(norm1_g[l], w_in[l], q_norm_g[l], k_norm_g[l], rwkv_mu[l], w_decay_0[l], w_decay_up[l],
                      a_0[l], a_up[l], g_up[l], k_k[l], k_a[l], r_k[l], ln_x_g[l], ln_x_b[l], w_out[l], norm2_g[l],
                      w_ff_up[l], w_ff_down[l])
    sinks = attn_sinks[l].astype(F32)
    bp, tp, _ = x_prompt.shape
    bs, ts, _ = x_sample.shape

    def attn_prompt(q, k, v):
        q3, k3, v3 = (a.reshape(bp, tp, -1) for a in (q, k, v))
        out = _attn_prompt_call(sinks, q3, k3, v3).reshape(bp * tp, ATTN_WIDTH)
        tail = lambda a: a[:, tp - WINDOW:].reshape(bp, WINDOW, N_KV_HEADS, HEAD_DIM)
        return out, tail(k3), tail(v3)

    pprev_p = jnp.zeros((bp, 1, RWKV_PROJ), F32)
    yp, k1, v1, w1, s1 = _layer(x_prompt, pprev_p, None, attn_prompt, w, tm_pre=tm_pre, tm_post=tm_post,
                                scan_tb=scan_tb)
    s1 = s1.reshape(bp, D_MODEL)

    ck, cv = cache_k[l].astype(F32), cache_v[l].astype(F32)
    group = N_Q_HEADS // N_KV_HEADS
    rows = group * ts
    pad = (-ts) % SUBLANES
    keys = WINDOW + ts + pad

    def attn_sample(q, k, v):
        k4 = k.reshape(bs, ts, N_KV_HEADS, HEAD_DIM)
        v4 = v.reshape(bs, ts, N_KV_HEADS, HEAD_DIM)
        kall = jnp.concatenate([ck, k4], axis=1)
        vall = jnp.concatenate([cv, v4], axis=1)
        dense = lambda a: jnp.pad(a, ((0, 0), (0, pad), (0, 0), (0, 0))).reshape(bs, keys, KV_WIDTH)
        q4 = q.reshape(bs, ts, N_KV_HEADS, group, HEAD_DIM).transpose(0, 2, 3, 1, 4).reshape(
            bs, N_KV_HEADS, rows, HEAD_DIM)
        eye = jnp.eye(N_KV_HEADS, dtype=F32)
        qbd = (q4[:, :, :, None, :] * eye[None, :, None, :, None]).reshape(bs, N_KV_HEADS * rows, KV_WIDTH)
        sink_rows = jnp.repeat(sinks, ts)[:, None]
        o = _attn_sample_call(sink_rows, qbd, dense(kall), dense(vall), dec_seq=ts)
        o = o.reshape(bs, N_KV_HEADS, rows, N_KV_HEADS, HEAD_DIM)
        o = jnp.stack([o[:, hk, :, hk, :] for hk in range(N_KV_HEADS)], axis=1)
        o = o.reshape(bs, N_KV_HEADS, group, ts, HEAD_DIM).transpose(0, 3, 1, 2, 4)
        return o.reshape(bs * ts, ATTN_WIDTH), kall[:, ts:], vall[:, ts:]

    p_prev = _matmul_call(state_shift[l], w["w_rwkv"])
    pprev_s = jnp.repeat(p_prev, ts, axis=0)
    ys, k2, v2, w2, h_all = _layer(x_sample, pprev_s, state_wkv[l], attn_sample, w, tm_pre=tm_pre,
                                   tm_post=tm_post, scan_tb=ts)
    s2 = h_all.reshape(bs, ts, D_MODEL)[:, -1]
    st = lambda a: a[None]
    return (yp, ys, st(k1), st(v1), st(w1), st(s1), st(k2), st(v2), st(w2), st(s2))


def kernel(x_prompt, x_sample, cache_k, cache_v, state_wkv, state_shift, norm1_g, w_in, q_norm_g, k_norm_g, attn_sinks, rwkv_mu, w_decay_0, w_decay_up, a_0, a_up, g_up, k_k, k_a, r_k, ln_x_g, ln_x_b, w_out, norm2_g, w_ff_up, w_ff_down):
    return _forward(x_prompt, x_sample, cache_k, cache_v, state_wkv, state_shift, norm1_g, w_in, q_norm_g, k_norm_g,
                    attn_sinks, rwkv_mu, w_decay_0, w_decay_up, a_0, a_up, g_up, k_k, k_a, r_k, ln_x_g, ln_x_b,
                    w_out, norm2_g, w_ff_up, w_ff_down)
```

```python
import functools

import jax
import jax.numpy as jnp
from jax import lax
from jax.experimental import pallas as pl
from jax.experimental.pallas import tpu as pltpu

D_MODEL = 1024
HEAD_DIM = 64
ATTN_WIDTH = 512
N_Q_HEADS = 8
N_KV_HEADS = 2
KV_WIDTH = 128
RWKV_WIDTH = 512
N_RWKV_HEADS = 8
WINDOW = 128
SCALE = HEAD_DIM ** -0.5
DECAY_LORA = 64
AAA_LORA = 64
GATE_LORA = 128
D_FF = 4 * D_MODEL
RMS_EPS = 1e-6
GN_EPS = 64e-5
RWKV_OFF = ATTN_WIDTH + 2 * KV_WIDTH
RWKV_PROJ = 3 * RWKV_WIDTH + DECAY_LORA + AAA_LORA + GATE_LORA
IN_WIDTH = RWKV_OFF + RWKV_PROJ
LANES = 128
SUBLANES = 8
VMEM_LIMIT = 56 * 1024 * 1024

F32 = jnp.float32
BF16 = jnp.bfloat16


def _seg_sum(x, ones_blk):
    return jnp.dot(x.astype(BF16), ones_blk, preferred_element_type=F32)


def _pre_body(x_ref, pprev_ref, g1_ref, win_ref, mu_ref, qg_ref, kg_ref, j512_ref, j128_ref,
              wcomb_ref, w0_ref, a0_ref, gup_ref, kk_ref, ka_ref, rk_ref,
              q_o, k_o, v_o, r_o, d_o, kh_o, vh_o, a_o, b_o, g_o, bon_o, h_o,
              carry, *, tm, tiles_per_seq, seq_len):
    x = x_ref[...]
    ms = jnp.mean(x * x, axis=-1, keepdims=True)
    h = x * lax.rsqrt(ms + RMS_EPS) * g1_ref[...]
    proj = jnp.dot(h.astype(BF16), win_ref[...], preferred_element_type=F32)

    j512 = j512_ref[...]
    q = proj[:, 0:ATTN_WIDTH]
    qs = _seg_sum(q * q, j512) * (1.0 / HEAD_DIM)
    q_o[...] = q * lax.rsqrt(qs + RMS_EPS) * qg_ref[...]
    k = proj[:, ATTN_WIDTH:ATTN_WIDTH + KV_WIDTH]
    ks = _seg_sum(k * k, j128_ref[...]) * (1.0 / HEAD_DIM)
    k_o[...] = k * lax.rsqrt(ks + RMS_EPS) * kg_ref[...]
    v_o[...] = proj[:, ATTN_WIDTH + KV_WIDTH:RWKV_OFF]

    p = proj[:, RWKV_OFF:]
    rolled = pltpu.roll(p, shift=1, axis=0)
    row = lax.broadcasted_iota(jnp.int32, (tm, 1), 0)
    if tiles_per_seq is not None:
        t = pl.program_id(0) % tiles_per_seq
        first = jnp.where(t == 0, pprev_ref[0], carry[...])
        p_shift = jnp.where(row == 0, first, rolled)
        carry[...] = p[tm - 1:tm, :]

        @pl.when(t == tiles_per_seq - 1)
        def _():
            h_o[0] = h[tm - 1:tm, :]
    else:
        p_shift = jnp.where(row % seq_len == 0, pprev_ref[...], rolled)
        h_o[...] = h
    pm = p + (p_shift - p) * mu_ref[...]

    xr = pm[:, 0:512]
    xk = pm[:, 512:1024]
    xv = pm[:, 1024:1536]
    xwa = pm[:, 1536:1664]
    xg = pm[:, 1664:1792]
    lane = lax.broadcasted_iota(jnp.int32, (1, LANES), 1)
    wa_in = jnp.where(lane < DECAY_LORA, jnp.tanh(xwa), xwa)
    pre = jnp.dot(wa_in.astype(BF16), wcomb_ref[...], preferred_element_type=F32)
    w_pre = pre[:, 0:512] + w0_ref[...]
    a_pre = pre[:, 512:1024] + a0_ref[...]
    z = -w_pre
    softplus = jnp.maximum(z, 0.0) + jnp.log(1.0 + jnp.exp(-jnp.abs(z)))
    w_log = -softplus - 0.5
    d_o[...] = jnp.exp(-jnp.exp(w_log))
    a_gate = 1.0 / (1.0 + jnp.exp(-a_pre))
    sg = 1.0 / (1.0 + jnp.exp(-xg))
    g_o[...] = jnp.dot(sg.astype(BF16), gup_ref[...], preferred_element_type=F32)
    kkv = xk * kk_ref[...]
    n2 = _seg_sum(kkv * kkv, j512)
    kkn = kkv / jnp.maximum(jnp.sqrt(n2), 1e-12)
    k_h = xk * (1.0 + (a_gate - 1.0) * ka_ref[...])
    r_o[...] = xr
    kh_o[...] = k_h
    vh_o[...] = xv
    a_o[...] = -kkn
    b_o[...] = kkn * a_gate
    bon_o[...] = _seg_sum(xr * k_h * rk_ref[...], j512) * xv


def _pre_call(x2, pprev, consts, *, tm, seq_len):
    n = x2.shape[0]
    tm = min(tm, n)
    n_tiles = n // tm
    per_seq = seq_len >= tm
    tiles_per_seq = seq_len // tm if per_seq else None
    n_seq = n // seq_len
    full = lambda a: pl.BlockSpec(a.shape, lambda i: (0,) * a.ndim)
    tok = lambda w: pl.BlockSpec((tm, w), lambda i: (i, 0))
    if per_seq:
        pprev_spec = pl.BlockSpec((1, 1, RWKV_PROJ), lambda i: (i // tiles_per_seq, 0, 0))
        h_spec = pl.BlockSpec((1, 1, D_MODEL), lambda i: (i // tiles_per_seq, 0, 0))
        h_shape = jax.ShapeDtypeStruct((n_seq, 1, D_MODEL), F32)
    else:
        pprev_spec = tok(RWKV_PROJ)
        h_spec = tok(D_MODEL)
        h_shape = jax.ShapeDtypeStruct((n, D_MODEL), F32)
    widths = [ATTN_WIDTH, KV_WIDTH, KV_WIDTH] + [RWKV_WIDTH] * 8
    out_shape = [jax.ShapeDtypeStruct((n, w), F32) for w in widths] + [h_shape]
    out_specs = [tok(w) for w in widths] + [h_spec]
    body = functools.partial(_pre_body, tm=tm, tiles_per_seq=tiles_per_seq, seq_len=seq_len)
    return pl.pallas_call(
        body,
        grid=(n_tiles,),
        in_specs=[tok(D_MODEL), pprev_spec] + [full(c) for c in consts],
        out_specs=out_specs,
        out_shape=out_shape,
        scratch_shapes=[pltpu.VMEM((1, RWKV_PROJ), F32)],
        compiler_params=pltpu.CompilerParams(dimension_semantics=("arbitrary",), vmem_limit_bytes=VMEM_LIMIT),
        name="pre",
    )(x2, pprev, *consts)


def _matmul_body(x_ref, w_ref, o_ref):
    o_ref[...] = jnp.dot(x_ref[...].astype(BF16), w_ref[...], preferred_element_type=F32)


def _matmul_call(x, w):
    return pl.pallas_call(_matmul_body, out_shape=jax.ShapeDtypeStruct((x.shape[0], w.shape[1]), F32),
                          name="shift_proj")(x, w)


def _softmax_sink(s, valid, sink):
    s = jnp.where(valid, s, -jnp.inf)
    m = jnp.maximum(jnp.max(s, axis=-1, keepdims=True), sink)
    p = jnp.exp(s - m)
    den = jnp.sum(p, axis=-1, keepdims=True) + jnp.exp(sink - m)
    return p / den


def _attn_prompt_body(sink_ref, q_ref, kc_ref, kp_ref, vc_ref, vp_ref, o_ref):
    n = pl.program_id(1)
    q = q_ref[0]
    kcat = jnp.concatenate([kp_ref[0], kc_ref[0]], axis=0)
    vcat = jnp.concatenate([vp_ref[0], vc_ref[0]], axis=0)
    krot = pltpu.roll(kcat, shift=HEAD_DIM, axis=1)
    vrot = pltpu.roll(vcat, shift=HEAD_DIM, axis=1)
    lane = lax.broadcasted_iota(jnp.int32, (1, LANES), 1)
    lo = lane < HEAD_DIM
    row = lax.broadcasted_iota(jnp.int32, (WINDOW, WINDOW), 0)
    col = lax.broadcasted_iota(jnp.int32, (WINDOW, WINDOW), 1)
    upper = col > row
    no_prev = upper & (n == 0)
    for hk in range(N_KV_HEADS):
        src_lo, src_hi = (kcat, krot) if hk == 0 else (krot, kcat)
        k_pads = (jnp.where(lo, src_lo, 0.0).astype(BF16), jnp.where(lo, 0.0, src_hi).astype(BF16))
        vsrc_lo, vsrc_hi = (vcat, vrot) if hk == 0 else (vrot, vcat)
        v_pads = (jnp.where(lo, vsrc_lo, 0.0).astype(BF16), jnp.where(lo, 0.0, vsrc_hi).astype(BF16))
        for pr in range(2):
            pair = hk * 2 + pr
            lanes = slice(pair * LANES, (pair + 1) * LANES)
            qp = (q[:, lanes] * SCALE).astype(BF16)
            outs = []
            inv = []
            for e in range(2):
                s = lax.dot_general(qp, k_pads[e], (((1,), (1,)), ((), ())),
                                    preferred_element_type=F32)
                s = jnp.where(upper, s[:, 0:WINDOW], s[:, WINDOW:2 * WINDOW])
                s = jnp.where(no_prev, -jnp.inf, s)
                sink = sink_ref[pair * 2 + e]
                m = jnp.maximum(jnp.max(s, axis=-1, keepdims=True), sink)
                p = jnp.exp(s - m)
                inv.append(1.0 / (jnp.sum(p, axis=-1, keepdims=True) + jnp.exp(sink - m)))
                p2 = jnp.concatenate([jnp.where(upper, p, 0.0), jnp.where(upper, 0.0, p)], axis=1)
                outs.append(jnp.dot(p2.astype(BF16), v_pads[e], preferred_element_type=F32))
            o_ref[0, :, lanes] = (outs[0] + outs[1]) * jnp.where(lo, inv[0], inv[1])


def _attn_prompt_call(sinks, q, k, v):
    b, t, _ = q.shape
    nb = t // WINDOW
    cur = lambda w: pl.BlockSpec((1, WINDOW, w), lambda i, j: (i, j, 0))
    prev = lambda w: pl.BlockSpec((1, WINDOW, w), lambda i, j: (i, jnp.maximum(j - 1, 0), 0))
    return pl.pallas_call(
        _attn_prompt_body,
        grid=(b, nb),
        in_specs=[pl.BlockSpec(memory_space=pltpu.SMEM), cur(ATTN_WIDTH), cur(KV_WIDTH), prev(KV_WIDTH),
                  cur(KV_WIDTH), prev(KV_WIDTH)],
        out_specs=cur(ATTN_WIDTH),
        out_shape=jax.ShapeDtypeStruct((b, t, ATTN_WIDTH), F32),
        compiler_params=pltpu.CompilerParams(dimension_semantics=("arbitrary", "arbitrary")),
        name="attn_prompt",
    )(sinks, q, k, k, v, v)


def _attn_sample_body(sink_ref, q_ref, k_ref, v_ref, o_ref, *, bb, rows, keys, dec_seq):
    r = lax.broadcasted_iota(jnp.int32, (rows, keys), 0)
    c = lax.broadcasted_iota(jnp.int32, (rows, keys), 1)
    t = r % dec_seq
    valid = (c > t) & (c <= t + WINDOW)
    sink = sink_ref[...]

    def one(i, _):
        q = q_ref[i].astype(BF16)
        k = k_ref[i].astype(BF16)
        s = lax.dot_general(q, k, (((1,), (1,)), ((), ())), preferred_element_type=F32) * SCALE
        probs = _softmax_sink(s, valid, sink)
        o_ref[i] = jnp.dot(probs.astype(BF16), v_ref[i].astype(BF16), preferred_element_type=F32)
        return 0

    lax.fori_loop(0, bb, one, 0, unroll=8)


def _attn_sample_call(sink_rows, q, kall, vall, *, dec_seq, bb=16):
    bsz, rows, _ = q.shape
    keys = kall.shape[1]
    blk = lambda r: pl.BlockSpec((bb, r, KV_WIDTH), lambda i: (i, 0, 0))
    body = functools.partial(_attn_sample_body, bb=bb, rows=rows, keys=keys, dec_seq=dec_seq)
    return pl.pallas_call(
        body,
        grid=(bsz // bb,),
        in_specs=[pl.BlockSpec((rows, 1), lambda i: (0, 0)), blk(rows), blk(keys), blk(keys)],
        out_specs=blk(rows),
        out_shape=jax.ShapeDtypeStruct((bsz, rows, KV_WIDTH), F32),
        compiler_params=pltpu.CompilerParams(dimension_semantics=("arbitrary",)),
        name="attn_sample",
    )(sink_rows, q, kall, vall)


IB_GROUP = 4


BATCH_GROUP = LANES // N_RWKV_HEADS
HEAD_PAIRS = N_RWKV_HEADS // 2


def _to_chains(x):
    m = jnp.concatenate([x[:, hp * LANES:(hp + 1) * LANES] for hp in range(HEAD_PAIRS)], axis=0)
    mt = m.T
    return jnp.concatenate([mt[0:HEAD_DIM, :], mt[HEAD_DIM:2 * HEAD_DIM, :]], axis=1)


def _from_chains(y):
    m = jnp.concatenate([y[:, 0:HEAD_DIM], y[:, HEAD_DIM:2 * HEAD_DIM]], axis=0)
    mt = m.T
    return jnp.concatenate([mt[hp * BATCH_GROUP:(hp + 1) * BATCH_GROUP, :] for hp in range(HEAD_PAIRS)], axis=1)


def _zero_like_dep(x):
    u = lax.bitcast_convert_type(x, jnp.uint32)
    z = lax.shift_right_logical(lax.shift_right_logical(u, jnp.uint32(16)), jnp.uint32(16))
    return lax.bitcast_convert_type(z, F32)


def _after(x, zero):
    return x + zero


def _scan_body(*refs, tb, n_tb, zero_init):
    n_in = 6 if zero_init else 7
    raw = refs[:6]
    s0_ref = None if zero_init else refs[6]
    y_out, st_ref = refs[n_in:n_in + 2]
    scratch = refs[n_in + 2:]
    slots = (scratch[0:6], scratch[6:12])
    state, sa, g_run = scratch[12:]
    s = pl.program_id(1)
    n_ib = HEAD_DIM // SUBLANES

    @pl.when(s == 0)
    def _():
        for buf in slots[1][:5]:
            buf[...] = jnp.zeros(buf.shape, F32)
        slots[0][5][...] = jnp.ones((HEAD_DIM, LANES), F32)
        slots[1][5][...] = jnp.ones((HEAD_DIM, LANES), F32)
        state[...] = jnp.zeros(state.shape, F32)

    if not zero_init:
        @pl.when(s == 1)
        def _():
            def load_i(i, _):
                state[:, i, :] = _to_chains(s0_ref[:, i, :])
                return 0

            lax.fori_loop(0, HEAD_DIM, load_i, 0, unroll=8)

    def run_block(rd, wr):
        r_ref, k_ref, v_ref, a_ref, b_ref, _ = rd
        wr_r, wr_k, wr_v, wr_a, wr_b, wr_g = wr
        assert n_ib // IB_GROUP == 2

        for ib in range(n_ib):
            rows = pl.ds(ib * SUBLANES, SUBLANES)

            def init_j(j, acc, rows=rows):
                jr = pl.ds(j, 1)
                s_full = state[j, rows, :] * wr_g[jr, :]
                state[j, rows, :] = s_full
                return acc + s_full * a_ref[0, jr, :]

            sa[rows, :] = lax.fori_loop(0, HEAD_DIM, init_j, jnp.zeros((SUBLANES, LANES), F32), unroll=8)
        g_run[...] = jnp.ones((HEAD_DIM, LANES), F32)

        def step(t, y_prev):
            y_tok = _from_chains(y_prev)
            y_out[:, jnp.maximum(t - 1, 0), :] = y_tok
            gates = {(0, 8): _zero_like_dep(y_tok[0:1, 0:LANES])}
            c_r, c_d, c_k, c_v, c_a, c_b = (_to_chains(src[:, t, :]) for src in raw)
            g_prev = g_run[...]
            g = g_prev * c_d
            g_inv = 1.0 / g
            g_run[...] = g
            staged = ((wr_r, c_r * g), (wr_k, c_k * g_inv), (wr_v, c_v), (wr_a, c_a * g_prev), (wr_b, c_b * g_inv))
            for n, (dst, val) in enumerate(staged):
                dst[t] = val
                gates[(n // 3, 24 + 16 * (n % 3))] = _zero_like_dep(val[0:1, :])
            y_new = []
            tn = jnp.minimum(t + 1, tb - 1)
            for grp in range(n_ib // IB_GROUP):
                rows = [pl.ds((grp * IB_GROUP + u) * SUBLANES, SUBLANES) for u in range(IB_GROUP)]
                sav = [sa[rw, :] for rw in rows]
                vv = [v_ref[t, rw, :] for rw in rows]
                yacc = [jnp.zeros((SUBLANES, LANES), F32) for _ in range(IB_GROUP)]
                sacc = [jnp.zeros((SUBLANES, LANES), F32) for _ in range(IB_GROUP)]
                for j in range(HEAD_DIM):
                    jr = pl.ds(j, 1)
                    bj = b_ref[t, jr, :]
                    if (grp, j) in gates:
                        bj = _after(bj, gates[(grp, j)])
                    kj = k_ref[t, jr, :]
                    rj = r_ref[t, jr, :]
                    aj = a_ref[tn, jr, :]
                    for u in range(IB_GROUP):
                        s_new = state[j, rows[u], :] + sav[u] * bj + vv[u] * kj
                        state[j, rows[u], :] = s_new
                        yacc[u] = yacc[u] + s_new * rj
                        sacc[u] = sacc[u] + s_new * aj
                for u in range(IB_GROUP):
                    sa[rows[u], :] = sacc[u]
                y_new.extend(yacc)
            return jnp.concatenate(y_new, axis=0)

        y_last = lax.fori_loop(0, tb, step, jnp.zeros((HEAD_DIM, LANES), F32))
        y_out[:, tb - 1, :] = _from_chains(y_last)
        wr_g[...] = g_run[...]

    @pl.when(s % 2 == 0)
    def _():
        run_block(slots[1], slots[0])

    @pl.when(s % 2 == 1)
    def _():
        run_block(slots[0], slots[1])

    @pl.when(s == n_tb)
    def _():
        g_last = slots[(n_tb - 1) % 2][5][...]

        def store_i(i, _):
            st_ref[:, i, :] = _from_chains(state[:, i, :] * g_last)
            return 0

        lax.fori_loop(0, HEAD_DIM, store_i, 0, unroll=8)


def _scan_call(r, d, k, v, a, b, s0, *, tb):
    bsz, t_len = r.shape[0], r.shape[1]
    bg = BATCH_GROUP
    n_tb = t_len // tb
    op_in = pl.BlockSpec((bg, tb, RWKV_WIDTH), lambda g, s: (g, jnp.minimum(s, n_tb - 1), 0))
    op_out = pl.BlockSpec((bg, tb, RWKV_WIDTH), lambda g, s: (g, jnp.maximum(s - 1, 0), 0))
    st = pl.BlockSpec((bg, HEAD_DIM, RWKV_WIDTH), lambda g, s: (g, 0, 0))
    zero_init = s0 is None
    body = functools.partial(_scan_body, tb=tb, n_tb=n_tb, zero_init=zero_init)
    return pl.pallas_call(
        body,
        grid=(bsz // bg, n_tb + 1),
        in_specs=[op_in] * 6 + ([] if zero_init else [st]),
        out_specs=[op_out, st],
        out_shape=[jax.ShapeDtypeStruct(r.shape, F32),
                   jax.ShapeDtypeStruct((bsz, HEAD_DIM, RWKV_WIDTH), F32)],
        scratch_shapes=([pltpu.VMEM((tb, HEAD_DIM, LANES), F32)] * 5 + [pltpu.VMEM((HEAD_DIM, LANES), F32)]) * 2
        + [pltpu.VMEM((HEAD_DIM, HEAD_DIM, LANES), F32), pltpu.VMEM((HEAD_DIM, LANES), F32),
           pltpu.VMEM((HEAD_DIM, LANES), F32)],
        compiler_params=pltpu.CompilerParams(dimension_semantics=("arbitrary", "arbitrary"),
                                             vmem_limit_bytes=VMEM_LIMIT),
        name="wkv_scan",
    )(r, d, k, v, a, b, *([] if zero_init else [s0]))


def _post_body(x_ref, at_ref, y_ref, g_ref, bon_ref, j512_ref, lng_ref, lnb_ref, wout_ref, g2_ref, wup_ref,
               wdn_ref, o_ref, *, ff_chunk):
    j512 = j512_ref[...]
    y = y_ref[...]
    mu = _seg_sum(y, j512) * (1.0 / HEAD_DIM)
    yc = y - mu
    var = _seg_sum(yc * yc, j512) * (1.0 / HEAD_DIM)
    yn = yc * lax.rsqrt(var + GN_EPS) * lng_ref[...] + lnb_ref[...]
    rw = (yn + bon_ref[...]) * g_ref[...]
    x1 = (x_ref[...]
          + jnp.dot(at_ref[...].astype(BF16), wout_ref[0:ATTN_WIDTH, :], preferred_element_type=F32)
          + jnp.dot(rw.astype(BF16), wout_ref[ATTN_WIDTH:D_MODEL, :], preferred_element_type=F32))
    ms = jnp.mean(x1 * x1, axis=-1, keepdims=True)
    h2 = (x1 * lax.rsqrt(ms + RMS_EPS) * g2_ref[...]).astype(BF16)
    o_ref[...] = x1
    for c in range(D_FF // ff_chunk):
        u = jnp.dot(h2, wup_ref[:, c * ff_chunk:(c + 1) * ff_chunk], preferred_element_type=F32)
        u = jnp.maximum(u, 0.0)
        o_ref[...] += jnp.dot((u * u).astype(BF16), wdn_ref[c * ff_chunk:(c + 1) * ff_chunk, :],
                              preferred_element_type=F32)


def _post_call(x2, attn, y, g, bon, consts, *, tm, ff_chunk=1024):
    n = x2.shape[0]
    tm = min(tm, n)
    tok = lambda w: pl.BlockSpec((tm, w), lambda i: (i, 0))
    full = lambda a: pl.BlockSpec(a.shape, lambda i: (0,) * a.ndim, pipeline_mode=pl.Buffered(1))
    return pl.pallas_call(
        functools.partial(_post_body, ff_chunk=ff_chunk),
        grid=(n // tm,),
        in_specs=[tok(D_MODEL), tok(ATTN_WIDTH), tok(RWKV_WIDTH), tok(RWKV_WIDTH), tok(RWKV_WIDTH)]
        + [full(c) for c in consts],
        out_specs=tok(D_MODEL),
        out_shape=jax.ShapeDtypeStruct((n, D_MODEL), F32),
        compiler_params=pltpu.CompilerParams(dimension_semantics=("arbitrary",), vmem_limit_bytes=VMEM_LIMIT),
        name="post",
    )(x2, attn, y, g, bon, *consts)


def _layer(x, p_prev_rows, s0, attn_fn, w, *, tm_pre, tm_post, scan_tb):
    b, t, _ = x.shape
    x2 = x.reshape(b * t, D_MODEL)
    outs = _pre_call(x2, p_prev_rows, w["pre"], tm=tm_pre, seq_len=t)
    q, k, v, r, d, kh, vh, av, bv, g, bon, h_last = outs
    attn, new_k, new_v = attn_fn(q, k, v)
    ops = [a.reshape(b, t, RWKV_WIDTH) for a in (r, d, kh, vh, av, bv)]
    s0r = None if s0 is None else s0.astype(F32).transpose(0, 2, 1, 3).reshape(b, HEAD_DIM, RWKV_WIDTH)
    y3, s_fin = _scan_call(*ops, s0r, tb=scan_tb)
    y = y3.reshape(b * t, RWKV_WIDTH)
    s_new = s_fin.reshape(b, HEAD_DIM, N_RWKV_HEADS, HEAD_DIM).transpose(0, 2, 1, 3)
    out = _post_call(x2, attn, y, g, bon, w["post"], tm=tm_post)
    return out.reshape(b, t, D_MODEL), new_k, new_v, s_new, h_last


def _prep_weights(norm1_g, w_in, q_norm_g, k_norm_g, rwkv_mu, w_decay_0, w_decay_up, a_0, a_up, g_up, k_k, k_a,
                  r_k, ln_x_g, ln_x_b, w_out, norm2_g, w_ff_up, w_ff_down):
    o = RWKV_OFF
    pieces = ((0, o + 512), (o + 576, o + 1088), (o + 1088, o + 1600), (o + 512, o + 576), (o + 1600, o + 1664),
              (o + 1664, o + 1792))
    w_in_p = jnp.concatenate([w_in[:, lo:hi] for lo, hi in pieces], axis=1).astype(BF16)
    mu_p = jnp.concatenate([rwkv_mu[max(lo - o, 0):hi - o] for lo, hi in pieces]).reshape(1, RWKV_PROJ)
    row = lambda a: a.reshape(1, -1).astype(F32)
    eye = lambda n: jnp.kron(jnp.eye(n, dtype=F32), jnp.ones((HEAD_DIM, HEAD_DIM), F32)).astype(BF16)
    j512, j128 = eye(N_RWKV_HEADS), eye(N_KV_HEADS)
    zeros = jnp.zeros((DECAY_LORA, RWKV_WIDTH), F32)
    wcomb = jnp.concatenate([jnp.concatenate([w_decay_up, zeros], axis=1),
                             jnp.concatenate([zeros, a_up], axis=1)], axis=0).astype(BF16)
    pre = [row(norm1_g), w_in_p, mu_p, row(jnp.tile(q_norm_g, N_Q_HEADS)), row(jnp.tile(k_norm_g, N_KV_HEADS)),
           j512, j128, wcomb, row(w_decay_0), row(a_0), g_up.astype(BF16), row(k_k), row(k_a), row(r_k)]
    post = [j512, row(ln_x_g), row(ln_x_b), w_out.astype(BF16), row(norm2_g), w_ff_up.astype(BF16),
            w_ff_down.astype(BF16)]
    return {"pre": pre, "post": post, "w_rwkv": w_in_p[:, RWKV_OFF:]}


def _forward(x_prompt, x_sample, cache_k, cache_v, state_wkv, state_shift, norm1_g, w_in, q_norm_g, k_norm_g,
             attn_sinks, rwkv_mu, w_decay_0, w_decay_up, a_0, a_up, g_up, k_k, k_a, r_k, ln_x_g, ln_x_b, w_out,
             norm2_g, w_ff_up, w_ff_down, *, tm_pre=256, tm_post=512, scan_tb=32):
    depth = norm1_g.shape[0]
    assert depth == 1
    l = 0
    w = _prep_weights(norm1_g[l], w_in[l], q_norm_g[l], k_norm_g[l], rwkv_mu[l], w_decay_0[l], w_decay_up[l],
                      a_0[l], a_up[l], g_up[l], k_k[l], k_a[l], r_k[l], ln_x_g[l], ln_x_b[l], w_out[l], norm2_g[l],
                      w_ff_up[l], w_ff_down[l])
    sinks = attn_sinks[l].astype(F32)
    bp, tp, _ = x_prompt.shape
    bs, ts, _ = x_sample.shape

    def attn_prompt(q, k, v):
        q3, k3, v3 = (a.reshape(bp, tp, -1) for a in (q, k, v))
        out = _attn_prompt_call(sinks, q3, k3, v3).reshape(bp * tp, ATTN_WIDTH)
        tail = lambda a: a[:, tp - WINDOW:].reshape(bp, WINDOW, N_KV_HEADS, HEAD_DIM)
        return out, tail(k3), tail(v3)

    pprev_p = jnp.zeros((bp, 1, RWKV_PROJ), F32)
    yp, k1, v1, w1, s1 = _layer(x_prompt, pprev_p, None, attn_prompt, w, tm_pre=tm_pre, tm_post=tm_post,
                                scan_tb=scan_tb)
    s1 = s1.reshape(bp, D_MODEL)

    ck, cv = cache_k[l].astype(F32), cache_v[l].astype(F32)
    group = N_Q_HEADS // N_KV_HEADS
    rows = group * ts
    pad = (-ts) % SUBLANES
    keys = WINDOW + ts + pad

    def attn_sample(q, k, v):
        k4 = k.reshape(bs, ts, N_KV_HEADS, HEAD_DIM)
        v4 = v.reshape(bs, ts, N_KV_HEADS, HEAD_DIM)
        kall = jnp.concatenate([ck, k4], axis=1)
        vall = jnp.concatenate([cv, v4], axis=1)
        dense = lambda a: jnp.pad(a, ((0, 0), (0, pad), (0, 0), (0, 0))).reshape(bs, keys, KV_WIDTH)
        q4 = q.reshape(bs, ts, N_KV_HEADS, group, HEAD_DIM).transpose(0, 2, 3, 1, 4).reshape(
            bs, N_KV_HEADS, rows, HEAD_DIM)
        eye = jnp.eye(N_KV_HEADS, dtype=F32)
        qbd = (q4[:, :, :, None, :] * eye[None, :, None, :, None]).reshape(bs, N_KV_HEADS * rows, KV_WIDTH)
        sink_rows = jnp.repeat(sinks, ts)[:, None]
        o = _attn_sample_call(sink_rows, qbd, dense(kall), dense(vall), dec_seq=ts)
        o = o.reshape(bs, N_KV_HEADS, rows, N_KV_HEADS, HEAD_DIM)
        o = jnp.stack([o[:, hk, :, hk, :] for hk in range(N_KV_HEADS)], axis=1)
        o = o.reshape(bs, N_KV_HEADS, group, ts, HEAD_DIM).transpose(0, 3, 1, 2, 4)
        return o.reshape(bs * ts, ATTN_WIDTH), kall[:, ts:], vall[:, ts:]

    p_prev = _matmul_call(state_shift[l], w["w_rwkv"])
    pprev_s = jnp.repeat(p_prev, ts, axis=0)
    ys, k2, v2, w2, h_all = _layer(x_sample, pprev_s, state_wkv[l], attn_sample, w, tm_pre=tm_pre,
                                   tm_post=tm_post, scan_tb=ts)
    s2 = h_all.reshape(bs, ts, D_MODEL)[:, -1]
    st = lambda a: a[None]
    return (yp, ys, st(k1), st(v1), st(w1), st(s1), st(k2), st(v2), st(w2), st(s2))


def kernel(x_prompt, x_sample, cache_k, cache_v, state_wkv, state_shift, norm1_g, w_in, q_norm_g, k_norm_g, attn_sinks, rwkv_mu, w_decay_0, w_decay_up, a_0, a_up, g_up, k_k, k_a, r_k, ln_x_g, ln_x_b, w_out, norm2_g, w_ff_up, w_ff_down):
    return _forward(x_prompt, x_sample, cache_k, cache_v, state_wkv, state_shift, norm1_g, w_in, q_norm_g, k_norm_g,
                    attn_sinks, rwkv_mu, w_decay_0, w_decay_up, a_0, a_up, g_up, k_k, k_a, r_k, ln_x_g, ln_x_b,
                    w_out, norm2_g, w_ff_up, w_ff_down)
```

```python
import functools

import jax
import jax.numpy as jnp
from jax import lax
from jax.experimental import pallas as pl
from jax.experimental.pallas import tpu as pltpu

D_MODEL = 1024
HEAD_DIM = 64
ATTN_WIDTH = 512
N_Q_HEADS = 8
N_KV_HEADS = 2
KV_WIDTH = 128
RWKV_WIDTH = 512
N_RWKV_HEADS = 8
WINDOW = 128
SCALE = HEAD_DIM ** -0.5
DECAY_LORA = 64
AAA_LORA = 64
GATE_LORA = 128
D_FF = 4 * D_MODEL
RMS_EPS = 1e-6
GN_EPS = 64e-5
RWKV_OFF = ATTN_WIDTH + 2 * KV_WIDTH
RWKV_PROJ = 3 * RWKV_WIDTH + DECAY_LORA + AAA_LORA + GATE_LORA
IN_WIDTH = RWKV_OFF + RWKV_PROJ
LANES = 128
SUBLANES = 8
VMEM_LIMIT = 56 * 1024 * 1024

F32 = jnp.float32
BF16 = jnp.bfloat16


def _seg_sum(x, ones_blk):
    return jnp.dot(x.astype(BF16), ones_blk, preferred_element_type=F32)


def _pre_body(x_ref, pprev_ref, g1_ref, win_ref, mu_ref, qg_ref, kg_ref, j512_ref, j128_ref,
              wcomb_ref, w0_ref, a0_ref, gup_ref, kk_ref, ka_ref, rk_ref,
              q_o, k_o, v_o, r_o, d_o, kh_o, vh_o, a_o, b_o, g_o, bon_o, h_o,
              carry, *, tm, tiles_per_seq, seq_len):
    x = x_ref[...]
    ms = jnp.mean(x * x, axis=-1, keepdims=True)
    h = x * lax.rsqrt(ms + RMS_EPS) * g1_ref[...]
    proj = jnp.dot(h.astype(BF16), win_ref[...], preferred_element_type=F32)

    j512 = j512_ref[...]
    q = proj[:, 0:ATTN_WIDTH]
    qs = _seg_sum(q * q, j512) * (1.0 / HEAD_DIM)
    q_o[...] = q * lax.rsqrt(qs + RMS_EPS) * qg_ref[...]
    k = proj[:, ATTN_WIDTH:ATTN_WIDTH + KV_WIDTH]
    ks = _seg_sum(k * k, j128_ref[...]) * (1.0 / HEAD_DIM)
    k_o[...] = k * lax.rsqrt(ks + RMS_EPS) * kg_ref[...]
    v_o[...] = proj[:, ATTN_WIDTH + KV_WIDTH:RWKV_OFF]

    p = proj[:, RWKV_OFF:]
    rolled = pltpu.roll(p, shift=1, axis=0)
    row = lax.broadcasted_iota(jnp.int32, (tm, 1), 0)
    if tiles_per_seq is not None:
        t = pl.program_id(0) % tiles_per_seq
        first = jnp.where(t == 0, pprev_ref[0], carry[...])
        p_shift = jnp.where(row == 0, first, rolled)
        carry[...] = p[tm - 1:tm, :]

        @pl.when(t == tiles_per_seq - 1)
        def _():
            h_o[0] = h[tm - 1:tm, :]
    else:
        p_shift = jnp.where(row % seq_len == 0, pprev_ref[...], rolled)
        h_o[...] = h
    pm = p + (p_shift - p) * mu_ref[...]

    xr = pm[:, 0:512]
    xk = pm[:, 512:1024]
    xv = pm[:, 1024:1536]
    xwa = pm[:, 1536:1664]
    xg = pm[:, 1664:1792]
    lane = lax.broadcasted_iota(jnp.int32, (1, LANES), 1)
    wa_in = jnp.where(lane < DECAY_LORA, jnp.tanh(xwa), xwa)
    pre = jnp.dot(wa_in.astype(BF16), wcomb_ref[...], preferred_element_type=F32)
    w_pre = pre[:, 0:512] + w0_ref[...]
    a_pre = pre[:, 512:1024] + a0_ref[...]
    z = -w_pre
    softplus = jnp.maximum(z, 0.0) + jnp.log(1.0 + jnp.exp(-jnp.abs(z)))
    w_log = -softplus - 0.5
    d_o[...] = jnp.exp(-jnp.exp(w_log))
    a_gate = 1.0 / (1.0 + jnp.exp(-a_pre))
    sg = 1.0 / (1.0 + jnp.exp(-xg))
    g_o[...] = jnp.dot(sg.astype(BF16), gup_ref[...], preferred_element_type=F32)
    kkv = xk * kk_ref[...]
    n2 = _seg_sum(kkv * kkv, j512)
    kkn = kkv / jnp.maximum(jnp.sqrt(n2), 1e-12)
    k_h = xk * (1.0 + (a_gate - 1.0) * ka_ref[...])
    r_o[...] = xr
    kh_o[...] = k_h
    vh_o[...] = xv
    a_o[...] = -kkn
    b_o[...] = kkn * a_gate
    bon_o[...] = _seg_sum(xr * k_h * rk_ref[...], j512) * xv


def _pre_call(x2, pprev, consts, *, tm, seq_len):
    n = x2.shape[0]
    tm = min(tm, n)
    n_tiles = n // tm
    per_seq = seq_len >= tm
    tiles_per_seq = seq_len // tm if per_seq else None
    n_seq = n // seq_len
    full = lambda a: pl.BlockSpec(a.shape, lambda i: (0,) * a.ndim)
    tok = lambda w: pl.BlockSpec((tm, w), lambda i: (i, 0))
    if per_seq:
        pprev_spec = pl.BlockSpec((1, 1, RWKV_PROJ), lambda i: (i // tiles_per_seq, 0, 0))
        h_spec = pl.BlockSpec((1, 1, D_MODEL), lambda i: (i // tiles_per_seq, 0, 0))
        h_shape = jax.ShapeDtypeStruct((n_seq, 1, D_MODEL), F32)
    else:
        pprev_spec = tok(RWKV_PROJ)
        h_spec = tok(D_MODEL)
        h_shape = jax.ShapeDtypeStruct((n, D_MODEL), F32)
    widths = [ATTN_WIDTH, KV_WIDTH, KV_WIDTH] + [RWKV_WIDTH] * 8
    out_shape = [jax.ShapeDtypeStruct((n, w), F32) for w in widths] + [h_shape]
    out_specs = [tok(w) for w in widths] + [h_spec]
    body = functools.partial(_pre_body, tm=tm, tiles_per_seq=tiles_per_seq, seq_len=seq_len)
    return pl.pallas_call(
        body,
        grid=(n_tiles,),
        in_specs=[tok(D_MODEL), pprev_spec] + [full(c) for c in consts],
        out_specs=out_specs,
        out_shape=out_shape,
        scratch_shapes=[pltpu.VMEM((1, RWKV_PROJ), F32)],
        compiler_params=pltpu.CompilerParams(dimension_semantics=("arbitrary",), vmem_limit_bytes=VMEM_LIMIT),
        name="pre",
    )(x2, pprev, *consts)


def _matmul_body(x_ref, w_ref, o_ref):
    o_ref[...] = jnp.dot(x_ref[...].astype(BF16), w_ref[...], preferred_element_type=F32)


def _matmul_call(x, w):
    return pl.pallas_call(_matmul_body, out_shape=jax.ShapeDtypeStruct((x.shape[0], w.shape[1]), F32),
                          name="shift_proj")(x, w)


def _softmax_sink(s, valid, sink):
    s = jnp.where(valid, s, -jnp.inf)
    m = jnp.maximum(jnp.max(s, axis=-1, keepdims=True), sink)
    p = jnp.exp(s - m)
    den = jnp.sum(p, axis=-1, keepdims=True) + jnp.exp(sink - m)
    return p / den


def _attn_prompt_body(sink_ref, q_ref, kc_ref, kp_ref, vc_ref, vp_ref, o_ref):
    n = pl.program_id(1)
    q = q_ref[0]
    kcat = jnp.concatenate([kp_ref[0], kc_ref[0]], axis=0)
    vcat = jnp.concatenate([vp_ref[0], vc_ref[0]], axis=0)
    krot = pltpu.roll(kcat, shift=HEAD_DIM, axis=1)
    vrot = pltpu.roll(vcat, shift=HEAD_DIM, axis=1)
    lane = lax.broadcasted_iota(jnp.int32, (1, LANES), 1)
    lo = lane < HEAD_DIM
    row = lax.broadcasted_iota(jnp.int32, (WINDOW, WINDOW), 0)
    col = lax.broadcasted_iota(jnp.int32, (WINDOW, WINDOW), 1)
    upper = col > row
    no_prev = upper & (n == 0)
    for hk in range(N_KV_HEADS):
        src_lo, src_hi = (kcat, krot) if hk == 0 else (krot, kcat)
        k_pads = (jnp.where(lo, src_lo, 0.0).astype(BF16), jnp.where(lo, 0.0, src_hi).astype(BF16))
        vsrc_lo, vsrc_hi = (vcat, vrot) if hk == 0 else (vrot, vcat)
        ones = jnp.ones((2 * WINDOW, LANES), BF16)
        v_pads = tuple(jnp.concatenate([vp.astype(BF16), ones], axis=1)
                       for vp in (jnp.where(lo, vsrc_lo, 0.0), jnp.where(lo, 0.0, vsrc_hi)))
        for pr in range(2):
            pair = hk * 2 + pr
            lanes = slice(pair * LANES, (pair + 1) * LANES)
            qp = (q[:, lanes] * SCALE).astype(BF16)
            out = None
            for e in range(2):
                s = lax.dot_general(qp, k_pads[e], (((1,), (1,)), ((), ())),
                                    preferred_element_type=F32)
                s = jnp.where(upper, s[:, 0:WINDOW], s[:, WINDOW:2 * WINDOW])
                s = jnp.where(no_prev, -jnp.inf, s)
                sink = sink_ref[pair * 2 + e]
                m = jnp.maximum(jnp.max(s, axis=-1, keepdims=True), sink)
                p = jnp.exp(s - m)
                p2 = jnp.concatenate([jnp.where(upper, p, 0.0), jnp.where(upper, 0.0, p)], axis=1)
                pv = jnp.dot(p2.astype(BF16), v_pads[e], preferred_element_type=F32)
                scaled = pv[:, 0:LANES] / (pv[:, LANES:2 * LANES] + jnp.exp(sink - m))
                out = scaled if out is None else out + scaled
            o_ref[0, :, lanes] = out


def _attn_prompt_call(sinks, q, k, v):
    b, t, _ = q.shape
    nb = t // WINDOW
    cur = lambda w: pl.BlockSpec((1, WINDOW, w), lambda i, j: (i, j, 0))
    prev = lambda w: pl.BlockSpec((1, WINDOW, w), lambda i, j: (i, jnp.maximum(j - 1, 0), 0))
    return pl.pallas_call(
        _attn_prompt_body,
        grid=(b, nb),
        in_specs=[pl.BlockSpec(memory_space=pltpu.SMEM), cur(ATTN_WIDTH), cur(KV_WIDTH), prev(KV_WIDTH),
                  cur(KV_WIDTH), prev(KV_WIDTH)],
        out_specs=cur(ATTN_WIDTH),
        out_shape=jax.ShapeDtypeStruct((b, t, ATTN_WIDTH), F32),
        compiler_params=pltpu.CompilerParams(dimension_semantics=("arbitrary", "arbitrary")),
        name="attn_prompt",
    )(sinks, q, k, k, v, v)


def _attn_sample_body(sink_ref, q_ref, k_ref, v_ref, o_ref, *, bb, rows, keys, dec_seq):
    r = lax.broadcasted_iota(jnp.int32, (rows, keys), 0)
    c = lax.broadcasted_iota(jnp.int32, (rows, keys), 1)
    t = r % dec_seq
    valid = (c > t) & (c <= t + WINDOW)
    sink = sink_ref[...]

    def one(i, _):
        q = q_ref[i].astype(BF16)
        k = k_ref[i].astype(BF16)
        s = lax.dot_general(q, k, (((1,), (1,)), ((), ())), preferred_element_type=F32) * SCALE
        probs = _softmax_sink(s, valid, sink)
        o_ref[i] = jnp.dot(probs.astype(BF16), v_ref[i].astype(BF16), preferred_element_type=F32)
        return 0

    lax.fori_loop(0, bb, one, 0, unroll=8)


def _attn_sample_call(sink_rows, q, kall, vall, *, dec_seq, bb=16):
    bsz, rows, _ = q.shape
    keys = kall.shape[1]
    blk = lambda r: pl.BlockSpec((bb, r, KV_WIDTH), lambda i: (i, 0, 0))
    body = functools.partial(_attn_sample_body, bb=bb, rows=rows, keys=keys, dec_seq=dec_seq)
    return pl.pallas_call(
        body,
        grid=(bsz // bb,),
        in_specs=[pl.BlockSpec((rows, 1), lambda i: (0, 0)), blk(rows), blk(keys), blk(keys)],
        out_specs=blk(rows),
        out_shape=jax.ShapeDtypeStruct((bsz, rows, KV_WIDTH), F32),
        compiler_params=pltpu.CompilerParams(dimension_semantics=("arbitrary",)),
        name="attn_sample",
    )(sink_rows, q, kall, vall)


IB_GROUP = 4


BATCH_GROUP = LANES // N_RWKV_HEADS
HEAD_PAIRS = N_RWKV_HEADS // 2


def _to_chains(x):
    m = jnp.concatenate([x[:, hp * LANES:(hp + 1) * LANES] for hp in range(HEAD_PAIRS)], axis=0)
    mt = m.T
    return jnp.concatenate([mt[0:HEAD_DIM, :], mt[HEAD_DIM:2 * HEAD_DIM, :]], axis=1)


def _from_chains(y):
    m = jnp.concatenate([y[:, 0:HEAD_DIM], y[:, HEAD_DIM:2 * HEAD_DIM]], axis=0)
    mt = m.T
    return jnp.concatenate([mt[hp * BATCH_GROUP:(hp + 1) * BATCH_GROUP, :] for hp in range(HEAD_PAIRS)], axis=1)


def _zero_like_dep(x):
    u = lax.bitcast_convert_type(x, jnp.uint32)
    z = lax.shift_right_logical(lax.shift_right_logical(u, jnp.uint32(16)), jnp.uint32(16))
    return lax.bitcast_convert_type(z, F32)


def _after(x, zero):
    return x + zero


def _scan_body(*refs, tb, n_tb, zero_init):
    n_in = 6 if zero_init else 7
    raw = refs[:6]
    s0_ref = None if zero_init else refs[6]
    y_out, st_ref = refs[n_in:n_in + 2]
    scratch = refs[n_in + 2:]
    slots = (scratch[0:6], scratch[6:12])
    state, sa, g_run = scratch[12:]
    s = pl.program_id(1)
    n_ib = HEAD_DIM // SUBLANES

    @pl.when(s == 0)
    def _():
        for buf in slots[1][:5]:
            buf[...] = jnp.zeros(buf.shape, F32)
        slots[0][5][...] = jnp.ones((HEAD_DIM, LANES), F32)
        slots[1][5][...] = jnp.ones((HEAD_DIM, LANES), F32)
        state[...] = jnp.zeros(state.shape, F32)

    if not zero_init:
        @pl.when(s == 1)
        def _():
            def load_i(i, _):
                state[:, i, :] = _to_chains(s0_ref[:, i, :])
                return 0

            lax.fori_loop(0, HEAD_DIM, load_i, 0, unroll=8)

    def run_block(rd, wr):
        r_ref, k_ref, v_ref, a_ref, b_ref, _ = rd
        wr_r, wr_k, wr_v, wr_a, wr_b, wr_g = wr
        assert n_ib // IB_GROUP == 2

        for ib in range(n_ib):
            rows = pl.ds(ib * SUBLANES, SUBLANES)

            def init_j(j, acc, rows=rows):
                jr = pl.ds(j, 1)
                s_full = state[j, rows, :] * wr_g[jr, :]
                state[j, rows, :] = s_full
                return acc + s_full * a_ref[0, jr, :]

            sa[rows, :] = lax.fori_loop(0, HEAD_DIM, init_j, jnp.zeros((SUBLANES, LANES), F32), unroll=8)
        g_run[...] = jnp.ones((HEAD_DIM, LANES), F32)

        def step(t, y_prev):
            y_tok = _from_chains(y_prev)
            y_out[:, jnp.maximum(t - 1, 0), :] = y_tok
            gates = {(0, 8): _zero_like_dep(y_tok[0:1, 0:LANES])}
            c_r, c_d, c_k, c_v, c_a, c_b = (_to_chains(src[:, t, :]) for src in raw)
            g_prev = g_run[...]
            g = g_prev * c_d
            g_inv = 1.0 / g
            g_run[...] = g
            staged = ((wr_r, c_r * g), (wr_k, c_k * g_inv), (wr_v, c_v), (wr_a, c_a * g_prev), (wr_b, c_b * g_inv))
            for n, (dst, val) in enumerate(staged):
                dst[t] = val
                gates[(n // 3, 24 + 16 * (n % 3))] = _zero_like_dep(val[0:1, :])
            y_new = []
            tn = jnp.minimum(t + 1, tb - 1)
            for grp in range(n_ib // IB_GROUP):
                rows = [pl.ds((grp * IB_GROUP + u) * SUBLANES, SUBLANES) for u in range(IB_GROUP)]
                sav = [sa[rw, :] for rw in rows]
                vv = [v_ref[t, rw, :] for rw in rows]
                yacc = [jnp.zeros((SUBLANES, LANES), F32) for _ in range(IB_GROUP)]
                sacc = [jnp.zeros((SUBLANES, LANES), F32) for _ in range(IB_GROUP)]
                for j in range(HEAD_DIM):
                    jr = pl.ds(j, 1)
                    bj = b_ref[t, jr, :]
                    if (grp, j) in gates:
                        bj = _after(bj, gates[(grp, j)])
                    kj = k_ref[t, jr, :]
                    rj = r_ref[t, jr, :]
                    aj = a_ref[tn, jr, :]
                    for u in range(IB_GROUP):
                        s_new = state[j, rows[u], :] + sav[u] * bj + vv[u] * kj
                        state[j, rows[u], :] = s_new
                        yacc[u] = yacc[u] + s_new * rj
                        sacc[u] = sacc[u] + s_new * aj
                for u in range(IB_GROUP):
                    sa[rows[u], :] = sacc[u]
                y_new.extend(yacc)
            return jnp.concatenate(y_new, axis=0)

        y_last = lax.fori_loop(0, tb, step, jnp.zeros((HEAD_DIM, LANES), F32))
        y_out[:, tb - 1, :] = _from_chains(y_last)
        wr_g[...] = g_run[...]

    @pl.when(s % 2 == 0)
    def _():
        run_block(slots[1], slots[0])

    @pl.when(s % 2 == 1)
    def _():
        run_block(slots[0], slots[1])

    @pl.when(s == n_tb)
    def _():
        g_last = slots[(n_tb - 1) % 2][5][...]

        def store_i(i, _):
            st_ref[:, i, :] = _from_chains(state[:, i, :] * g_last)
            return 0

        lax.fori_loop(0, HEAD_DIM, store_i, 0, unroll=8)


def _scan_call(r, d, k, v, a, b, s0, *, tb):
    bsz, t_len = r.shape[0], r.shape[1]
    bg = BATCH_GROUP
    n_tb = t_len // tb
    op_in = pl.BlockSpec((bg, tb, RWKV_WIDTH), lambda g, s: (g, jnp.minimum(s, n_tb - 1), 0))
    op_out = pl.BlockSpec((bg, tb, RWKV_WIDTH), lambda g, s: (g, jnp.maximum(s - 1, 0), 0))
    st = pl.BlockSpec((bg, HEAD_DIM, RWKV_WIDTH), lambda g, s: (g, 0, 0))
    zero_init = s0 is None
    body = functools.partial(_scan_body, tb=tb, n_tb=n_tb, zero_init=zero_init)
    return pl.pallas_call(
        body,
        grid=(bsz // bg, n_tb + 1),
        in_specs=[op_in] * 6 + ([] if zero_init else [st]),
        out_specs=[op_out, st],
        out_shape=[jax.ShapeDtypeStruct(r.shape, F32),
                   jax.ShapeDtypeStruct((bsz, HEAD_DIM, RWKV_WIDTH), F32)],
        scratch_shapes=([pltpu.VMEM((tb, HEAD_DIM, LANES), F32)] * 5 + [pltpu.VMEM((HEAD_DIM, LANES), F32)]) * 2
        + [pltpu.VMEM((HEAD_DIM, HEAD_DIM, LANES), F32), pltpu.VMEM((HEAD_DIM, LANES), F32),
           pltpu.VMEM((HEAD_DIM, LANES), F32)],
        compiler_params=pltpu.CompilerParams(dimension_semantics=("arbitrary", "arbitrary"),
                                             vmem_limit_bytes=VMEM_LIMIT),
        name="wkv_scan",
    )(r, d, k, v, a, b, *([] if zero_init else [s0]))


def _post_body(x_ref, at_ref, y_ref, g_ref, bon_ref, j512_ref, lng_ref, lnb_ref, wout_ref, g2_ref, wup_ref,
               wdn_ref, o_ref, *, ff_chunk):
    j512 = j512_ref[...]
    y = y_ref[...]
    mu = _seg_sum(y, j512) * (1.0 / HEAD_DIM)
    yc = y - mu
    var = _seg_sum(yc * yc, j512) * (1.0 / HEAD_DIM)
    yn = yc * lax.rsqrt(var + GN_EPS) * lng_ref[...] + lnb_ref[...]
    rw = (yn + bon_ref[...]) * g_ref[...]
    x1 = (x_ref[...]
          + jnp.dot(at_ref[...].astype(BF16), wout_ref[0:ATTN_WIDTH, :], preferred_element_type=F32)
          + jnp.dot(rw.astype(BF16), wout_ref[ATTN_WIDTH:D_MODEL, :], preferred_element_type=F32))
    ms = jnp.mean(x1 * x1, axis=-1, keepdims=True)
    h2 = (x1 * lax.rsqrt(ms + RMS_EPS) * g2_ref[...]).astype(BF16)
    o_ref[...] = x1
    for c in range(D_FF // ff_chunk):
        u = jnp.dot(h2, wup_ref[:, c * ff_chunk:(c + 1) * ff_chunk], preferred_element_type=F32)
        u = jnp.maximum(u, 0.0)
        o_ref[...] += jnp.dot((u * u).astype(BF16), wdn_ref[c * ff_chunk:(c + 1) * ff_chunk, :],
                              preferred_element_type=F32)


def _post_call(x2, attn, y, g, bon, consts, *, tm, ff_chunk=1024):
    n = x2.shape[0]
    tm = min(tm, n)
    tok = lambda w: pl.BlockSpec((tm, w), lambda i: (i, 0))
    full = lambda a: pl.BlockSpec(a.shape, lambda i: (0,) * a.ndim, pipeline_mode=pl.Buffered(1))
    return pl.pallas_call(
        functools.partial(_post_body, ff_chunk=ff_chunk),
        grid=(n // tm,),
        in_specs=[tok(D_MODEL), tok(ATTN_WIDTH), tok(RWKV_WIDTH), tok(RWKV_WIDTH), tok(RWKV_WIDTH)]
        + [full(c) for c in consts],
        out_specs=tok(D_MODEL),
        out_shape=jax.ShapeDtypeStruct((n, D_MODEL), F32),
        compiler_params=pltpu.CompilerParams(dimension_semantics=("arbitrary",), vmem_limit_bytes=VMEM_LIMIT),
        name="post",
    )(x2, attn, y, g, bon, *consts)


def _layer(x, p_prev_rows, s0, attn_fn, w, *, tm_pre, tm_post, scan_tb):
    b, t, _ = x.shape
    x2 = x.reshape(b * t, D_MODEL)
    outs = _pre_call(x2, p_prev_rows, w["pre"], tm=tm_pre, seq_len=t)
    q, k, v, r, d, kh, vh, av, bv, g, bon, h_last = outs
    attn, new_k, new_v = attn_fn(q, k, v)
    ops = [a.reshape(b, t, RWKV_WIDTH) for a in (r, d, kh, vh, av, bv)]
    s0r = None if s0 is None else s0.astype(F32).transpose(0, 2, 1, 3).reshape(b, HEAD_DIM, RWKV_WIDTH)
    y3, s_fin = _scan_call(*ops, s0r, tb=scan_tb)
    y = y3.reshape(b * t, RWKV_WIDTH)
    s_new = s_fin.reshape(b, HEAD_DIM, N_RWKV_HEADS, HEAD_DIM).transpose(0, 2, 1, 3)
    out = _post_call(x2, attn, y, g, bon, w["post"], tm=tm_post)
    return out.reshape(b, t, D_MODEL), new_k, new_v, s_new, h_last


def _prep_weights(norm1_g, w_in, q_norm_g, k_norm_g, rwkv_mu, w_decay_0, w_decay_up, a_0, a_up, g_up, k_k, k_a,
                  r_k, ln_x_g, ln_x_b, w_out, norm2_g, w_ff_up, w_ff_down):
    o = RWKV_OFF
    pieces = ((0, o + 512), (o + 576, o + 1088), (o + 1088, o + 1600), (o + 512, o + 576), (o + 1600, o + 1664),
              (o + 1664, o + 1792))
    w_in_p = jnp.concatenate([w_in[:, lo:hi] for lo, hi in pieces], axis=1).astype(BF16)
    mu_p = jnp.concatenate([rwkv_mu[max(lo - o, 0):hi - o] for lo, hi in pieces]).reshape(1, RWKV_PROJ)
    row = lambda a: a.reshape(1, -1).astype(F32)
    eye = lambda n: jnp.kron(jnp.eye(n, dtype=F32), jnp.ones((HEAD_DIM, HEAD_DIM), F32)).astype(BF16)
    j512, j128 = eye(N_RWKV_HEADS), eye(N_KV_HEADS)
    zeros = jnp.zeros((DECAY_LORA, RWKV_WIDTH), F32)
    wcomb = jnp.concatenate([jnp.concatenate([w_decay_up, zeros], axis=1),
                             jnp.concatenate([zeros, a_up], axis=1)], axis=0).astype(BF16)
    pre = [row(norm1_g), w_in_p, mu_p, row(jnp.tile(q_norm_g, N_Q_HEADS)), row(jnp.tile(k_norm_g, N_KV_HEADS)),
           j512, j128, wcomb, row(w_decay_0), row(a_0), g_up.astype(BF16), row(k_k), row(k_a), row(r_k)]
    post = [j512, row(ln_x_g), row(ln_x_b), w_out.astype(BF16), row(norm2_g), w_ff_up.astype(BF16),
            w_ff_down.astype(BF16)]
    return {"pre": pre, "post": post, "w_rwkv": w_in_p[:, RWKV_OFF:]}


def _forward(x_prompt, x_sample, cache_k, cache_v, state_wkv, state_shift, norm1_g, w_in, q_norm_g, k_norm_g,
             attn_sinks, rwkv_mu, w_decay_0, w_decay_up, a_0, a_up, g_up, k_k, k_a, r_k, ln_x_g, ln_x_b, w_out,
             norm2_g, w_ff_up, w_ff_down, *, tm_pre=512, tm_post=512, scan_tb=32):
    depth = norm1_g.shape[0]
    assert depth == 1
    l = 0
    w = _prep_weights(norm1_g[l], w_in[l], q_norm_g[l], k_norm_g[l], rwkv_mu[l], w_decay_0[l], w_decay_up[l],
                      a_0[l], a_up[l], g_up[l], k_k[l], k_a[l], r_k[l], ln_x_g[l], ln_x_b[l], w_out[l], norm2_g[l],
                      w_ff_up[l], w_ff_down[l])
    sinks = attn_sinks[l].astype(F32)
    bp, tp, _ = x_prompt.shape
    bs, ts, _ = x_sample.shape

    def attn_prompt(q, k, v):
        q3, k3, v3 = (a.reshape(bp, tp, -1) for a in (q, k, v))
        out = _attn_prompt_call(sinks, q3, k3, v3).reshape(bp * tp, ATTN_WIDTH)
        tail = lambda a: a[:, tp - WINDOW:].reshape(bp, WINDOW, N_KV_HEADS, HEAD_DIM)
        return out, tail(k3), tail(v3)

    pprev_p = jnp.zeros((bp, 1, RWKV_PROJ), F32)
    yp, k1, v1, w1, s1 = _layer(x_prompt, pprev_p, None, attn_prompt, w, tm_pre=tm_pre, tm_post=tm_post,
                                scan_tb=scan_tb)
    s1 = s1.reshape(bp, D_MODEL)

    ck, cv = cache_k[l].astype(F32), cache_v[l].astype(F32)
    group = N_Q_HEADS // N_KV_HEADS
    rows = group * ts
    pad = (-ts) % SUBLANES
    keys = WINDOW + ts + pad

    def attn_sample(q, k, v):
        k4 = k.reshape(bs, ts, N_KV_HEADS, HEAD_DIM)
        v4 = v.reshape(bs, ts, N_KV_HEADS, HEAD_DIM)
        kall = jnp.concatenate([ck, k4], axis=1)
        vall = jnp.concatenate([cv, v4], axis=1)
        dense = lambda a: jnp.pad(a, ((0, 0), (0, pad), (0, 0), (0, 0))).reshape(bs, keys, KV_WIDTH)
        q4 = q.reshape(bs, ts, N_KV_HEADS, group, HEAD_DIM).transpose(0, 2, 3, 1, 4).reshape(
            bs, N_KV_HEADS, rows, HEAD_DIM)
        eye = jnp.eye(N_KV_HEADS, dtype=F32)
        qbd = (q4[:, :, :, None, :] * eye[None, :, None, :, None]).reshape(bs, N_KV_HEADS * rows, KV_WIDTH)
        sink_rows = jnp.repeat(sinks, ts)[:, None]
        o = _attn_sample_call(sink_rows, qbd, dense(kall), dense(vall), dec_seq=ts)
        o = o.reshape(bs, N_KV_HEADS, rows, N_KV_HEADS, HEAD_DIM)
        o = jnp.stack([o[:, hk, :, hk, :] for hk in range(N_KV_HEADS)], axis=1)
        o = o.reshape(bs, N_KV_HEADS, group, ts, HEAD_DIM).transpose(0, 3, 1, 2, 4)
        return o.reshape(bs * ts, ATTN_WIDTH), kall[:, ts:], vall[:, ts:]

    p_prev = _matmul_call(state_shift[l], w["w_rwkv"])
    pprev_s = jnp.repeat(p_prev, ts, axis=0)
    ys, k2, v2, w2, h_all = _layer(x_sample, pprev_s, state_wkv[l], attn_sample, w, tm_pre=tm_pre,
                                   tm_post=tm_post, scan_tb=ts)
    s2 = h_all.reshape(bs, ts, D_MODEL)[:, -1]
    st = lambda a: a[None]
    return (yp, ys, st(k1), st(v1), st(w1), st(s1), st(k2), st(v2), st(w2), st(s2))


def kernel(x_prompt, x_sample, cache_k, cache_v, state_wkv, state_shift, norm1_g, w_in, q_norm_g, k_norm_g, attn_sinks, rwkv_mu, w_decay_0, w_decay_up, a_0, a_up, g_up, k_k, k_a, r_k, ln_x_g, ln_x_b, w_out, norm2_g, w_ff_up, w_ff_down):
    return _forward(x_prompt, x_sample, cache_k, cache_v, state_wkv, state_shift, norm1_g, w_in, q_norm_g, k_norm_g,
                    attn_sinks, rwkv_mu, w_decay_0, w_decay_up, a_0, a_up, g_up, k_k, k_a, r_k, ln_x_g, ln_x_b,
                    w_out, norm2_g, w_ff_up, w_ff_down)
```

```python
import functools

import jax
import jax.numpy as jnp
from jax import lax
from jax.experimental import pallas as pl
from jax.experimental.pallas import tpu as pltpu

D_MODEL = 1024
HEAD_DIM = 64
ATTN_WIDTH = 512
N_Q_HEADS = 8
N_KV_HEADS = 2
KV_WIDTH = 128
RWKV_WIDTH = 512
N_RWKV_HEADS = 8
WINDOW = 128
SCALE = HEAD_DIM ** -0.5
DECAY_LORA = 64
AAA_LORA = 64
GATE_LORA = 128
D_FF = 4 * D_MODEL
RMS_EPS = 1e-6
GN_EPS = 64e-5
RWKV_OFF = ATTN_WIDTH + 2 * KV_WIDTH
RWKV_PROJ = 3 * RWKV_WIDTH + DECAY_LORA + AAA_LORA + GATE_LORA
IN_WIDTH = RWKV_OFF + RWKV_PROJ
LANES = 128
SUBLANES = 8
VMEM_LIMIT = 56 * 1024 * 1024

F32 = jnp.float32
BF16 = jnp.bfloat16


def _seg_sum(x, ones_blk):
    return jnp.dot(x.astype(BF16), ones_blk, preferred_element_type=F32)


def _time_tiled(seq_len):
    return seq_len % SUBLANES == 0


def _pre_body(x_ref, pprev_ref, g1_ref, win_ref, mu_ref, qg_ref, kg_ref, j512_ref, j128_ref,
              wcomb_ref, w0_ref, a0_ref, gup_ref, kk_ref, ka_ref, rk_ref,
              q_o, k_o, v_o, r_o, d_o, kh_o, vh_o, a_o, b_o, g_o, bon_o, h_o,
              carry, *, tm, tiles_per_seq, seq_len):
    x = x_ref[...]
    ms = jnp.mean(x * x, axis=-1, keepdims=True)
    h = x * lax.rsqrt(ms + RMS_EPS) * g1_ref[...]
    proj = jnp.dot(h.astype(BF16), win_ref[...], preferred_element_type=F32)

    j512 = j512_ref[...]
    q = proj[:, 0:ATTN_WIDTH]
    qs = _seg_sum(q * q, j512) * (1.0 / HEAD_DIM)
    q_o[...] = q * lax.rsqrt(qs + RMS_EPS) * qg_ref[...]
    k = proj[:, ATTN_WIDTH:ATTN_WIDTH + KV_WIDTH]
    ks = _seg_sum(k * k, j128_ref[...]) * (1.0 / HEAD_DIM)
    k_o[...] = k * lax.rsqrt(ks + RMS_EPS) * kg_ref[...]
    v_o[...] = proj[:, ATTN_WIDTH + KV_WIDTH:RWKV_OFF]

    p = proj[:, RWKV_OFF:]
    rolled = pltpu.roll(p, shift=1, axis=0)
    row = lax.broadcasted_iota(jnp.int32, (tm, 1), 0)
    if tiles_per_seq is not None:
        t = pl.program_id(0) % tiles_per_seq
        first = jnp.where(t == 0, pprev_ref[0], carry[...])
        p_shift = jnp.where(row == 0, first, rolled)
        carry[...] = p[tm - 1:tm, :]

        @pl.when(t == tiles_per_seq - 1)
        def _():
            h_o[0] = h[tm - 1:tm, :]
    else:
        p_shift = jnp.where(row % seq_len == 0, pprev_ref[...], rolled)
        h_o[...] = h
    pm = p + (p_shift - p) * mu_ref[...]

    xr = pm[:, 0:512]
    xk = pm[:, 512:1024]
    xv = pm[:, 1024:1536]
    xwa = pm[:, 1536:1664]
    xg = pm[:, 1664:1792]
    lane = lax.broadcasted_iota(jnp.int32, (1, LANES), 1)
    wa_in = jnp.where(lane < DECAY_LORA, jnp.tanh(xwa), xwa)
    pre = jnp.dot(wa_in.astype(BF16), wcomb_ref[...], preferred_element_type=F32)
    w_pre = pre[:, 0:512] + w0_ref[...]
    a_pre = pre[:, 512:1024] + a0_ref[...]
    z = -w_pre
    softplus = jnp.maximum(z, 0.0) + jnp.log(1.0 + jnp.exp(-jnp.abs(z)))
    w_log = -softplus - 0.5
    d_o[...] = jnp.exp(-jnp.exp(w_log)).reshape(d_o.shape)
    a_gate = 1.0 / (1.0 + jnp.exp(-a_pre))
    sg = 1.0 / (1.0 + jnp.exp(-xg))
    g_o[...] = jnp.dot(sg.astype(BF16), gup_ref[...], preferred_element_type=F32)
    kkv = xk * kk_ref[...]
    n2 = _seg_sum(kkv * kkv, j512)
    kkn = kkv / jnp.maximum(jnp.sqrt(n2), 1e-12)
    k_h = xk * (1.0 + (a_gate - 1.0) * ka_ref[...])
    r_o[...] = xr.reshape(r_o.shape)
    kh_o[...] = k_h.reshape(kh_o.shape)
    vh_o[...] = xv.reshape(vh_o.shape)
    a_o[...] = (-kkn).reshape(a_o.shape)
    b_o[...] = (kkn * a_gate).reshape(b_o.shape)
    bon_o[...] = _seg_sum(xr * k_h * rk_ref[...], j512) * xv


def _pre_call(x2, pprev, consts, *, tm, seq_len):
    n = x2.shape[0]
    tm = min(tm, n)
    n_tiles = n // tm
    per_seq = seq_len >= tm
    tiles_per_seq = seq_len // tm if per_seq else None
    n_seq = n // seq_len
    full = lambda a: pl.BlockSpec(a.shape, lambda i: (0,) * a.ndim)
    tok = lambda w: pl.BlockSpec((tm, w), lambda i: (i, 0))
    if per_seq:
        pprev_spec = pl.BlockSpec((1, 1, RWKV_PROJ), lambda i: (i // tiles_per_seq, 0, 0))
        h_spec = pl.BlockSpec((1, 1, D_MODEL), lambda i: (i // tiles_per_seq, 0, 0))
        h_shape = jax.ShapeDtypeStruct((n_seq, 1, D_MODEL), F32)
    else:
        pprev_spec = tok(RWKV_PROJ)
        h_spec = tok(D_MODEL)
        h_shape = jax.ShapeDtypeStruct((n, D_MODEL), F32)
    widths = [ATTN_WIDTH, KV_WIDTH, KV_WIDTH] + [RWKV_WIDTH] * 8
    out_shape = [jax.ShapeDtypeStruct((n, w), F32) for w in widths] + [h_shape]
    out_specs = [tok(w) for w in widths] + [h_spec]
    if per_seq and _time_tiled(seq_len):
        for idx in range(3, 9):
            out_shape[idx] = jax.ShapeDtypeStruct((seq_len // SUBLANES, n_seq, SUBLANES, RWKV_WIDTH), F32)
            out_specs[idx] = pl.BlockSpec((tm // SUBLANES, 1, SUBLANES, RWKV_WIDTH),
                                          lambda i: (i % tiles_per_seq, i // tiles_per_seq, 0, 0))
    body = functools.partial(_pre_body, tm=tm, tiles_per_seq=tiles_per_seq, seq_len=seq_len)
    return pl.pallas_call(
        body,
        grid=(n_tiles,),
        in_specs=[tok(D_MODEL), pprev_spec] + [full(c) for c in consts],
        out_specs=out_specs,
        out_shape=out_shape,
        scratch_shapes=[pltpu.VMEM((1, RWKV_PROJ), F32)],
        compiler_params=pltpu.CompilerParams(dimension_semantics=("arbitrary",), vmem_limit_bytes=VMEM_LIMIT),
        name="pre",
    )(x2, pprev, *consts)


def _matmul_body(x_ref, w_ref, o_ref):
    o_ref[...] = jnp.dot(x_ref[...].astype(BF16), w_ref[...], preferred_element_type=F32)


def _matmul_call(x, w):
    return pl.pallas_call(_matmul_body, out_shape=jax.ShapeDtypeStruct((x.shape[0], w.shape[1]), F32),
                          name="shift_proj")(x, w)


def _softmax_sink(s, valid, sink):
    s = jnp.where(valid, s, -jnp.inf)
    m = jnp.maximum(jnp.max(s, axis=-1, keepdims=True), sink)
    p = jnp.exp(s - m)
    den = jnp.sum(p, axis=-1, keepdims=True) + jnp.exp(sink - m)
    return p / den


def _attn_prompt_body(sink_ref, q_ref, kc_ref, kp_ref, vc_ref, vp_ref, o_ref, *, nq):
    for sub in range(nq):
        rows = slice(sub * WINDOW, (sub + 1) * WINDOW)
        prev = slice((sub - 1) * WINDOW, sub * WINDOW)
        k_prev, v_prev = (kp_ref[0], vp_ref[0]) if sub == 0 else (kc_ref[0, prev], vc_ref[0, prev])
        first = (pl.program_id(1) == 0) if sub == 0 else None
        _attn_block(sink_ref, q_ref[0, rows], k_prev, kc_ref[0, rows], v_prev, vc_ref[0, rows], first,
                    o_ref.at[0, rows])


def _attn_block(sink_ref, q, k_prev, k_cur, v_prev, v_cur, first, o_ref):
    kcat = jnp.concatenate([k_prev, k_cur], axis=0)
    vcat = jnp.concatenate([v_prev, v_cur], axis=0)
    krot = pltpu.roll(kcat, shift=HEAD_DIM, axis=1)
    vrot = pltpu.roll(vcat, shift=HEAD_DIM, axis=1)
    lane = lax.broadcasted_iota(jnp.int32, (1, LANES), 1)
    lo = lane < HEAD_DIM
    row = lax.broadcasted_iota(jnp.int32, (WINDOW, WINDOW), 0)
    col = lax.broadcasted_iota(jnp.int32, (WINDOW, WINDOW), 1)
    upper = col > row
    no_prev = None if first is None else upper & first
    for hk in range(N_KV_HEADS):
        src_lo, src_hi = (kcat, krot) if hk == 0 else (krot, kcat)
        k_pads = (jnp.where(lo, src_lo, 0.0).astype(BF16), jnp.where(lo, 0.0, src_hi).astype(BF16))
        vsrc_lo, vsrc_hi = (vcat, vrot) if hk == 0 else (vrot, vcat)
        ones = jnp.ones((2 * WINDOW, LANES), BF16)
        v_pads = tuple(jnp.concatenate([vp.astype(BF16), ones], axis=1)
                       for vp in (jnp.where(lo, vsrc_lo, 0.0), jnp.where(lo, 0.0, vsrc_hi)))
        for pr in range(2):
            pair = hk * 2 + pr
            lanes = slice(pair * LANES, (pair + 1) * LANES)
            qp = (q[:, lanes] * SCALE).astype(BF16)
            out = None
            for e in range(2):
                s = lax.dot_general(qp, k_pads[e], (((1,), (1,)), ((), ())),
                                    preferred_element_type=F32)
                s = jnp.where(upper, s[:, 0:WINDOW], s[:, WINDOW:2 * WINDOW])
                if no_prev is not None:
                    s = jnp.where(no_prev, -jnp.inf, s)
                sink = sink_ref[pair * 2 + e]
                m = jnp.maximum(jnp.max(s, axis=-1, keepdims=True), sink)
                p = jnp.exp(s - m)
                p2 = jnp.concatenate([jnp.where(upper, p, 0.0), jnp.where(upper, 0.0, p)], axis=1)
                pv = jnp.dot(p2.astype(BF16), v_pads[e], preferred_element_type=F32)
                scaled = pv[:, 0:LANES] / (pv[:, LANES:2 * LANES] + jnp.exp(sink - m))
                out = scaled if out is None else out + scaled
            o_ref[:, lanes] = out


def _attn_prompt_call(sinks, q, k, v, *, nq=4):
    b, t, _ = q.shape
    nb = t // WINDOW
    nq = min(nq, nb)
    assert nb % nq == 0
    cur = lambda w: pl.BlockSpec((1, nq * WINDOW, w), lambda i, j: (i, j, 0))
    prev = lambda w: pl.BlockSpec((1, WINDOW, w), lambda i, j: (i, jnp.maximum(nq * j - 1, 0), 0))
    return pl.pallas_call(
        functools.partial(_attn_prompt_body, nq=nq),
        grid=(b, nb // nq),
        in_specs=[pl.BlockSpec(memory_space=pltpu.SMEM), cur(ATTN_WIDTH), cur(KV_WIDTH), prev(KV_WIDTH),
                  cur(KV_WIDTH), prev(KV_WIDTH)],
        out_specs=cur(ATTN_WIDTH),
        out_shape=jax.ShapeDtypeStruct((b, t, ATTN_WIDTH), F32),
        compiler_params=pltpu.CompilerParams(dimension_semantics=("arbitrary", "arbitrary")),
        name="attn_prompt",
    )(sinks, q, k, k, v, v)


def _attn_sample_body(sink_ref, q_ref, k_ref, v_ref, o_ref, *, bb, rows, keys, dec_seq):
    r = lax.broadcasted_iota(jnp.int32, (rows, keys), 0)
    c = lax.broadcasted_iota(jnp.int32, (rows, keys), 1)
    t = r % dec_seq
    valid = (c > t) & (c <= t + WINDOW)
    sink = sink_ref[...]

    def one(i, _):
        q = q_ref[i].astype(BF16)
        k = k_ref[i].astype(BF16)
        s = lax.dot_general(q, k, (((1,), (1,)), ((), ())), preferred_element_type=F32) * SCALE
        probs = _softmax_sink(s, valid, sink)
        o_ref[i] = jnp.dot(probs.astype(BF16), v_ref[i].astype(BF16), preferred_element_type=F32)
        return 0

    lax.fori_loop(0, bb, one, 0, unroll=8)


def _attn_sample_call(sink_rows, q, kall, vall, *, dec_seq, bb=16):
    bsz, rows, _ = q.shape
    keys = kall.shape[1]
    blk = lambda r: pl.BlockSpec((bb, r, KV_WIDTH), lambda i: (i, 0, 0))
    body = functools.partial(_attn_sample_body, bb=bb, rows=rows, keys=keys, dec_seq=dec_seq)
    return pl.pallas_call(
        body,
        grid=(bsz // bb,),
        in_specs=[pl.BlockSpec((rows, 1), lambda i: (0, 0)), blk(rows), blk(keys), blk(keys)],
        out_specs=blk(rows),
        out_shape=jax.ShapeDtypeStruct((bsz, rows, KV_WIDTH), F32),
        compiler_params=pltpu.CompilerParams(dimension_semantics=("arbitrary",)),
        name="attn_sample",
    )(sink_rows, q, kall, vall)


IB_GROUP = 4


BATCH_GROUP = LANES // N_RWKV_HEADS
HEAD_PAIRS = N_RWKV_HEADS // 2


def _to_chains(x):
    m = jnp.concatenate([x[:, hp * LANES:(hp + 1) * LANES] for hp in range(HEAD_PAIRS)], axis=0)
    mt = m.T
    return jnp.concatenate([mt[0:HEAD_DIM, :], mt[HEAD_DIM:2 * HEAD_DIM, :]], axis=1)


def _from_chains(y):
    m = jnp.concatenate([y[:, 0:HEAD_DIM], y[:, HEAD_DIM:2 * HEAD_DIM]], axis=0)
    mt = m.T
    return jnp.concatenate([mt[hp * BATCH_GROUP:(hp + 1) * BATCH_GROUP, :] for hp in range(HEAD_PAIRS)], axis=1)


def _zero_like_dep(x):
    u = lax.bitcast_convert_type(x, jnp.uint32)
    z = lax.shift_right_logical(lax.shift_right_logical(u, jnp.uint32(16)), jnp.uint32(16))
    return lax.bitcast_convert_type(z, F32)


def _after(x, zero):
    return x + zero


def _scan_body(*refs, tb, n_tb, zero_init, tiled):
    n_in = 6 if zero_init else 7
    raw = refs[:6]
    s0_ref = None if zero_init else refs[6]
    y_out, st_ref = refs[n_in:n_in + 2]
    scratch = refs[n_in + 2:]
    slots = (scratch[0:6], scratch[6:12])
    state, sa, g_run = scratch[12:]
    s = pl.program_id(1)
    n_ib = HEAD_DIM // SUBLANES

    def rows_at(t):
        if tiled:
            return (t // SUBLANES, slice(None), t % SUBLANES, slice(None))
        return (slice(None), t, slice(None))

    @pl.when(s == 0)
    def _():
        for buf in slots[1][:5]:
            buf[...] = jnp.zeros(buf.shape, F32)
        slots[0][5][...] = jnp.ones((HEAD_DIM, LANES), F32)
        slots[1][5][...] = jnp.ones((HEAD_DIM, LANES), F32)
        state[...] = jnp.zeros(state.shape, F32)

    if not zero_init:
        @pl.when(s == 1)
        def _():
            def load_i(i, _):
                state[:, i, :] = _to_chains(s0_ref[:, i, :])
                return 0

            lax.fori_loop(0, HEAD_DIM, load_i, 0, unroll=8)

    def run_block(rd, wr):
        r_ref, k_ref, v_ref, a_ref, b_ref, _ = rd
        wr_r, wr_k, wr_v, wr_a, wr_b, wr_g = wr
        assert n_ib // IB_GROUP == 2

        for ib in range(n_ib):
            rows = pl.ds(ib * SUBLANES, SUBLANES)

            def init_j(j, acc, rows=rows):
                jr = pl.ds(j, 1)
                s_full = state[j, rows, :] * wr_g[jr, :]
                state[j, rows, :] = s_full
                return acc + s_full * a_ref[0, jr, :]

            sa[rows, :] = lax.fori_loop(0, HEAD_DIM, init_j, jnp.zeros((SUBLANES, LANES), F32), unroll=8)
        g_run[...] = jnp.ones((HEAD_DIM, LANES), F32)

        def step(t, y_prev):
            y_tok = _from_chains(y_prev)
            y_out[rows_at(jnp.maximum(t - 1, 0))] = y_tok
            gates = {(0, 8): _zero_like_dep(y_tok[0:1, 0:LANES])}
            c_r, c_d, c_k, c_v, c_a, c_b = (_to_chains(src[rows_at(t)]) for src in raw)
            g_prev = g_run[...]
            g = g_prev * c_d
            g_inv = 1.0 / g
            g_run[...] = g
            staged = ((wr_r, c_r * g), (wr_k, c_k * g_inv), (wr_v, c_v), (wr_a, c_a * g_prev), (wr_b, c_b * g_inv))
            for n, (dst, val) in enumerate(staged):
                dst[t] = val
                gates[(n // 3, 24 + 16 * (n % 3))] = _zero_like_dep(val[0:1, :])
            y_new = []
            tn = jnp.minimum(t + 1, tb - 1)
            for grp in range(n_ib // IB_GROUP):
                rows = [pl.ds((grp * IB_GROUP + u) * SUBLANES, SUBLANES) for u in range(IB_GROUP)]
                sav = [sa[rw, :] for rw in rows]
                vv = [v_ref[t, rw, :] for rw in rows]
                yacc = [jnp.zeros((SUBLANES, LANES), F32) for _ in range(IB_GROUP)]
                sacc = [jnp.zeros((SUBLANES, LANES), F32) for _ in range(IB_GROUP)]
                for j in range(HEAD_DIM):
                    jr = pl.ds(j, 1)
                    bj = b_ref[t, jr, :]
                    if (grp, j) in gates:
                        bj = _after(bj, gates[(grp, j)])
                    kj = k_ref[t, jr, :]
                    rj = r_ref[t, jr, :]
                    aj = a_ref[tn, jr, :]
                    for u in range(IB_GROUP):
                        s_new = state[j, rows[u], :] + sav[u] * bj + vv[u] * kj
                        state[j, rows[u], :] = s_new
                        yacc[u] = yacc[u] + s_new * rj
                        sacc[u] = sacc[u] + s_new * aj
                for u in range(IB_GROUP):
                    sa[rows[u], :] = sacc[u]
                y_new.extend(yacc)
            return jnp.concatenate(y_new, axis=0)

        y_last = lax.fori_loop(0, tb, step, jnp.zeros((HEAD_DIM, LANES), F32))
        y_out[rows_at(tb - 1)] = _from_chains(y_last)
        wr_g[...] = g_run[...]

    @pl.when(s % 2 == 0)
    def _():
        run_block(slots[1], slots[0])

    @pl.when(s % 2 == 1)
    def _():
        run_block(slots[0], slots[1])

    @pl.when(s == n_tb)
    def _():
        g_last = slots[(n_tb - 1) % 2][5][...]

        def store_i(i, _):
            st_ref[:, i, :] = _from_chains(state[:, i, :] * g_last)
            return 0

        lax.fori_loop(0, HEAD_DIM, store_i, 0, unroll=8)


def _scan_call(r, d, k, v, a, b, s0, *, tb):
    tiled = r.ndim == 4
    bsz, t_len = (r.shape[1], r.shape[0] * SUBLANES) if tiled else (r.shape[0], r.shape[1])
    bg = BATCH_GROUP
    n_tb = t_len // tb
    if tiled:
        blk = (tb // SUBLANES, bg, SUBLANES, RWKV_WIDTH)
        op_in = pl.BlockSpec(blk, lambda g, s: (jnp.minimum(s, n_tb - 1), g, 0, 0))
        op_out = pl.BlockSpec(blk, lambda g, s: (jnp.maximum(s - 1, 0), g, 0, 0))
    else:
        op_in = pl.BlockSpec((bg, tb, RWKV_WIDTH), lambda g, s: (g, jnp.minimum(s, n_tb - 1), 0))
        op_out = pl.BlockSpec((bg, tb, RWKV_WIDTH), lambda g, s: (g, jnp.maximum(s - 1, 0), 0))
    st = pl.BlockSpec((bg, HEAD_DIM, RWKV_WIDTH), lambda g, s: (g, 0, 0))
    zero_init = s0 is None
    body = functools.partial(_scan_body, tb=tb, n_tb=n_tb, zero_init=zero_init, tiled=tiled)
    return pl.pallas_call(
        body,
        grid=(bsz // bg, n_tb + 1),
        in_specs=[op_in] * 6 + ([] if zero_init else [st]),
        out_specs=[op_out, st],
        out_shape=[jax.ShapeDtypeStruct(r.shape, F32),
                   jax.ShapeDtypeStruct((bsz, HEAD_DIM, RWKV_WIDTH), F32)],
        scratch_shapes=([pltpu.VMEM((tb, HEAD_DIM, LANES), F32)] * 5 + [pltpu.VMEM((HEAD_DIM, LANES), F32)]) * 2
        + [pltpu.VMEM((HEAD_DIM, HEAD_DIM, LANES), F32), pltpu.VMEM((HEAD_DIM, LANES), F32),
           pltpu.VMEM((HEAD_DIM, LANES), F32)],
        compiler_params=pltpu.CompilerParams(dimension_semantics=("arbitrary", "arbitrary"),
                                             vmem_limit_bytes=VMEM_LIMIT),
        name="wkv_scan",
    )(r, d, k, v, a, b, *([] if zero_init else [s0]))


def _post_body(x_ref, at_ref, y_ref, g_ref, bon_ref, j512_ref, lng_ref, lnb_ref, wout_ref, g2_ref, wup_ref,
               wdn_ref, o_ref, *, ff_chunk):
    j512 = j512_ref[...]
    y = y_ref[...].reshape(g_ref.shape)
    mu = _seg_sum(y, j512) * (1.0 / HEAD_DIM)
    yc = y - mu
    var = _seg_sum(yc * yc, j512) * (1.0 / HEAD_DIM)
    yn = yc * lax.rsqrt(var + GN_EPS) * lng_ref[...] + lnb_ref[...]
    rw = (yn + bon_ref[...]) * g_ref[...]
    x1 = (x_ref[...]
          + jnp.dot(at_ref[...].astype(BF16), wout_ref[0:ATTN_WIDTH, :], preferred_element_type=F32)
          + jnp.dot(rw.astype(BF16), wout_ref[ATTN_WIDTH:D_MODEL, :], preferred_element_type=F32))
    ms = jnp.mean(x1 * x1, axis=-1, keepdims=True)
    h2 = (x1 * lax.rsqrt(ms + RMS_EPS) * g2_ref[...]).astype(BF16)
    o_ref[...] = x1
    for c in range(D_FF // ff_chunk):
        u = jnp.dot(h2, wup_ref[:, c * ff_chunk:(c + 1) * ff_chunk], preferred_element_type=F32)
        u = jnp.maximum(u, 0.0)
        o_ref[...] += jnp.dot((u * u).astype(BF16), wdn_ref[c * ff_chunk:(c + 1) * ff_chunk, :],
                              preferred_element_type=F32)


def _post_call(x2, attn, y, g, bon, consts, *, tm, ff_chunk=1024):
    n = x2.shape[0]
    tm = min(tm, n)
    tok = lambda w: pl.BlockSpec((tm, w), lambda i: (i, 0))
    full = lambda a: pl.BlockSpec(a.shape, lambda i: (0,) * a.ndim, pipeline_mode=pl.Buffered(1))
    if y.ndim == 4:
        tiles_per_seq = y.shape[0] * SUBLANES // tm
        y_spec = pl.BlockSpec((tm // SUBLANES, 1, SUBLANES, RWKV_WIDTH),
                              lambda i: (i % tiles_per_seq, i // tiles_per_seq, 0, 0))
    else:
        y_spec = tok(RWKV_WIDTH)
    return pl.pallas_call(
        functools.partial(_post_body, ff_chunk=ff_chunk),
        grid=(n // tm,),
        in_specs=[tok(D_MODEL), tok(ATTN_WIDTH), y_spec, tok(RWKV_WIDTH), tok(RWKV_WIDTH)]
        + [full(c) for c in consts],
        out_specs=tok(D_MODEL),
        out_shape=jax.ShapeDtypeStruct((n, D_MODEL), F32),
        compiler_params=pltpu.CompilerParams(dimension_semantics=("arbitrary",), vmem_limit_bytes=VMEM_LIMIT),
        name="post",
    )(x2, attn, y, g, bon, *consts)


def _layer(x, p_prev_rows, s0, attn_fn, w, *, tm_pre, tm_post, scan_tb):
    b, t, _ = x.shape
    x2 = x.reshape(b * t, D_MODEL)
    outs = _pre_call(x2, p_prev_rows, w["pre"], tm=tm_pre, seq_len=t)
    q, k, v, r, d, kh, vh, av, bv, g, bon, h_last = outs
    attn, new_k, new_v = attn_fn(q, k, v)
    ops = [a if a.ndim == 4 else a.reshape(b, t, RWKV_WIDTH) for a in (r, d, kh, vh, av, bv)]
    s0r = None if s0 is None else s0.astype(F32).transpose(0, 2, 1, 3).reshape(b, HEAD_DIM, RWKV_WIDTH)
    y, s_fin = _scan_call(*ops, s0r, tb=scan_tb)
    if y.ndim == 3:
        y = y.reshape(b * t, RWKV_WIDTH)
    s_new = s_fin.reshape(b, HEAD_DIM, N_RWKV_HEADS, HEAD_DIM).transpose(0, 2, 1, 3)
    out = _post_call(x2, attn, y, g, bon, w["post"], tm=tm_post)
    return out.reshape(b, t, D_MODEL), new_k, new_v, s_new, h_last


def _prep_weights(norm1_g, w_in, q_norm_g, k_norm_g, rwkv_mu, w_decay_0, w_decay_up, a_0, a_up, g_up, k_k, k_a,
                  r_k, ln_x_g, ln_x_b, w_out, norm2_g, w_ff_up, w_ff_down):
    o = RWKV_OFF
    pieces = ((0, o + 512), (o + 576, o + 1088), (o + 1088, o + 1600), (o + 512, o + 576), (o + 1600, o + 1664),
              (o + 1664, o + 1792))
    w_in_p = jnp.concatenate([w_in[:, lo:hi] for lo, hi in pieces], axis=1).astype(BF16)
    mu_p = jnp.concatenate([rwkv_mu[max(lo - o, 0):hi - o] for lo, hi in pieces]).reshape(1, RWKV_PROJ)
    row = lambda a: a.reshape(1, -1).astype(F32)
    eye = lambda n: jnp.kron(jnp.eye(n, dtype=F32), jnp.ones((HEAD_DIM, HEAD_DIM), F32)).astype(BF16)
    j512, j128 = eye(N_RWKV_HEADS), eye(N_KV_HEADS)
    zeros = jnp.zeros((DECAY_LORA, RWKV_WIDTH), F32)
    wcomb = jnp.concatenate([jnp.concatenate([w_decay_up, zeros], axis=1),
                             jnp.concatenate([zeros, a_up], axis=1)], axis=0).astype(BF16)
    pre = [row(norm1_g), w_in_p, mu_p, row(jnp.tile(q_norm_g, N_Q_HEADS)), row(jnp.tile(k_norm_g, N_KV_HEADS)),
           j512, j128, wcomb, row(w_decay_0), row(a_0), g_up.astype(BF16), row(k_k), row(k_a), row(r_k)]
    post = [j512, row(ln_x_g), row(ln_x_b), w_out.astype(BF16), row(norm2_g), w_ff_up.astype(BF16),
            w_ff_down.astype(BF16)]
    return {"pre": pre, "post": post, "w_rwkv": w_in_p[:, RWKV_OFF:]}


def _forward(x_prompt, x_sample, cache_k, cache_v, state_wkv, state_shift, norm1_g, w_in, q_norm_g, k_norm_g,
             attn_sinks, rwkv_mu, w_decay_0, w_decay_up, a_0, a_up, g_up, k_k, k_a, r_k, ln_x_g, ln_x_b, w_out,
             norm2_g, w_ff_up, w_ff_down, *, tm_pre=512, tm_post=512, scan_tb=32):
    depth = norm1_g.shape[0]
    assert depth == 1
    l = 0
    w = _prep_weights(norm1_g[l], w_in[l], q_norm_g[l], k_norm_g[l], rwkv_mu[l], w_decay_0[l], w_decay_up[l],
                      a_0[l], a_up[l], g_up[l], k_k[l], k_a[l], r_k[l], ln_x_g[l], ln_x_b[l], w_out[l], norm2_g[l],
                      w_ff_up[l], w_ff_down[l])
    sinks = attn_sinks[l].astype(F32)
    bp, tp, _ = x_prompt.shape
    bs, ts, _ = x_sample.shape

    def attn_prompt(q, k, v):
        q3, k3, v3 = (a.reshape(bp, tp, -1) for a in (q, k, v))
        out = _attn_prompt_call(sinks, q3, k3, v3).reshape(bp * tp, ATTN_WIDTH)
        tail = lambda a: a[:, tp - WINDOW:].reshape(bp, WINDOW, N_KV_HEADS, HEAD_DIM)
        return out, tail(k3), tail(v3)

    pprev_p = jnp.zeros((bp, 1, RWKV_PROJ), F32)
    yp, k1, v1, w1, s1 = _layer(x_prompt, pprev_p, None, attn_prompt, w, tm_pre=tm_pre, tm_post=tm_post,
                                scan_tb=scan_tb)
    s1 = s1.reshape(bp, D_MODEL)

    ck, cv = cache_k[l].astype(F32), cache_v[l].astype(F32)
    group = N_Q_HEADS // N_KV_HEADS
    rows = group * ts
    pad = (-ts) % SUBLANES
    keys = WINDOW + ts + pad

    def attn_sample(q, k, v):
        k4 = k.reshape(bs, ts, N_KV_HEADS, HEAD_DIM)
        v4 = v.reshape(bs, ts, N_KV_HEADS, HEAD_DIM)
        kall = jnp.concatenate([ck, k4], axis=1)
        vall = jnp.concatenate([cv, v4], axis=1)
        dense = lambda a: jnp.pad(a, ((0, 0), (0, pad), (0, 0), (0, 0))).reshape(bs, keys, KV_WIDTH)
        q4 = q.reshape(bs, ts, N_KV_HEADS, group, HEAD_DIM).transpose(0, 2, 3, 1, 4).reshape(
            bs, N_KV_HEADS, rows, HEAD_DIM)
        eye = jnp.eye(N_KV_HEADS, dtype=F32)
        qbd = (q4[:, :, :, None, :] * eye[None, :, None, :, None]).reshape(bs, N_KV_HEADS * rows, KV_WIDTH)
        sink_rows = jnp.repeat(sinks, ts)[:, None]
        o = _attn_sample_call(sink_rows, qbd, dense(kall), dense(vall), dec_seq=ts)
        o = o.reshape(bs, N_KV_HEADS, rows, N_KV_HEADS, HEAD_DIM)
        o = jnp.stack([o[:, hk, :, hk, :] for hk in range(N_KV_HEADS)], axis=1)
        o = o.reshape(bs, N_KV_HEADS, group, ts, HEAD_DIM).transpose(0, 3, 1, 2, 4)
        return o.reshape(bs * ts, ATTN_WIDTH), kall[:, ts:], vall[:, ts:]

    p_prev = _matmul_call(state_shift[l], w["w_rwkv"])
    pprev_s = jnp.repeat(p_prev, ts, axis=0)
    ys, k2, v2, w2, h_all = _layer(x_sample, pprev_s, state_wkv[l], attn_sample, w, tm_pre=tm_pre,
                                   tm_post=tm_post, scan_tb=ts)
    s2 = h_all.reshape(bs, ts, D_MODEL)[:, -1]
    st = lambda a: a[None]
    return (yp, ys, st(k1), st(v1), st(w1), st(s1), st(k2), st(v2), st(w2), st(s2))


def kernel(x_prompt, x_sample, cache_k, cache_v, state_wkv, state_shift, norm1_g, w_in, q_norm_g, k_norm_g, attn_sinks, rwkv_mu, w_decay_0, w_decay_up, a_0, a_up, g_up, k_k, k_a, r_k, ln_x_g, ln_x_b, w_out, norm2_g, w_ff_up, w_ff_down):
    return _forward(x_prompt, x_sample, cache_k, cache_v, state_wkv, state_shift, norm1_g, w_in, q_norm_g, k_norm_g,
                    attn_sinks, rwkv_mu, w_decay_0, w_decay_up, a_0, a_up, g_up, k_k, k_a, r_k, ln_x_g, ln_x_b,
                    w_out, norm2_g, w_ff_up, w_ff_down)
```

```python
import functools

import jax
import jax.numpy as jnp
from jax import lax
from jax.experimental import pallas as pl
from jax.experimental.pallas import tpu as pltpu

D_MODEL = 1024
HEAD_DIM = 64
ATTN_WIDTH = 512
N_Q_HEADS = 8
N_KV_HEADS = 2
KV_WIDTH = 128
RWKV_WIDTH = 512
N_RWKV_HEADS = 8
WINDOW = 128
SCALE = HEAD_DIM ** -0.5
DECAY_LORA = 64
AAA_LORA = 64
GATE_LORA = 128
D_FF = 4 * D_MODEL
RMS_EPS = 1e-6
GN_EPS = 64e-5
RWKV_OFF = ATTN_WIDTH + 2 * KV_WIDTH
RWKV_PROJ = 3 * RWKV_WIDTH + DECAY_LORA + AAA_LORA + GATE_LORA
IN_WIDTH = RWKV_OFF + RWKV_PROJ
LANES = 128
SUBLANES = 8
VMEM_LIMIT = 56 * 1024 * 1024

F32 = jnp.float32
BF16 = jnp.bfloat16


def _seg_sum(x, ones_blk):
    return jnp.dot(x.astype(BF16), ones_blk, preferred_element_type=F32)


def _time_tiled(seq_len):
    return seq_len % SUBLANES == 0


def _pre_body(x_ref, pprev_ref, g1_ref, win_ref, mu_ref, qg_ref, kg_ref, j512_ref, j128_ref,
              wcomb_ref, w0_ref, a0_ref, gup_ref, kk_ref, ka_ref, rk_ref,
              q_o, k_o, v_o, r_o, d_o, kh_o, vh_o, a_o, b_o, g_o, bon_o, h_o,
              carry, *, tm, tiles_per_seq, seq_len):
    x = x_ref[...]
    ms = jnp.mean(x * x, axis=-1, keepdims=True)
    h = x * lax.rsqrt(ms + RMS_EPS) * g1_ref[...]
    proj = jnp.dot(h.astype(BF16), win_ref[...], preferred_element_type=F32)

    j512 = j512_ref[...]
    q = proj[:, 0:ATTN_WIDTH]
    qs = _seg_sum(q * q, j512) * (1.0 / HEAD_DIM)
    q_o[...] = q * lax.rsqrt(qs + RMS_EPS) * qg_ref[...]
    k = proj[:, ATTN_WIDTH:ATTN_WIDTH + KV_WIDTH]
    ks = _seg_sum(k * k, j128_ref[...]) * (1.0 / HEAD_DIM)
    k_o[...] = k * lax.rsqrt(ks + RMS_EPS) * kg_ref[...]
    v_o[...] = proj[:, ATTN_WIDTH + KV_WIDTH:RWKV_OFF]

    p = proj[:, RWKV_OFF:]
    rolled = pltpu.roll(p, shift=1, axis=0)
    row = lax.broadcasted_iota(jnp.int32, (tm, 1), 0)
    if tiles_per_seq is not None:
        t = pl.program_id(0) % tiles_per_seq
        first = jnp.where(t == 0, pprev_ref[0], carry[...])
        p_shift = jnp.where(row == 0, first, rolled)
        carry[...] = p[tm - 1:tm, :]

        @pl.when(t == tiles_per_seq - 1)
        def _():
            h_o[0] = h[tm - 1:tm, :]
    else:
        p_shift = jnp.where(row % seq_len == 0, pprev_ref[...], rolled)
        h_o[...] = h
    pm = p + (p_shift - p) * mu_ref[...]

    xr = pm[:, 0:512]
    xk = pm[:, 512:1024]
    xv = pm[:, 1024:1536]
    xwa = pm[:, 1536:1664]
    xg = pm[:, 1664:1792]
    lane = lax.broadcasted_iota(jnp.int32, (1, LANES), 1)
    wa_in = jnp.where(lane < DECAY_LORA, jnp.tanh(xwa), xwa)
    pre = jnp.dot(wa_in.astype(BF16), wcomb_ref[...], preferred_element_type=F32)
    w_pre = pre[:, 0:512] + w0_ref[...]
    a_pre = pre[:, 512:1024] + a0_ref[...]
    z = -w_pre
    softplus = jnp.maximum(z, 0.0) + jnp.log(1.0 + jnp.exp(-jnp.abs(z)))
    w_log = -softplus - 0.5
    d_o[...] = jnp.exp(-jnp.exp(w_log)).reshape(d_o.shape)
    a_gate = 1.0 / (1.0 + jnp.exp(-a_pre))
    sg = 1.0 / (1.0 + jnp.exp(-xg))
    g_o[...] = jnp.dot(sg.astype(BF16), gup_ref[...], preferred_element_type=F32)
    kkv = xk * kk_ref[...]
    n2 = _seg_sum(kkv * kkv, j512)
    kkn = kkv / jnp.maximum(jnp.sqrt(n2), 1e-12)
    k_h = xk * (1.0 + (a_gate - 1.0) * ka_ref[...])
    r_o[...] = xr.reshape(r_o.shape)
    kh_o[...] = k_h.reshape(kh_o.shape)
    vh_o[...] = xv.reshape(vh_o.shape)
    a_o[...] = (-kkn).reshape(a_o.shape)
    b_o[...] = (kkn * a_gate).reshape(b_o.shape)
    bon_o[...] = _seg_sum(xr * k_h * rk_ref[...], j512) * xv


def _pre_call(x2, pprev, consts, *, tm, seq_len):
    n = x2.shape[0]
    tm = min(tm, n)
    n_tiles = n // tm
    per_seq = seq_len >= tm
    tiles_per_seq = seq_len // tm if per_seq else None
    n_seq = n // seq_len
    full = lambda a: pl.BlockSpec(a.shape, lambda i: (0,) * a.ndim)
    tok = lambda w: pl.BlockSpec((tm, w), lambda i: (i, 0))
    if per_seq:
        pprev_spec = pl.BlockSpec((1, 1, RWKV_PROJ), lambda i: (i // tiles_per_seq, 0, 0))
        h_spec = pl.BlockSpec((1, 1, D_MODEL), lambda i: (i // tiles_per_seq, 0, 0))
        h_shape = jax.ShapeDtypeStruct((n_seq, 1, D_MODEL), F32)
    else:
        pprev_spec = tok(RWKV_PROJ)
        h_spec = tok(D_MODEL)
        h_shape = jax.ShapeDtypeStruct((n, D_MODEL), F32)
    widths = [ATTN_WIDTH, KV_WIDTH, KV_WIDTH] + [RWKV_WIDTH] * 8
    out_shape = [jax.ShapeDtypeStruct((n, w), F32) for w in widths] + [h_shape]
    out_specs = [tok(w) for w in widths] + [h_spec]
    if per_seq and _time_tiled(seq_len):
        for idx in range(3, 9):
            out_shape[idx] = jax.ShapeDtypeStruct((seq_len // SUBLANES, n_seq, SUBLANES, RWKV_WIDTH), F32)
            out_specs[idx] = pl.BlockSpec((tm // SUBLANES, 1, SUBLANES, RWKV_WIDTH),
                                          lambda i: (i % tiles_per_seq, i // tiles_per_seq, 0, 0))
    body = functools.partial(_pre_body, tm=tm, tiles_per_seq=tiles_per_seq, seq_len=seq_len)
    return pl.pallas_call(
        body,
        grid=(n_tiles,),
        in_specs=[tok(D_MODEL), pprev_spec] + [full(c) for c in consts],
        out_specs=out_specs,
        out_shape=out_shape,
        scratch_shapes=[pltpu.VMEM((1, RWKV_PROJ), F32)],
        compiler_params=pltpu.CompilerParams(dimension_semantics=("arbitrary",), vmem_limit_bytes=VMEM_LIMIT),
        name="pre",
    )(x2, pprev, *consts)


def _matmul_body(x_ref, w_ref, o_ref):
    o_ref[...] = jnp.dot(x_ref[...].astype(BF16), w_ref[...], preferred_element_type=F32)


def _matmul_call(x, w):
    return pl.pallas_call(_matmul_body, out_shape=jax.ShapeDtypeStruct((x.shape[0], w.shape[1]), F32),
                          name="shift_proj")(x, w)


def _softmax_sink(s, valid, sink):
    s = jnp.where(valid, s, -jnp.inf)
    m = jnp.maximum(jnp.max(s, axis=-1, keepdims=True), sink)
    p = jnp.exp(s - m)
    den = jnp.sum(p, axis=-1, keepdims=True) + jnp.exp(sink - m)
    return p / den


def _attn_prompt_body(sink_ref, q_ref, kc_ref, kp_ref, vc_ref, vp_ref, o_ref, *, nq):
    for sub in range(nq):
        rows = slice(sub * WINDOW, (sub + 1) * WINDOW)
        prev = slice((sub - 1) * WINDOW, sub * WINDOW)
        k_prev, v_prev = (kp_ref[0], vp_ref[0]) if sub == 0 else (kc_ref[0, prev], vc_ref[0, prev])
        first = (pl.program_id(1) == 0) if sub == 0 else None
        _attn_block(sink_ref, q_ref[0, rows], k_prev, kc_ref[0, rows], v_prev, vc_ref[0, rows], first,
                    o_ref.at[0, rows])


def _attn_block(sink_ref, q, k_prev, k_cur, v_prev, v_cur, first, o_ref):
    kcat = jnp.concatenate([k_prev, k_cur], axis=0)
    vcat = jnp.concatenate([v_prev, v_cur], axis=0)
    krot = pltpu.roll(kcat, shift=HEAD_DIM, axis=1)
    vrot = pltpu.roll(vcat, shift=HEAD_DIM, axis=1)
    lane = lax.broadcasted_iota(jnp.int32, (1, LANES), 1)
    lo = lane < HEAD_DIM
    row = lax.broadcasted_iota(jnp.int32, (WINDOW, WINDOW), 0)
    col = lax.broadcasted_iota(jnp.int32, (WINDOW, WINDOW), 1)
    upper = col > row
    no_prev = None if first is None else upper & first
    for hk in range(N_KV_HEADS):
        src_lo, src_hi = (kcat, krot) if hk == 0 else (krot, kcat)
        k_pads = (jnp.where(lo, src_lo, 0.0).astype(BF16), jnp.where(lo, 0.0, src_hi).astype(BF16))
        vsrc_lo, vsrc_hi = (vcat, vrot) if hk == 0 else (vrot, vcat)
        ones = jnp.ones((2 * WINDOW, LANES), BF16)
        v_pads = tuple(jnp.concatenate([vp.astype(BF16), ones], axis=1)
                       for vp in (jnp.where(lo, vsrc_lo, 0.0), jnp.where(lo, 0.0, vsrc_hi)))
        for pr in range(2):
            pair = hk * 2 + pr
            lanes = slice(pair * LANES, (pair + 1) * LANES)
            qp = (q[:, lanes] * SCALE).astype(BF16)
            out = None
            for e in range(2):
                s = lax.dot_general(qp, k_pads[e], (((1,), (1,)), ((), ())),
                                    preferred_element_type=F32)
                s = jnp.where(upper, s[:, 0:WINDOW], s[:, WINDOW:2 * WINDOW])
                if no_prev is not None:
                    s = jnp.where(no_prev, -jnp.inf, s)
                sink = sink_ref[pair * 2 + e]
                m = jnp.maximum(jnp.max(s, axis=-1, keepdims=True), sink)
                p = jnp.exp(s - m)
                p2 = jnp.concatenate([jnp.where(upper, p, 0.0), jnp.where(upper, 0.0, p)], axis=1)
                pv = jnp.dot(p2.astype(BF16), v_pads[e], preferred_element_type=F32)
                scaled = pv[:, 0:LANES] / (pv[:, LANES:2 * LANES] + jnp.exp(sink - m))
                out = scaled if out is None else out + scaled
            o_ref[:, lanes] = out


def _attn_prompt_call(sinks, q, k, v, *, nq=4):
    b, t, _ = q.shape
    nb = t // WINDOW
    nq = min(nq, nb)
    assert nb % nq == 0
    cur = lambda w: pl.BlockSpec((1, nq * WINDOW, w), lambda i, j: (i, j, 0))
    prev = lambda w: pl.BlockSpec((1, WINDOW, w), lambda i, j: (i, jnp.maximum(nq * j - 1, 0), 0))
    return pl.pallas_call(
        functools.partial(_attn_prompt_body, nq=nq),
        grid=(b, nb // nq),
        in_specs=[pl.BlockSpec(memory_space=pltpu.SMEM), cur(ATTN_WIDTH), cur(KV_WIDTH), prev(KV_WIDTH),
                  cur(KV_WIDTH), prev(KV_WIDTH)],
        out_specs=cur(ATTN_WIDTH),
        out_shape=jax.ShapeDtypeStruct((b, t, ATTN_WIDTH), F32),
        compiler_params=pltpu.CompilerParams(dimension_semantics=("arbitrary", "arbitrary")),
        name="attn_prompt",
    )(sinks, q, k, k, v, v)


def _attn_sample_body(sink_ref, q_ref, k_ref, v_ref, o_ref, *, bb, rows, keys, dec_seq):
    r = lax.broadcasted_iota(jnp.int32, (rows, keys), 0)
    c = lax.broadcasted_iota(jnp.int32, (rows, keys), 1)
    t = r % dec_seq
    valid = (c > t) & (c <= t + WINDOW)
    sink = sink_ref[...]

    def one(i, _):
        q = q_ref[i].astype(BF16)
        k = k_ref[i].astype(BF16)
        s = lax.dot_general(q, k, (((1,), (1,)), ((), ())), preferred_element_type=F32) * SCALE
        probs = _softmax_sink(s, valid, sink)
        o_ref[i] = jnp.dot(probs.astype(BF16), v_ref[i].astype(BF16), preferred_element_type=F32)
        return 0

    lax.fori_loop(0, bb, one, 0, unroll=8)


def _attn_sample_call(sink_rows, q, kall, vall, *, dec_seq, bb=16):
    bsz, rows, _ = q.shape
    keys = kall.shape[1]
    blk = lambda r: pl.BlockSpec((bb, r, KV_WIDTH), lambda i: (i, 0, 0))
    body = functools.partial(_attn_sample_body, bb=bb, rows=rows, keys=keys, dec_seq=dec_seq)
    return pl.pallas_call(
        body,
        grid=(bsz // bb,),
        in_specs=[pl.BlockSpec((rows, 1), lambda i: (0, 0)), blk(rows), blk(keys), blk(keys)],
        out_specs=blk(rows),
        out_shape=jax.ShapeDtypeStruct((bsz, rows, KV_WIDTH), F32),
        compiler_params=pltpu.CompilerParams(dimension_semantics=("arbitrary",)),
        name="attn_sample",
    )(sink_rows, q, kall, vall)


IB_GROUP = 4


BATCH_GROUP = LANES // N_RWKV_HEADS
HEAD_PAIRS = N_RWKV_HEADS // 2


def _to_chains(x):
    m = jnp.concatenate([x[:, hp * LANES:(hp + 1) * LANES] for hp in range(HEAD_PAIRS)], axis=0)
    mt = m.T
    return jnp.concatenate([mt[0:HEAD_DIM, :], mt[HEAD_DIM:2 * HEAD_DIM, :]], axis=1)


def _from_chains(y):
    m = jnp.concatenate([y[:, 0:HEAD_DIM], y[:, HEAD_DIM:2 * HEAD_DIM]], axis=0)
    mt = m.T
    return jnp.concatenate([mt[hp * BATCH_GROUP:(hp + 1) * BATCH_GROUP, :] for hp in range(HEAD_PAIRS)], axis=1)


def _zero_like_dep(x):
    u = lax.bitcast_convert_type(x, jnp.uint32)
    z = lax.shift_right_logical(lax.shift_right_logical(u, jnp.uint32(16)), jnp.uint32(16))
    return lax.bitcast_convert_type(z, F32)


def _after(x, zero):
    return x + zero


def _scan_body(*refs, tb, n_tb, zero_init, tiled):
    n_in = 6 if zero_init else 7
    raw = refs[:6]
    s0_ref = None if zero_init else refs[6]
    y_out, st_ref = refs[n_in:n_in + 2]
    scratch = refs[n_in + 2:]
    slots = (scratch[0:6], scratch[6:12])
    state, sa, g_run = scratch[12:]
    s = pl.program_id(1)
    n_ib = HEAD_DIM // SUBLANES

    def rows_at(t):
        if tiled:
            return (t // SUBLANES, slice(None), t % SUBLANES, slice(None))
        return (slice(None), t, slice(None))

    def stage(t, wr):
        c_r, c_d, c_k, c_v, c_a, c_b = (_to_chains(src[rows_at(t)]) for src in raw)
        g_prev = g_run[...]
        g = g_prev * c_d
        g_inv = 1.0 / g
        g_run[...] = g
        staged = (c_r * g, c_k * g_inv, c_v, c_a * g_prev, c_b * g_inv)
        for dst, val in zip(wr[:5], staged):
            dst[t] = val
        return staged

    @pl.when(s == 0)
    def _():
        if zero_init:
            state[...] = jnp.zeros(state.shape, F32)
        else:
            def load_i(i, _):
                state[:, i, :] = _to_chains(s0_ref[:, i, :])
                return 0

            lax.fori_loop(0, HEAD_DIM, load_i, 0, unroll=8)
        slots[1][5][...] = jnp.ones((HEAD_DIM, LANES), F32)
        g_run[...] = jnp.ones((HEAD_DIM, LANES), F32)

        def stage_only(t, _):
            stage(t, slots[0])
            return 0

        lax.fori_loop(0, tb, stage_only, 0, unroll=min(tb, 2))
        slots[0][5][...] = g_run[...]

    def run_block(rd, wr):
        r_ref, k_ref, v_ref, a_ref, b_ref, _ = rd
        wr_g = wr[5]
        n_grp = n_ib // IB_GROUP
        assert n_grp == 2

        for grp in range(n_grp):
            rows = [pl.ds((grp * IB_GROUP + u) * SUBLANES, SUBLANES) for u in range(IB_GROUP)]

            def init_j(j, accs, rows=rows):
                jr = pl.ds(j, 1)
                gj, aj = wr_g[jr, :], a_ref[0, jr, :]
                out = []
                for u in range(IB_GROUP):
                    s_full = state[j, rows[u], :] * gj
                    state[j, rows[u], :] = s_full
                    out.append(accs[u] + s_full * aj)
                return tuple(out)

            zero = tuple(jnp.zeros((SUBLANES, LANES), F32) for _ in range(IB_GROUP))
            accs = lax.fori_loop(0, HEAD_DIM, init_j, zero, unroll=8)
            for u in range(IB_GROUP):
                sa[rows[u], :] = accs[u]
        g_run[...] = jnp.ones((HEAD_DIM, LANES), F32)

        def step(t, y_prev):
            y_tok = _from_chains(y_prev)
            y_out[rows_at(jnp.maximum(t - 1, 0))] = y_tok
            gates = {(0, 8): _zero_like_dep(y_tok[0:1, 0:LANES])}
            for n, val in enumerate(stage(t, wr)):
                gates[(n // 3, 24 + 16 * (n % 3))] = _zero_like_dep(val[0:1, :])
            y_new = []
            tn = jnp.minimum(t + 1, tb - 1)
            for grp in range(n_ib // IB_GROUP):
                rows = [pl.ds((grp * IB_GROUP + u) * SUBLANES, SUBLANES) for u in range(IB_GROUP)]
                sav = [sa[rw, :] for rw in rows]
                vv = [v_ref[t, rw, :] for rw in rows]
                yacc = [jnp.zeros((SUBLANES, LANES), F32) for _ in range(IB_GROUP)]
                sacc = [jnp.zeros((SUBLANES, LANES), F32) for _ in range(IB_GROUP)]
                for j in range(HEAD_DIM):
                    jr = pl.ds(j, 1)
                    bj = b_ref[t, jr, :]
                    if (grp, j) in gates:
                        bj = _after(bj, gates[(grp, j)])
                    kj = k_ref[t, jr, :]
                    rj = r_ref[t, jr, :]
                    aj = a_ref[tn, jr, :]
                    for u in range(IB_GROUP):
                        s_new = state[j, rows[u], :] + sav[u] * bj + vv[u] * kj
                        state[j, rows[u], :] = s_new
                        yacc[u] = yacc[u] + s_new * rj
                        sacc[u] = sacc[u] + s_new * aj
                for u in range(IB_GROUP):
                    sa[rows[u], :] = sacc[u]
                y_new.extend(yacc)
            return jnp.concatenate(y_new, axis=0)

        y_last = lax.fori_loop(0, tb, step, jnp.zeros((HEAD_DIM, LANES), F32))
        y_out[rows_at(tb - 1)] = _from_chains(y_last)
        wr_g[...] = g_run[...]

    @pl.when((s > 0) & (s % 2 == 0))
    def _():
        run_block(slots[1], slots[0])

    @pl.when(s % 2 == 1)
    def _():
        run_block(slots[0], slots[1])

    @pl.when(s == n_tb)
    def _():
        g_last = slots[(n_tb - 1) % 2][5][...]

        def store_i(i, _):
            st_ref[:, i, :] = _from_chains(state[:, i, :] * g_last)
            return 0

        lax.fori_loop(0, HEAD_DIM, store_i, 0, unroll=8)


def _scan_call(r, d, k, v, a, b, s0, *, tb):
    tiled = r.ndim == 4
    bsz, t_len = (r.shape[1], r.shape[0] * SUBLANES) if tiled else (r.shape[0], r.shape[1])
    bg = BATCH_GROUP
    n_tb = t_len // tb
    if tiled:
        blk = (tb // SUBLANES, bg, SUBLANES, RWKV_WIDTH)
        op_in = pl.BlockSpec(blk, lambda g, s: (jnp.minimum(s, n_tb - 1), g, 0, 0))
        op_out = pl.BlockSpec(blk, lambda g, s: (jnp.maximum(s - 1, 0), g, 0, 0))
    else:
        op_in = pl.BlockSpec((bg, tb, RWKV_WIDTH), lambda g, s: (g, jnp.minimum(s, n_tb - 1), 0))
        op_out = pl.BlockSpec((bg, tb, RWKV_WIDTH), lambda g, s: (g, jnp.maximum(s - 1, 0), 0))
    st = pl.BlockSpec((bg, HEAD_DIM, RWKV_WIDTH), lambda g, s: (g, 0, 0))
    zero_init = s0 is None
    body = functools.partial(_scan_body, tb=tb, n_tb=n_tb, zero_init=zero_init, tiled=tiled)
    return pl.pallas_call(
        body,
        grid=(bsz // bg, n_tb + 1),
        in_specs=[op_in] * 6 + ([] if zero_init else [st]),
        out_specs=[op_out, st],
        out_shape=[jax.ShapeDtypeStruct(r.shape, F32),
                   jax.ShapeDtypeStruct((bsz, HEAD_DIM, RWKV_WIDTH), F32)],
        scratch_shapes=([pltpu.VMEM((tb, HEAD_DIM, LANES), F32)] * 5 + [pltpu.VMEM((HEAD_DIM, LANES), F32)]) * 2
        + [pltpu.VMEM((HEAD_DIM, HEAD_DIM, LANES), F32), pltpu.VMEM((HEAD_DIM, LANES), F32),
           pltpu.VMEM((HEAD_DIM, LANES), F32)],
        compiler_params=pltpu.CompilerParams(dimension_semantics=("arbitrary", "arbitrary"),
                                             vmem_limit_bytes=VMEM_LIMIT),
        name="wkv_scan",
    )(r, d, k, v, a, b, *([] if zero_init else [s0]))


def _post_body(x_ref, at_ref, y_ref, g_ref, bon_ref, j512_ref, lng_ref, lnb_ref, wout_ref, g2_ref, wup_ref,
               wdn_ref, o_ref, *, ff_chunk):
    j512 = j512_ref[...]
    y = y_ref[...].reshape(g_ref.shape)
    mu = _seg_sum(y, j512) * (1.0 / HEAD_DIM)
    yc = y - mu
    var = _seg_sum(yc * yc, j512) * (1.0 / HEAD_DIM)
    yn = yc * lax.rsqrt(var + GN_EPS) * lng_ref[...] + lnb_ref[...]
    rw = (yn + bon_ref[...]) * g_ref[...]
    x1 = (x_ref[...]
          + jnp.dot(at_ref[...].astype(BF16), wout_ref[0:ATTN_WIDTH, :], preferred_element_type=F32)
          + jnp.dot(rw.astype(BF16), wout_ref[ATTN_WIDTH:D_MODEL, :], preferred_element_type=F32))
    ms = jnp.mean(x1 * x1, axis=-1, keepdims=True)
    h2 = (x1 * lax.rsqrt(ms + RMS_EPS) * g2_ref[...]).astype(BF16)
    o_ref[...] = x1
    for c in range(D_FF // ff_chunk):
        u = jnp.dot(h2, wup_ref[:, c * ff_chunk:(c + 1) * ff_chunk], preferred_element_type=F32)
        u = jnp.maximum(u, 0.0)
        o_ref[...] += jnp.dot((u * u).astype(BF16), wdn_ref[c * ff_chunk:(c + 1) * ff_chunk, :],
                              preferred_element_type=F32)


def _post_call(x2, attn, y, g, bon, consts, *, tm, ff_chunk=1024):
    n = x2.shape[0]
    tm = min(tm, n)
    tok = lambda w: pl.BlockSpec((tm, w), lambda i: (i, 0))
    full = lambda a: pl.BlockSpec(a.shape, lambda i: (0,) * a.ndim, pipeline_mode=pl.Buffered(1))
    if y.ndim == 4:
        tiles_per_seq = y.shape[0] * SUBLANES // tm
        y_spec = pl.BlockSpec((tm // SUBLANES, 1, SUBLANES, RWKV_WIDTH),
                              lambda i: (i % tiles_per_seq, i // tiles_per_seq, 0, 0))
    else:
        y_spec = tok(RWKV_WIDTH)
    return pl.pallas_call(
        functools.partial(_post_body, ff_chunk=ff_chunk),
        grid=(n // tm,),
        in_specs=[tok(D_MODEL), tok(ATTN_WIDTH), y_spec, tok(RWKV_WIDTH), tok(RWKV_WIDTH)]
        + [full(c) for c in consts],
        out_specs=tok(D_MODEL),
        out_shape=jax.ShapeDtypeStruct((n, D_MODEL), F32),
        compiler_params=pltpu.CompilerParams(dimension_semantics=("arbitrary",), vmem_limit_bytes=VMEM_LIMIT),
        name="post",
    )(x2, attn, y, g, bon, *consts)


def _layer(x, p_prev_rows, s0, attn_fn, w, *, tm_pre, tm_post, scan_tb):
    b, t, _ = x.shape
    x2 = x.reshape(b * t, D_MODEL)
    outs = _pre_call(x2, p_prev_rows, w["pre"], tm=tm_pre, seq_len=t)
    q, k, v, r, d, kh, vh, av, bv, g, bon, h_last = outs
    attn, new_k, new_v = attn_fn(q, k, v)
    ops = [a if a.ndim == 4 else a.reshape(b, t, RWKV_WIDTH) for a in (r, d, kh, vh, av, bv)]
    s0r = None if s0 is None else s0.astype(F32).transpose(0, 2, 1, 3).reshape(b, HEAD_DIM, RWKV_WIDTH)
    y, s_fin = _scan_call(*ops, s0r, tb=scan_tb)
    if y.ndim == 3:
        y = y.reshape(b * t, RWKV_WIDTH)
    s_new = s_fin.reshape(b, HEAD_DIM, N_RWKV_HEADS, HEAD_DIM).transpose(0, 2, 1, 3)
    out = _post_call(x2, attn, y, g, bon, w["post"], tm=tm_post)
    return out.reshape(b, t, D_MODEL), new_k, new_v, s_new, h_last


def _prep_weights(norm1_g, w_in, q_norm_g, k_norm_g, rwkv_mu, w_decay_0, w_decay_up, a_0, a_up, g_up, k_k, k_a,
                  r_k, ln_x_g, ln_x_b, w_out, norm2_g, w_ff_up, w_ff_down):
    o = RWKV_OFF
    pieces = ((0, o + 512), (o + 576, o + 1088), (o + 1088, o + 1600), (o + 512, o + 576), (o + 1600, o + 1664),
              (o + 1664, o + 1792))
    w_in_p = jnp.concatenate([w_in[:, lo:hi] for lo, hi in pieces], axis=1).astype(BF16)
    mu_p = jnp.concatenate([rwkv_mu[max(lo - o, 0):hi - o] for lo, hi in pieces]).reshape(1, RWKV_PROJ)
    row = lambda a: a.reshape(1, -1).astype(F32)
    eye = lambda n: jnp.kron(jnp.eye(n, dtype=F32), jnp.ones((HEAD_DIM, HEAD_DIM), F32)).astype(BF16)
    j512, j128 = eye(N_RWKV_HEADS), eye(N_KV_HEADS)
    zeros = jnp.zeros((DECAY_LORA, RWKV_WIDTH), F32)
    wcomb = jnp.concatenate([jnp.concatenate([w_decay_up, zeros], axis=1),
                             jnp.concatenate([zeros, a_up], axis=1)], axis=0).astype(BF16)
    pre = [row(norm1_g), w_in_p, mu_p, row(jnp.tile(q_norm_g, N_Q_HEADS)), row(jnp.tile(k_norm_g, N_KV_HEADS)),
           j512, j128, wcomb, row(w_decay_0), row(a_0), g_up.astype(BF16), row(k_k), row(k_a), row(r_k)]
    post = [j512, row(ln_x_g), row(ln_x_b), w_out.astype(BF16), row(norm2_g), w_ff_up.astype(BF16),
            w_ff_down.astype(BF16)]
    return {"pre": pre, "post": post, "w_rwkv": w_in_p[:, RWKV_OFF:]}


def _forward(x_prompt, x_sample, cache_k, cache_v, state_wkv, state_shift, norm1_g, w_in, q_norm_g, k_norm_g,
             attn_sinks, rwkv_mu, w_decay_0, w_decay_up, a_0, a_up, g_up, k_k, k_a, r_k, ln_x_g, ln_x_b, w_out,
             norm2_g, w_ff_up, w_ff_down, *, tm_pre=512, tm_post=512, scan_tb=32):
    depth = norm1_g.shape[0]
    assert depth == 1
    l = 0
    w = _prep_weights(norm1_g[l], w_in[l], q_norm_g[l], k_norm_g[l], rwkv_mu[l], w_decay_0[l], w_decay_up[l],
                      a_0[l], a_up[l], g_up[l], k_k[l], k_a[l], r_k[l], ln_x_g[l], ln_x_b[l], w_out[l], norm2_g[l],
                      w_ff_up[l], w_ff_down[l])
    sinks = attn_sinks[l].astype(F32)
    bp, tp, _ = x_prompt.shape
    bs, ts, _ = x_sample.shape

    def attn_prompt(q, k, v):
        q3, k3, v3 = (a.reshape(bp, tp, -1) for a in (q, k, v))
        out = _attn_prompt_call(sinks, q3, k3, v3).reshape(bp * tp, ATTN_WIDTH)
        tail = lambda a: a[:, tp - WINDOW:].reshape(bp, WINDOW, N_KV_HEADS, HEAD_DIM)
        return out, tail(k3), tail(v3)

    pprev_p = jnp.zeros((bp, 1, RWKV_PROJ), F32)
    yp, k1, v1, w1, s1 = _layer(x_prompt, pprev_p, None, attn_prompt, w, tm_pre=tm_pre, tm_post=tm_post,
                                scan_tb=scan_tb)
    s1 = s1.reshape(bp, D_MODEL)

    ck, cv = cache_k[l].astype(F32), cache_v[l].astype(F32)
    group = N_Q_HEADS // N_KV_HEADS
    rows = group * ts
    keys = WINDOW + ts

    def attn_sample(q, k, v):
        k4 = k.reshape(bs, ts, N_KV_HEADS, HEAD_DIM)
        v4 = v.reshape(bs, ts, N_KV_HEADS, HEAD_DIM)
        kall = jnp.concatenate([ck, k4], axis=1)
        vall = jnp.concatenate([cv, v4], axis=1)
        dense = lambda a: a.reshape(bs, keys, KV_WIDTH)
        q4 = q.reshape(bs, ts, N_KV_HEADS, group, HEAD_DIM).transpose(0, 2, 3, 1, 4).reshape(
            bs, N_KV_HEADS, rows, HEAD_DIM)
        eye = jnp.eye(N_KV_HEADS, dtype=F32)
        qbd = (q4[:, :, :, None, :] * eye[None, :, None, :, None]).reshape(bs, N_KV_HEADS * rows, KV_WIDTH)
        sink_rows = jnp.repeat(sinks, ts)[:, None]
        o = _attn_sample_call(sink_rows, qbd, dense(kall), dense(vall), dec_seq=ts)
        o = o.reshape(bs, N_KV_HEADS, rows, N_KV_HEADS, HEAD_DIM)
        o = jnp.stack([o[:, hk, :, hk, :] for hk in range(N_KV_HEADS)], axis=1)
        o = o.reshape(bs, N_KV_HEADS, group, ts, HEAD_DIM).transpose(0, 3, 1, 2, 4)
        return o.reshape(bs * ts, ATTN_WIDTH), kall[:, ts:], vall[:, ts:]

    p_prev = _matmul_call(state_shift[l], w["w_rwkv"])
    pprev_s = jnp.repeat(p_prev, ts, axis=0)
    ys, k2, v2, w2, h_all = _layer(x_sample, pprev_s, state_wkv[l], attn_sample, w, tm_pre=tm_pre,
                                   tm_post=tm_post, scan_tb=ts)
    s2 = h_all.reshape(bs, ts, D_MODEL)[:, -1]
    st = lambda a: a[None]
    return (yp, ys, st(k1), st(v1), st(w1), st(s1), st(k2), st(v2), st(w2), st(s2))


def kernel(x_prompt, x_sample, cache_k, cache_v, state_wkv, state_shift, norm1_g, w_in, q_norm_g, k_norm_g, attn_sinks, rwkv_mu, w_decay_0, w_decay_up, a_0, a_up, g_up, k_k, k_a, r_k, ln_x_g, ln_x_b, w_out, norm2_g, w_ff_up, w_ff_down):
    return _forward(x_prompt, x_sample, cache_k, cache_v, state_wkv, state_shift, norm1_g, w_in, q_norm_g, k_norm_g,
                    attn_sinks, rwkv_mu, w_decay_0, w_decay_up, a_0, a_up, g_up, k_k, k_a, r_k, ln_x_g, ln_x_b,
                    w_out, norm2_g, w_ff_up, w_ff_down)
```

```python
import functools

import jax
import jax.numpy as jnp
from jax import lax
from jax.experimental import pallas as pl
from jax.experimental.pallas import tpu as pltpu

D_MODEL = 1024
HEAD_DIM = 64
ATTN_WIDTH = 512
N_Q_HEADS = 8
N_KV_HEADS = 2
KV_WIDTH = 128
RWKV_WIDTH = 512
N_RWKV_HEADS = 8
WINDOW = 128
SCALE = HEAD_DIM ** -0.5
DECAY_LORA = 64
AAA_LORA = 64
GATE_LORA = 128
D_FF = 4 * D_MODEL
RMS_EPS = 1e-6
GN_EPS = 64e-5
RWKV_OFF = ATTN_WIDTH + 2 * KV_WIDTH
RWKV_PROJ = 3 * RWKV_WIDTH + DECAY_LORA + AAA_LORA + GATE_LORA
IN_WIDTH = RWKV_OFF + RWKV_PROJ
LANES = 128
SUBLANES = 8
VMEM_LIMIT = 56 * 1024 * 1024

F32 = jnp.float32
BF16 = jnp.bfloat16


def _seg_sum(x, ones_blk):
    return jnp.dot(x.astype(BF16), ones_blk, preferred_element_type=F32)


def _time_tiled(seq_len):
    return seq_len % SUBLANES == 0


def _pre_body(x_ref, pprev_ref, g1_ref, win_ref, mu_ref, qg_ref, kg_ref, j512_ref, j128_ref,
              wcomb_ref, w0_ref, a0_ref, gup_ref, kk_ref, ka_ref, rk_ref,
              q_o, k_o, v_o, r_o, d_o, kh_o, vh_o, a_o, b_o, g_o, bon_o, h_o,
              carry, *, tm, tiles_per_seq, seq_len):
    x = x_ref[...]
    ms = jnp.mean(x * x, axis=-1, keepdims=True)
    h = x * lax.rsqrt(ms + RMS_EPS) * g1_ref[...]
    proj = jnp.dot(h.astype(BF16), win_ref[...], preferred_element_type=F32)

    j512 = j512_ref[...]
    q = proj[:, 0:ATTN_WIDTH]
    qs = _seg_sum(q * q, j512) * (1.0 / HEAD_DIM)
    q_o[...] = q * lax.rsqrt(qs + RMS_EPS) * qg_ref[...]
    k = proj[:, ATTN_WIDTH:ATTN_WIDTH + KV_WIDTH]
    ks = _seg_sum(k * k, j128_ref[...]) * (1.0 / HEAD_DIM)
    k_o[...] = k * lax.rsqrt(ks + RMS_EPS) * kg_ref[...]
    v_o[...] = proj[:, ATTN_WIDTH + KV_WIDTH:RWKV_OFF]

    p = proj[:, RWKV_OFF:]
    rolled = pltpu.roll(p, shift=1, axis=0)
    row = lax.broadcasted_iota(jnp.int32, (tm, 1), 0)
    if tiles_per_seq is not None:
        t = pl.program_id(0) % tiles_per_seq
        first = jnp.where(t == 0, pprev_ref[0], carry[...])
        p_shift = jnp.where(row == 0, first, rolled)
        carry[...] = p[tm - 1:tm, :]

        @pl.when(t == tiles_per_seq - 1)
        def _():
            h_o[0] = h[tm - 1:tm, :]
    else:
        p_shift = jnp.where(row % seq_len == 0, pprev_ref[...], rolled)
        h_o[...] = h
    pm = p + (p_shift - p) * mu_ref[...]

    xr = pm[:, 0:512]
    xk = pm[:, 512:1024]
    xv = pm[:, 1024:1536]
    xwa = pm[:, 1536:1664]
    xg = pm[:, 1664:1792]
    lane = lax.broadcasted_iota(jnp.int32, (1, LANES), 1)
    wa_in = jnp.where(lane < DECAY_LORA, jnp.tanh(xwa), xwa)
    pre = jnp.dot(wa_in.astype(BF16), wcomb_ref[...], preferred_element_type=F32)
    w_pre = pre[:, 0:512] + w0_ref[...]
    a_pre = pre[:, 512:1024] + a0_ref[...]
    w_log = -jnp.log(1.0 + jnp.exp(-w_pre)) - 0.5
    d_o[...] = jnp.exp(-jnp.exp(w_log)).reshape(d_o.shape)
    a_gate = 1.0 / (1.0 + jnp.exp(-a_pre))
    sg = 1.0 / (1.0 + jnp.exp(-xg))
    g_o[...] = jnp.dot(sg.astype(BF16), gup_ref[...], preferred_element_type=F32)
    kkv = xk * kk_ref[...]
    n2 = _seg_sum(kkv * kkv, j512)
    kkn = kkv * lax.rsqrt(jnp.maximum(n2, 1e-24))
    k_h = xk * (1.0 + (a_gate - 1.0) * ka_ref[...])
    r_o[...] = xr.reshape(r_o.shape)
    kh_o[...] = k_h.reshape(kh_o.shape)
    vh_o[...] = xv.reshape(vh_o.shape)
    a_o[...] = (-kkn).reshape(a_o.shape)
    b_o[...] = (kkn * a_gate).reshape(b_o.shape)
    bon_o[...] = _seg_sum(xr * k_h * rk_ref[...], j512) * xv


def _pre_call(x2, pprev, consts, *, tm, seq_len):
    n = x2.shape[0]
    tm = min(tm, n)
    n_tiles = n // tm
    per_seq = seq_len >= tm
    tiles_per_seq = seq_len // tm if per_seq else None
    n_seq = n // seq_len
    full = lambda a: pl.BlockSpec(a.shape, lambda i: (0,) * a.ndim)
    tok = lambda w: pl.BlockSpec((tm, w), lambda i: (i, 0))
    if per_seq:
        pprev_spec = pl.BlockSpec((1, 1, RWKV_PROJ), lambda i: (i // tiles_per_seq, 0, 0))
        h_spec = pl.BlockSpec((1, 1, D_MODEL), lambda i: (i // tiles_per_seq, 0, 0))
        h_shape = jax.ShapeDtypeStruct((n_seq, 1, D_MODEL), F32)
    else:
        pprev_spec = tok(RWKV_PROJ)
        h_spec = tok(D_MODEL)
        h_shape = jax.ShapeDtypeStruct((n, D_MODEL), F32)
    widths = [ATTN_WIDTH, KV_WIDTH, KV_WIDTH] + [RWKV_WIDTH] * 8
    out_shape = [jax.ShapeDtypeStruct((n, w), F32) for w in widths] + [h_shape]
    out_specs = [tok(w) for w in widths] + [h_spec]
    if per_seq and _time_tiled(seq_len):
        for idx in range(3, 9):
            out_shape[idx] = jax.ShapeDtypeStruct((seq_len // SUBLANES, n_seq, SUBLANES, RWKV_WIDTH), F32)
            out_specs[idx] = pl.BlockSpec((tm // SUBLANES, 1, SUBLANES, RWKV_WIDTH),
                                          lambda i: (i % tiles_per_seq, i // tiles_per_seq, 0, 0))
    body = functools.partial(_pre_body, tm=tm, tiles_per_seq=tiles_per_seq, seq_len=seq_len)
    return pl.pallas_call(
        body,
        grid=(n_tiles,),
        in_specs=[tok(D_MODEL), pprev_spec] + [full(c) for c in consts],
        out_specs=out_specs,
        out_shape=out_shape,
        scratch_shapes=[pltpu.VMEM((1, RWKV_PROJ), F32)],
        compiler_params=pltpu.CompilerParams(dimension_semantics=("arbitrary",), vmem_limit_bytes=VMEM_LIMIT),
        name="pre",
    )(x2, pprev, *consts)


def _matmul_body(x_ref, w_ref, o_ref):
    o_ref[...] = jnp.dot(x_ref[...].astype(BF16), w_ref[...], preferred_element_type=F32)


def _matmul_call(x, w):
    return pl.pallas_call(_matmul_body, out_shape=jax.ShapeDtypeStruct((x.shape[0], w.shape[1]), F32),
                          name="shift_proj")(x, w)


def _softmax_sink(s, valid, sink):
    s = jnp.where(valid, s, -jnp.inf)
    m = jnp.maximum(jnp.max(s, axis=-1, keepdims=True), sink)
    p = jnp.exp(s - m)
    den = jnp.sum(p, axis=-1, keepdims=True) + jnp.exp(sink - m)
    return p / den


def _attn_prompt_body(sink_ref, q_ref, kc_ref, kp_ref, vc_ref, vp_ref, o_ref, *, nq):
    for sub in range(nq):
        rows = slice(sub * WINDOW, (sub + 1) * WINDOW)
        prev = slice((sub - 1) * WINDOW, sub * WINDOW)
        k_prev, v_prev = (kp_ref[0], vp_ref[0]) if sub == 0 else (kc_ref[0, prev], vc_ref[0, prev])
        first = (pl.program_id(1) == 0) if sub == 0 else None
        _attn_block(sink_ref, q_ref[0, rows], k_prev, kc_ref[0, rows], v_prev, vc_ref[0, rows], first,
                    o_ref.at[0, rows])


def _attn_block(sink_ref, q, k_prev, k_cur, v_prev, v_cur, first, o_ref):
    kcat = jnp.concatenate([k_prev, k_cur], axis=0)
    vcat = jnp.concatenate([v_prev, v_cur], axis=0)
    krot = pltpu.roll(kcat, shift=HEAD_DIM, axis=1)
    vrot = pltpu.roll(vcat, shift=HEAD_DIM, axis=1)
    lane = lax.broadcasted_iota(jnp.int32, (1, LANES), 1)
    lo = lane < HEAD_DIM
    row = lax.broadcasted_iota(jnp.int32, (WINDOW, WINDOW), 0)
    col = lax.broadcasted_iota(jnp.int32, (WINDOW, WINDOW), 1)
    upper = col > row
    no_prev = None if first is None else upper & first
    for hk in range(N_KV_HEADS):
        src_lo, src_hi = (kcat, krot) if hk == 0 else (krot, kcat)
        k_pads = (jnp.where(lo, src_lo, 0.0).astype(BF16), jnp.where(lo, 0.0, src_hi).astype(BF16))
        vsrc_lo, vsrc_hi = (vcat, vrot) if hk == 0 else (vrot, vcat)
        ones = jnp.ones((2 * WINDOW, LANES), BF16)
        v_pads = tuple(jnp.concatenate([vp.astype(BF16), ones], axis=1)
                       for vp in (jnp.where(lo, vsrc_lo, 0.0), jnp.where(lo, 0.0, vsrc_hi)))
        for pr in range(2):
            pair = hk * 2 + pr
            lanes = slice(pair * LANES, (pair + 1) * LANES)
            qp = (q[:, lanes] * SCALE).astype(BF16)
            out = None
            for e in range(2):
                s = lax.dot_general(qp, k_pads[e], (((1,), (1,)), ((), ())),
                                    preferred_element_type=F32)
                s = jnp.where(upper, s[:, 0:WINDOW], s[:, WINDOW:2 * WINDOW])
                if no_prev is not None:
                    s = jnp.where(no_prev, -jnp.inf, s)
                sink = sink_ref[pair * 2 + e]
                m = jnp.maximum(jnp.max(s, axis=-1, keepdims=True), sink)
                p = jnp.exp(s - m)
                p2 = jnp.concatenate([jnp.where(upper, p, 0.0), jnp.where(upper, 0.0, p)], axis=1)
                pv = jnp.dot(p2.astype(BF16), v_pads[e], preferred_element_type=F32)
                scaled = pv[:, 0:LANES] / (pv[:, LANES:2 * LANES] + jnp.exp(sink - m))
                out = scaled if out is None else out + scaled
            o_ref[:, lanes] = out


def _attn_prompt_call(sinks, q, k, v, *, nq=4):
    b, t, _ = q.shape
    nb = t // WINDOW
    nq = min(nq, nb)
    assert nb % nq == 0
    cur = lambda w: pl.BlockSpec((1, nq * WINDOW, w), lambda i, j: (i, j, 0))
    prev = lambda w: pl.BlockSpec((1, WINDOW, w), lambda i, j: (i, jnp.maximum(nq * j - 1, 0), 0))
    return pl.pallas_call(
        functools.partial(_attn_prompt_body, nq=nq),
        grid=(b, nb // nq),
        in_specs=[pl.BlockSpec(memory_space=pltpu.SMEM), cur(ATTN_WIDTH), cur(KV_WIDTH), prev(KV_WIDTH),
                  cur(KV_WIDTH), prev(KV_WIDTH)],
        out_specs=cur(ATTN_WIDTH),
        out_shape=jax.ShapeDtypeStruct((b, t, ATTN_WIDTH), F32),
        compiler_params=pltpu.CompilerParams(dimension_semantics=("arbitrary", "arbitrary")),
        name="attn_prompt",
    )(sinks, q, k, k, v, v)


def _attn_sample_body(sink_ref, q_ref, k_ref, v_ref, o_ref, *, bb, rows, keys, dec_seq):
    r = lax.broadcasted_iota(jnp.int32, (rows, keys), 0)
    c = lax.broadcasted_iota(jnp.int32, (rows, keys), 1)
    t = r % dec_seq
    valid = (c > t) & (c <= t + WINDOW)
    sink = sink_ref[...]

    def one(i, _):
        q = q_ref[i].astype(BF16)
        k = k_ref[i].astype(BF16)
        s = lax.dot_general(q, k, (((1,), (1,)), ((), ())), preferred_element_type=F32) * SCALE
        probs = _softmax_sink(s, valid, sink)
        o_ref[i] = jnp.dot(probs.astype(BF16), v_ref[i].astype(BF16), preferred_element_type=F32)
        return 0

    lax.fori_loop(0, bb, one, 0, unroll=8)


def _attn_sample_call(sink_rows, q, kall, vall, *, dec_seq, bb=16):
    bsz, rows, _ = q.shape
    keys = kall.shape[1]
    blk = lambda r: pl.BlockSpec((bb, r, KV_WIDTH), lambda i: (i, 0, 0))
    body = functools.partial(_attn_sample_body, bb=bb, rows=rows, keys=keys, dec_seq=dec_seq)
    return pl.pallas_call(
        body,
        grid=(bsz // bb,),
        in_specs=[pl.BlockSpec((rows, 1), lambda i: (0, 0)), blk(rows), blk(keys), blk(keys)],
        out_specs=blk(rows),
        out_shape=jax.ShapeDtypeStruct((bsz, rows, KV_WIDTH), F32),
        compiler_params=pltpu.CompilerParams(dimension_semantics=("arbitrary",)),
        name="attn_sample",
    )(sink_rows, q, kall, vall)


IB_GROUP = 4
RELAYOUT_GATES = ((0, 8), (0, 24), (0, 40), (0, 56), (1, 24), (1, 40))


BATCH_GROUP = LANES // N_RWKV_HEADS
HEAD_PAIRS = N_RWKV_HEADS // 2


def _to_chains(x):
    m = jnp.concatenate([x[:, hp * LANES:(hp + 1) * LANES] for hp in range(HEAD_PAIRS)], axis=0)
    mt = m.T
    return jnp.concatenate([mt[0:HEAD_DIM, :], mt[HEAD_DIM:2 * HEAD_DIM, :]], axis=1)


def _from_chains(y):
    m = jnp.concatenate([y[:, 0:HEAD_DIM], y[:, HEAD_DIM:2 * HEAD_DIM]], axis=0)
    mt = m.T
    return jnp.concatenate([mt[hp * BATCH_GROUP:(hp + 1) * BATCH_GROUP, :] for hp in range(HEAD_PAIRS)], axis=1)


def _zero_like_dep(x):
    u = lax.bitcast_convert_type(x, jnp.uint32)
    z = lax.shift_right_logical(lax.shift_right_logical(u, jnp.uint32(16)), jnp.uint32(16))
    return lax.bitcast_convert_type(z, F32)


def _after(x, zero):
    return x + zero


def _scan_body(*refs, tb, n_tb, zero_init, tiled):
    n_in = 6 if zero_init else 7
    raw = refs[:6]
    s0_ref = None if zero_init else refs[6]
    y_out, st_ref = refs[n_in:n_in + 2]
    scratch = refs[n_in + 2:]
    slots = (scratch[0:6], scratch[6:12])
    state, sa, g_run = scratch[12:]
    s = pl.program_id(1)
    n_ib = HEAD_DIM // SUBLANES

    def rows_at(t):
        if tiled:
            return (t // SUBLANES, slice(None), t % SUBLANES, slice(None))
        return (slice(None), t, slice(None))

    def stage(t, wr):
        c_r, c_d, c_k, c_v, c_a, c_b = (_to_chains(src[rows_at(t)]) for src in raw)
        g_prev = g_run[...]
        g = g_prev * c_d
        g_inv = 1.0 / g
        g_run[...] = g
        staged = (c_r * g, c_k * g_inv, c_v, c_a * g_prev, c_b * g_inv)
        for dst, val in zip(wr[:5], staged):
            dst[t] = val
        return staged

    @pl.when(s == 0)
    def _():
        if zero_init:
            state[...] = jnp.zeros(state.shape, F32)
        else:
            def load_i(i, _):
                state[:, i, :] = _to_chains(s0_ref[:, i, :])
                return 0

            lax.fori_loop(0, HEAD_DIM, load_i, 0, unroll=8)
        slots[1][5][...] = jnp.ones((HEAD_DIM, LANES), F32)
        g_run[...] = jnp.ones((HEAD_DIM, LANES), F32)

        def stage_only(t, _):
            stage(t, slots[0])
            return 0

        lax.fori_loop(0, tb, stage_only, 0, unroll=min(tb, 2))
        slots[0][5][...] = g_run[...]

    def run_block(rd, wr):
        r_ref, k_ref, v_ref, a_ref, b_ref, _ = rd
        wr_g = wr[5]
        n_grp = n_ib // IB_GROUP
        assert n_grp == 2

        for grp in range(n_grp):
            rows = [pl.ds((grp * IB_GROUP + u) * SUBLANES, SUBLANES) for u in range(IB_GROUP)]

            def init_j(j, accs, rows=rows):
                jr = pl.ds(j, 1)
                gj, aj = wr_g[jr, :], a_ref[0, jr, :]
                out = []
                for u in range(IB_GROUP):
                    s_full = state[j, rows[u], :] * gj
                    state[j, rows[u], :] = s_full
                    out.append(accs[u] + s_full * aj)
                return tuple(out)

            zero = tuple(jnp.zeros((SUBLANES, LANES), F32) for _ in range(IB_GROUP))
            accs = lax.fori_loop(0, HEAD_DIM, init_j, zero, unroll=8)
            for u in range(IB_GROUP):
                sa[rows[u], :] = accs[u]
        g_run[...] = jnp.ones((HEAD_DIM, LANES), F32)

        def step(t, y_prev):
            y_tok = _from_chains(y_prev)
            y_out[rows_at(jnp.maximum(t - 1, 0))] = y_tok
            staged = stage(t, wr)
            gates = {pos: _zero_like_dep(val[0:1, 0:LANES]) for pos, val in zip(RELAYOUT_GATES, (y_tok,) + staged)}
            y_new = []
            tn = jnp.minimum(t + 1, tb - 1)
            for grp in range(n_ib // IB_GROUP):
                rows = [pl.ds((grp * IB_GROUP + u) * SUBLANES, SUBLANES) for u in range(IB_GROUP)]
                sav = [sa[rw, :] for rw in rows]
                vv = [v_ref[t, rw, :] for rw in rows]
                yacc = [jnp.zeros((SUBLANES, LANES), F32) for _ in range(IB_GROUP)]
                sacc = [jnp.zeros((SUBLANES, LANES), F32) for _ in range(IB_GROUP)]
                for j in range(HEAD_DIM):
                    jr = pl.ds(j, 1)
                    bj = b_ref[t, jr, :]
                    if (grp, j) in gates:
                        bj = _after(bj, gates[(grp, j)])
                    kj = k_ref[t, jr, :]
                    rj = r_ref[t, jr, :]
                    aj = a_ref[tn, jr, :]
                    for u in range(IB_GROUP):
                        s_new = state[j, rows[u], :] + sav[u] * bj + vv[u] * kj
                        state[j, rows[u], :] = s_new
                        yacc[u] = yacc[u] + s_new * rj
                        sacc[u] = sacc[u] + s_new * aj
                for u in range(IB_GROUP):
                    sa[rows[u], :] = sacc[u]
                y_new.extend(yacc)
            return jnp.concatenate(y_new, axis=0)

        y_last = lax.fori_loop(0, tb, step, jnp.zeros((HEAD_DIM, LANES), F32))
        y_out[rows_at(tb - 1)] = _from_chains(y_last)
        wr_g[...] = g_run[...]

    @pl.when((s > 0) & (s % 2 == 0))
    def _():
        run_block(slots[1], slots[0])

    @pl.when(s % 2 == 1)
    def _():
        run_block(slots[0], slots[1])

    @pl.when(s == n_tb)
    def _():
        g_last = slots[(n_tb - 1) % 2][5][...]

        def store_i(i, _):
            st_ref[:, i, :] = _from_chains(state[:, i, :] * g_last)
            return 0

        lax.fori_loop(0, HEAD_DIM, store_i, 0, unroll=8)


def _scan_call(r, d, k, v, a, b, s0, *, tb):
    tiled = r.ndim == 4
    bsz, t_len = (r.shape[1], r.shape[0] * SUBLANES) if tiled else (r.shape[0], r.shape[1])
    bg = BATCH_GROUP
    n_tb = t_len // tb
    if tiled:
        blk = (tb // SUBLANES, bg, SUBLANES, RWKV_WIDTH)
        op_in = pl.BlockSpec(blk, lambda g, s: (jnp.minimum(s, n_tb - 1), g, 0, 0))
        op_out = pl.BlockSpec(blk, lambda g, s: (jnp.maximum(s - 1, 0), g, 0, 0))
    else:
        op_in = pl.BlockSpec((bg, tb, RWKV_WIDTH), lambda g, s: (g, jnp.minimum(s, n_tb - 1), 0))
        op_out = pl.BlockSpec((bg, tb, RWKV_WIDTH), lambda g, s: (g, jnp.maximum(s - 1, 0), 0))
    st = pl.BlockSpec((bg, HEAD_DIM, RWKV_WIDTH), lambda g, s: (g, 0, 0))
    zero_init = s0 is None
    body = functools.partial(_scan_body, tb=tb, n_tb=n_tb, zero_init=zero_init, tiled=tiled)
    return pl.pallas_call(
        body,
        grid=(bsz // bg, n_tb + 1),
        in_specs=[op_in] * 6 + ([] if zero_init else [st]),
        out_specs=[op_out, st],
        out_shape=[jax.ShapeDtypeStruct(r.shape, F32),
                   jax.ShapeDtypeStruct((bsz, HEAD_DIM, RWKV_WIDTH), F32)],
        scratch_shapes=([pltpu.VMEM((tb, HEAD_DIM, LANES), F32)] * 5 + [pltpu.VMEM((HEAD_DIM, LANES), F32)]) * 2
        + [pltpu.VMEM((HEAD_DIM, HEAD_DIM, LANES), F32), pltpu.VMEM((HEAD_DIM, LANES), F32),
           pltpu.VMEM((HEAD_DIM, LANES), F32)],
        compiler_params=pltpu.CompilerParams(dimension_semantics=("arbitrary", "arbitrary"),
                                             vmem_limit_bytes=VMEM_LIMIT),
        name="wkv_scan",
    )(r, d, k, v, a, b, *([] if zero_init else [s0]))


def _post_body(x_ref, at_ref, y_ref, g_ref, bon_ref, j512_ref, lng_ref, lnb_ref, wout_ref, g2_ref, wup_ref,
               wdn_ref, o_ref, *, ff_chunk):
    j512 = j512_ref[...]
    y = y_ref[...].reshape(g_ref.shape)
    mu = _seg_sum(y, j512) * (1.0 / HEAD_DIM)
    yc = y - mu
    var = _seg_sum(yc * yc, j512) * (1.0 / HEAD_DIM)
    yn = yc * lax.rsqrt(var + GN_EPS) * lng_ref[...] + lnb_ref[...]
    rw = (yn + bon_ref[...]) * g_ref[...]
    x1 = (x_ref[...]
          + jnp.dot(at_ref[...].astype(BF16), wout_ref[0:ATTN_WIDTH, :], preferred_element_type=F32)
          + jnp.dot(rw.astype(BF16), wout_ref[ATTN_WIDTH:D_MODEL, :], preferred_element_type=F32))
    ms = jnp.mean(x1 * x1, axis=-1, keepdims=True)
    h2 = (x1 * lax.rsqrt(ms + RMS_EPS) * g2_ref[...]).astype(BF16)
    o_ref[...] = x1
    for c in range(D_FF // ff_chunk):
        u = jnp.dot(h2, wup_ref[:, c * ff_chunk:(c + 1) * ff_chunk], preferred_element_type=F32)
        u = jnp.maximum(u, 0.0)
        o_ref[...] += jnp.dot((u * u).astype(BF16), wdn_ref[c * ff_chunk:(c + 1) * ff_chunk, :],
                              preferred_element_type=F32)


def _post_call(x2, attn, y, g, bon, consts, *, tm, ff_chunk=1024):
    n = x2.shape[0]
    tm = min(tm, n)
    tok = lambda w: pl.BlockSpec((tm, w), lambda i: (i, 0))
    full = lambda a: pl.BlockSpec(a.shape, lambda i: (0,) * a.ndim, pipeline_mode=pl.Buffered(1))
    if y.ndim == 4:
        tiles_per_seq = y.shape[0] * SUBLANES // tm
        y_spec = pl.BlockSpec((tm // SUBLANES, 1, SUBLANES, RWKV_WIDTH),
                              lambda i: (i % tiles_per_seq, i // tiles_per_seq, 0, 0))
    else:
        y_spec = tok(RWKV_WIDTH)
    return pl.pallas_call(
        functools.partial(_post_body, ff_chunk=ff_chunk),
        grid=(n // tm,),
        in_specs=[tok(D_MODEL), tok(ATTN_WIDTH), y_spec, tok(RWKV_WIDTH), tok(RWKV_WIDTH)]
        + [full(c) for c in consts],
        out_specs=tok(D_MODEL),
        out_shape=jax.ShapeDtypeStruct((n, D_MODEL), F32),
        compiler_params=pltpu.CompilerParams(dimension_semantics=("arbitrary",), vmem_limit_bytes=VMEM_LIMIT),
        name="post",
    )(x2, attn, y, g, bon, *consts)


def _layer(x, p_prev_rows, s0, attn_fn, w, *, tm_pre, tm_post, scan_tb):
    b, t, _ = x.shape
    x2 = x.reshape(b * t, D_MODEL)
    outs = _pre_call(x2, p_prev_rows, w["pre"], tm=tm_pre, seq_len=t)
    q, k, v, r, d, kh, vh, av, bv, g, bon, h_last = outs
    attn, new_k, new_v = attn_fn(q, k, v)
    ops = [a if a.ndim == 4 else a.reshape(b, t, RWKV_WIDTH) for a in (r, d, kh, vh, av, bv)]
    s0r = None if s0 is None else s0.astype(F32).transpose(0, 2, 1, 3).reshape(b, HEAD_DIM, RWKV_WIDTH)
    y, s_fin = _scan_call(*ops, s0r, tb=scan_tb)
    if y.ndim == 3:
        y = y.reshape(b * t, RWKV_WIDTH)
    s_new = s_fin.reshape(b, HEAD_DIM, N_RWKV_HEADS, HEAD_DIM).transpose(0, 2, 1, 3)
    out = _post_call(x2, attn, y, g, bon, w["post"], tm=tm_post)
    return out.reshape(b, t, D_MODEL), new_k, new_v, s_new, h_last


def _prep_weights(norm1_g, w_in, q_norm_g, k_norm_g, rwkv_mu, w_decay_0, w_decay_up, a_0, a_up, g_up, k_k, k_a,
                  r_k, ln_x_g, ln_x_b, w_out, norm2_g, w_ff_up, w_ff_down):
    o = RWKV_OFF
    pieces = ((0, o + 512), (o + 576, o + 1088), (o + 1088, o + 1600), (o + 512, o + 576), (o + 1600, o + 1664),
              (o + 1664, o + 1792))
    w_in_p = jnp.concatenate([w_in[:, lo:hi] for lo, hi in pieces], axis=1).astype(BF16)
    mu_p = jnp.concatenate([rwkv_mu[max(lo - o, 0):hi - o] for lo, hi in pieces]).reshape(1, RWKV_PROJ)
    row = lambda a: a.reshape(1, -1).astype(F32)
    eye = lambda n: jnp.kron(jnp.eye(n, dtype=F32), jnp.ones((HEAD_DIM, HEAD_DIM), F32)).astype(BF16)
    j512, j128 = eye(N_RWKV_HEADS), eye(N_KV_HEADS)
    zeros = jnp.zeros((DECAY_LORA, RWKV_WIDTH), F32)
    wcomb = jnp.concatenate([jnp.concatenate([w_decay_up, zeros], axis=1),
                             jnp.concatenate([zeros, a_up], axis=1)], axis=0).astype(BF16)
    pre = [row(norm1_g), w_in_p, mu_p, row(jnp.tile(q_norm_g, N_Q_HEADS)), row(jnp.tile(k_norm_g, N_KV_HEADS)),
           j512, j128, wcomb, row(w_decay_0), row(a_0), g_up.astype(BF16), row(k_k), row(k_a), row(r_k)]
    post = [j512, row(ln_x_g), row(ln_x_b), w_out.astype(BF16), row(norm2_g), w_ff_up.astype(BF16),
            w_ff_down.astype(BF16)]
    return {"pre": pre, "post": post, "w_rwkv": w_in_p[:, RWKV_OFF:]}


def _forward(x_prompt, x_sample, cache_k, cache_v, state_wkv, state_shift, norm1_g, w_in, q_norm_g, k_norm_g,
             attn_sinks, rwkv_mu, w_decay_0, w_decay_up, a_0, a_up, g_up, k_k, k_a, r_k, ln_x_g, ln_x_b, w_out,
             norm2_g, w_ff_up, w_ff_down, *, tm_pre=512, tm_post=512, scan_tb=32):
    depth = norm1_g.shape[0]
    assert depth == 1
    l = 0
    w = _prep_weights(norm1_g[l], w_in[l], q_norm_g[l], k_norm_g[l], rwkv_mu[l], w_decay_0[l], w_decay_up[l],
                      a_0[l], a_up[l], g_up[l], k_k[l], k_a[l], r_k[l], ln_x_g[l], ln_x_b[l], w_out[l], norm2_g[l],
                      w_ff_up[l], w_ff_down[l])
    sinks = attn_sinks[l].astype(F32)
    bp, tp, _ = x_prompt.shape
    bs, ts, _ = x_sample.shape

    def attn_prompt(q, k, v):
        q3, k3, v3 = (a.reshape(bp, tp, -1) for a in (q, k, v))
        out = _attn_prompt_call(sinks, q3, k3, v3).reshape(bp * tp, ATTN_WIDTH)
        tail = lambda a: a[:, tp - WINDOW:].reshape(bp, WINDOW, N_KV_HEADS, HEAD_DIM)
        return out, tail(k3), tail(v3)

    pprev_p = jnp.zeros((bp, 1, RWKV_PROJ), F32)
    yp, k1, v1, w1, s1 = _layer(x_prompt, pprev_p, None, attn_prompt, w, tm_pre=tm_pre, tm_post=tm_post,
                                scan_tb=scan_tb)
    s1 = s1.reshape(bp, D_MODEL)

    ck, cv = cache_k[l].astype(F32), cache_v[l].astype(F32)
    group = N_Q_HEADS // N_KV_HEADS
    rows = group * ts
    keys = WINDOW + ts

    def attn_sample(q, k, v):
        k4 = k.reshape(bs, ts, N_KV_HEADS, HEAD_DIM)
        v4 = v.reshape(bs, ts, N_KV_HEADS, HEAD_DIM)
        kall = jnp.concatenate([ck, k4], axis=1)
        vall = jnp.concatenate([cv, v4], axis=1)
        dense = lambda a: a.reshape(bs, keys, KV_WIDTH)
        q4 = q.reshape(bs, ts, N_KV_HEADS, group, HEAD_DIM).transpose(0, 2, 3, 1, 4).reshape(
            bs, N_KV_HEADS, rows, HEAD_DIM)
        eye = jnp.eye(N_KV_HEADS, dtype=F32)
        qbd = (q4[:, :, :, None, :] * eye[None, :, None, :, None]).reshape(bs, N_KV_HEADS * rows, KV_WIDTH)
        sink_rows = jnp.repeat(sinks, ts)[:, None]
        o = _attn_sample_call(sink_rows, qbd, dense(kall), dense(vall), dec_seq=ts)
        o = o.reshape(bs, N_KV_HEADS, rows, N_KV_HEADS, HEAD_DIM)
        o = jnp.stack([o[:, hk, :, hk, :] for hk in range(N_KV_HEADS)], axis=1)
        o = o.reshape(bs, N_KV_HEADS, group, ts, HEAD_DIM).transpose(0, 3, 1, 2, 4)
        return o.reshape(bs * ts, ATTN_WIDTH), kall[:, ts:], vall[:, ts:]

    p_prev = _matmul_call(state_shift[l], w["w_rwkv"])
    pprev_s = jnp.repeat(p_prev, ts, axis=0)
    ys, k2, v2, w2, h_all = _layer(x_sample, pprev_s, state_wkv[l], attn_sample, w, tm_pre=tm_pre,
                                   tm_post=tm_post, scan_tb=ts)
    s2 = h_all.reshape(bs, ts, D_MODEL)[:, -1]
    st = lambda a: a[None]
    return (yp, ys, st(k1), st(v1), st(w1), st(s1), st(k2), st(v2), st(w2), st(s2))


def kernel(x_prompt, x_sample, cache_k, cache_v, state_wkv, state_shift, norm1_g, w_in, q_norm_g, k_norm_g, attn_sinks, rwkv_mu, w_decay_0, w_decay_up, a_0, a_up, g_up, k_k, k_a, r_k, ln_x_g, ln_x_b, w_out, norm2_g, w_ff_up, w_ff_down):
    return _forward(x_prompt, x_sample, cache_k, cache_v, state_wkv, state_shift, norm1_g, w_in, q_norm_g, k_norm_g,
                    attn_sinks, rwkv_mu, w_decay_0, w_decay_up, a_0, a_up, g_up, k_k, k_a, r_k, ln_x_g, ln_x_b,
                    w_out, norm2_g, w_ff_up, w_ff_down)
```

```python
import functools

import jax
import jax.numpy as jnp
from jax import lax
from jax.experimental import pallas as pl
from jax.experimental.pallas import tpu as pltpu

D_MODEL = 1024
HEAD_DIM = 64
ATTN_WIDTH = 512
N_Q_HEADS = 8
N_KV_HEADS = 2
KV_WIDTH = 128
RWKV_WIDTH = 512
N_RWKV_HEADS = 8
WINDOW = 128
SCALE = HEAD_DIM ** -0.5
DECAY_LORA = 64
AAA_LORA = 64
GATE_LORA = 128
D_FF = 4 * D_MODEL
RMS_EPS = 1e-6
GN_EPS = 64e-5
RWKV_OFF = ATTN_WIDTH + 2 * KV_WIDTH
RWKV_PROJ = 3 * RWKV_WIDTH + DECAY_LORA + AAA_LORA + GATE_LORA
IN_WIDTH = RWKV_OFF + RWKV_PROJ
LANES = 128
SUBLANES = 8
VMEM_LIMIT = 56 * 1024 * 1024

F32 = jnp.float32
BF16 = jnp.bfloat16


def _seg_sum(x, ones_blk):
    return jnp.dot(x.astype(BF16), ones_blk, preferred_element_type=F32)


def _time_tiled(seq_len):
    return seq_len % SUBLANES == 0


def _pre_body(x_ref, pprev_ref, g1_ref, win_ref, mu_ref, qg_ref, kg_ref, j512_ref, j128_ref,
              wcomb_ref, w0_ref, a0_ref, gup_ref, kk_ref, ka_ref, rk_ref,
              q_o, k_o, v_o, r_o, d_o, kh_o, vh_o, a_o, b_o, g_o, bon_o, h_o,
              carry, *, tm, tiles_per_seq, seq_len):
    x = x_ref[...]
    ms = jnp.mean(x * x, axis=-1, keepdims=True)
    h = x * lax.rsqrt(ms + RMS_EPS) * g1_ref[...]
    proj = jnp.dot(h.astype(BF16), win_ref[...], preferred_element_type=F32)

    j512 = j512_ref[...]
    q = proj[:, 0:ATTN_WIDTH]
    qs = _seg_sum(q * q, j512) * (1.0 / HEAD_DIM)
    q_o[...] = q * lax.rsqrt(qs + RMS_EPS) * qg_ref[...]
    k = proj[:, ATTN_WIDTH:ATTN_WIDTH + KV_WIDTH]
    ks = _seg_sum(k * k, j128_ref[...]) * (1.0 / HEAD_DIM)
    k_o[...] = k * lax.rsqrt(ks + RMS_EPS) * kg_ref[...]
    v_o[...] = proj[:, ATTN_WIDTH + KV_WIDTH:RWKV_OFF]

    p = proj[:, RWKV_OFF:]
    rolled = pltpu.roll(p, shift=1, axis=0)
    row = lax.broadcasted_iota(jnp.int32, (tm, 1), 0)
    if tiles_per_seq is not None:
        t = pl.program_id(0) % tiles_per_seq
        first = jnp.where(t == 0, pprev_ref[0], carry[...])
        p_shift = jnp.where(row == 0, first, rolled)
        carry[...] = p[tm - 1:tm, :]

        @pl.when(t == tiles_per_seq - 1)
        def _():
            h_o[0] = h[tm - 1:tm, :]
    else:
        p_shift = jnp.where(row % seq_len == 0, pprev_ref[...], rolled)
        h_o[...] = h
    pm = p + (p_shift - p) * mu_ref[...]

    xr = pm[:, 0:512]
    xk = pm[:, 512:1024]
    xv = pm[:, 1024:1536]
    xwa = pm[:, 1536:1664]
    xg = pm[:, 1664:1792]
    lane = lax.broadcasted_iota(jnp.int32, (1, LANES), 1)
    wa_in = jnp.where(lane < DECAY_LORA, jnp.tanh(xwa), xwa)
    pre = jnp.dot(wa_in.astype(BF16), wcomb_ref[...], preferred_element_type=F32)
    w_pre = pre[:, 0:512] + w0_ref[...]
    a_pre = pre[:, 512:1024] + a0_ref[...]
    w_log = -jnp.log(1.0 + jnp.exp(-w_pre)) - 0.5
    d_o[...] = jnp.exp(-jnp.exp(w_log)).reshape(d_o.shape)
    a_gate = 1.0 / (1.0 + jnp.exp(-a_pre))
    sg = 1.0 / (1.0 + jnp.exp(-xg))
    g_o[...] = jnp.dot(sg.astype(BF16), gup_ref[...], preferred_element_type=F32)
    kkv = xk * kk_ref[...]
    n2 = _seg_sum(kkv * kkv, j512)
    kkn = kkv * lax.rsqrt(jnp.maximum(n2, 1e-24))
    k_h = xk * (1.0 + (a_gate - 1.0) * ka_ref[...])
    r_o[...] = xr.reshape(r_o.shape)
    kh_o[...] = k_h.reshape(kh_o.shape)
    vh_o[...] = xv.reshape(vh_o.shape)
    a_o[...] = (-kkn).reshape(a_o.shape)
    b_o[...] = (kkn * a_gate).reshape(b_o.shape)
    bon_o[...] = _seg_sum(xr * k_h * rk_ref[...], j512) * xv


def _pre_call(x2, pprev, consts, *, tm, seq_len):
    n = x2.shape[0]
    tm = min(tm, n)
    n_tiles = n // tm
    per_seq = seq_len >= tm
    tiles_per_seq = seq_len // tm if per_seq else None
    n_seq = n // seq_len
    full = lambda a: pl.BlockSpec(a.shape, lambda i: (0,) * a.ndim)
    tok = lambda w: pl.BlockSpec((tm, w), lambda i: (i, 0))
    if per_seq:
        pprev_spec = pl.BlockSpec((1, 1, RWKV_PROJ), lambda i: (i // tiles_per_seq, 0, 0))
        h_spec = pl.BlockSpec((1, 1, D_MODEL), lambda i: (i // tiles_per_seq, 0, 0))
        h_shape = jax.ShapeDtypeStruct((n_seq, 1, D_MODEL), F32)
    else:
        pprev_spec = tok(RWKV_PROJ)
        h_spec = tok(D_MODEL)
        h_shape = jax.ShapeDtypeStruct((n, D_MODEL), F32)
    widths = [ATTN_WIDTH, KV_WIDTH, KV_WIDTH] + [RWKV_WIDTH] * 8
    out_shape = [jax.ShapeDtypeStruct((n, w), F32) for w in widths] + [h_shape]
    out_specs = [tok(w) for w in widths] + [h_spec]
    if per_seq and _time_tiled(seq_len):
        for idx in range(3, 9):
            out_shape[idx] = jax.ShapeDtypeStruct((seq_len // SUBLANES, n_seq, SUBLANES, RWKV_WIDTH), F32)
            out_specs[idx] = pl.BlockSpec((tm // SUBLANES, 1, SUBLANES, RWKV_WIDTH),
                                          lambda i: (i % tiles_per_seq, i // tiles_per_seq, 0, 0))
    body = functools.partial(_pre_body, tm=tm, tiles_per_seq=tiles_per_seq, seq_len=seq_len)
    return pl.pallas_call(
        body,
        grid=(n_tiles,),
        in_specs=[tok(D_MODEL), pprev_spec] + [full(c) for c in consts],
        out_specs=out_specs,
        out_shape=out_shape,
        scratch_shapes=[pltpu.VMEM((1, RWKV_PROJ), F32)],
        compiler_params=pltpu.CompilerParams(dimension_semantics=("arbitrary",), vmem_limit_bytes=VMEM_LIMIT),
        name="pre",
    )(x2, pprev, *consts)


def _matmul_body(x_ref, w_ref, o_ref):
    o_ref[...] = jnp.dot(x_ref[...].astype(BF16), w_ref[...], preferred_element_type=F32)


def _matmul_call(x, w):
    return pl.pallas_call(_matmul_body, out_shape=jax.ShapeDtypeStruct((x.shape[0], w.shape[1]), F32),
                          name="shift_proj")(x, w)


def _softmax_sink(s, valid, sink):
    s = jnp.where(valid, s, -jnp.inf)
    m = jnp.maximum(jnp.max(s, axis=-1, keepdims=True), sink)
    p = jnp.exp(s - m)
    den = jnp.sum(p, axis=-1, keepdims=True) + jnp.exp(sink - m)
    return p / den


def _attn_prompt_body(sink_ref, q_ref, kc_ref, kp_ref, vc_ref, vp_ref, o_ref, *, nq):
    for sub in range(nq):
        rows = slice(sub * WINDOW, (sub + 1) * WINDOW)
        prev = slice((sub - 1) * WINDOW, sub * WINDOW)
        k_prev, v_prev = (kp_ref[0], vp_ref[0]) if sub == 0 else (kc_ref[0, prev], vc_ref[0, prev])
        first = (pl.program_id(1) == 0) if sub == 0 else None
        _attn_block(sink_ref, q_ref[0, rows], k_prev, kc_ref[0, rows], v_prev, vc_ref[0, rows], first,
                    o_ref.at[0, rows])


def _attn_block(sink_ref, q, k_prev, k_cur, v_prev, v_cur, first, o_ref):
    kcat = jnp.concatenate([k_prev, k_cur], axis=0)
    vcat = jnp.concatenate([v_prev, v_cur], axis=0)
    krot = pltpu.roll(kcat, shift=HEAD_DIM, axis=1)
    vrot = pltpu.roll(vcat, shift=HEAD_DIM, axis=1)
    lane = lax.broadcasted_iota(jnp.int32, (1, LANES), 1)
    lo = lane < HEAD_DIM
    row = lax.broadcasted_iota(jnp.int32, (WINDOW, WINDOW), 0)
    col = lax.broadcasted_iota(jnp.int32, (WINDOW, WINDOW), 1)
    upper = col > row
    no_prev = None if first is None else upper & first
    for hk in range(N_KV_HEADS):
        src_lo, src_hi = (kcat, krot) if hk == 0 else (krot, kcat)
        k_pads = (jnp.where(lo, src_lo, 0.0).astype(BF16), jnp.where(lo, 0.0, src_hi).astype(BF16))
        vsrc_lo, vsrc_hi = (vcat, vrot) if hk == 0 else (vrot, vcat)
        ones = jnp.ones((2 * WINDOW, LANES), BF16)
        v_pads = tuple(jnp.concatenate([vp.astype(BF16), ones], axis=1)
                       for vp in (jnp.where(lo, vsrc_lo, 0.0), jnp.where(lo, 0.0, vsrc_hi)))
        for pr in range(2):
            pair = hk * 2 + pr
            lanes = slice(pair * LANES, (pair + 1) * LANES)
            qp = (q[:, lanes] * SCALE).astype(BF16)
            out = None
            for e in range(2):
                s = lax.dot_general(qp, k_pads[e], (((1,), (1,)), ((), ())),
                                    preferred_element_type=F32)
                s = jnp.where(upper, s[:, 0:WINDOW], s[:, WINDOW:2 * WINDOW])
                if no_prev is not None:
                    s = jnp.where(no_prev, -jnp.inf, s)
                sink = sink_ref[pair * 2 + e]
                m = jnp.maximum(jnp.max(s, axis=-1, keepdims=True), sink)
                p = jnp.exp(s - m)
                p2 = jnp.concatenate([jnp.where(upper, p, 0.0), jnp.where(upper, 0.0, p)], axis=1)
                pv = jnp.dot(p2.astype(BF16), v_pads[e], preferred_element_type=F32)
                scaled = pv[:, 0:LANES] / (pv[:, LANES:2 * LANES] + jnp.exp(sink - m))
                out = scaled if out is None else out + scaled
            o_ref[:, lanes] = out


def _attn_prompt_call(sinks, q, k, v, *, nq=8):
    b, t, _ = q.shape
    nb = t // WINDOW
    nq = min(nq, nb)
    assert nb % nq == 0
    cur = lambda w: pl.BlockSpec((1, nq * WINDOW, w), lambda i, j: (i, j, 0))
    prev = lambda w: pl.BlockSpec((1, WINDOW, w), lambda i, j: (i, jnp.maximum(nq * j - 1, 0), 0))
    return pl.pallas_call(
        functools.partial(_attn_prompt_body, nq=nq),
        grid=(b, nb // nq),
        in_specs=[pl.BlockSpec(memory_space=pltpu.SMEM), cur(ATTN_WIDTH), cur(KV_WIDTH), prev(KV_WIDTH),
                  cur(KV_WIDTH), prev(KV_WIDTH)],
        out_specs=cur(ATTN_WIDTH),
        out_shape=jax.ShapeDtypeStruct((b, t, ATTN_WIDTH), F32),
        compiler_params=pltpu.CompilerParams(dimension_semantics=("arbitrary", "arbitrary")),
        name="attn_prompt",
    )(sinks, q, k, k, v, v)


def _attn_sample_body(sink_ref, q_ref, k_ref, v_ref, o_ref, *, bb, rows, keys, dec_seq):
    r = lax.broadcasted_iota(jnp.int32, (rows, keys), 0)
    c = lax.broadcasted_iota(jnp.int32, (rows, keys), 1)
    t = r % dec_seq
    valid = (c > t) & (c <= t + WINDOW)
    sink = sink_ref[...]

    def one(i, _):
        q = q_ref[i].astype(BF16)
        k = k_ref[i].astype(BF16)
        s = lax.dot_general(q, k, (((1,), (1,)), ((), ())), preferred_element_type=F32) * SCALE
        probs = _softmax_sink(s, valid, sink)
        o_ref[i] = jnp.dot(probs.astype(BF16), v_ref[i].astype(BF16), preferred_element_type=F32)
        return 0

    lax.fori_loop(0, bb, one, 0, unroll=8)


def _attn_sample_call(sink_rows, q, kall, vall, *, dec_seq, bb=16):
    bsz, rows, _ = q.shape
    keys = kall.shape[1]
    blk = lambda r: pl.BlockSpec((bb, r, KV_WIDTH), lambda i: (i, 0, 0))
    body = functools.partial(_attn_sample_body, bb=bb, rows=rows, keys=keys, dec_seq=dec_seq)
    return pl.pallas_call(
        body,
        grid=(bsz // bb,),
        in_specs=[pl.BlockSpec((rows, 1), lambda i: (0, 0)), blk(rows), blk(keys), blk(keys)],
        out_specs=blk(rows),
        out_shape=jax.ShapeDtypeStruct((bsz, rows, KV_WIDTH), F32),
        compiler_params=pltpu.CompilerParams(dimension_semantics=("arbitrary",)),
        name="attn_sample",
    )(sink_rows, q, kall, vall)


IB_GROUP = 4
RELAYOUT_GATES = ((0, 8), (0, 24), (0, 40), (0, 56), (1, 24), (1, 40))


BATCH_GROUP = LANES // N_RWKV_HEADS
HEAD_PAIRS = N_RWKV_HEADS // 2


def _to_chains(x):
    m = jnp.concatenate([x[:, hp * LANES:(hp + 1) * LANES] for hp in range(HEAD_PAIRS)], axis=0)
    mt = m.T
    return jnp.concatenate([mt[0:HEAD_DIM, :], mt[HEAD_DIM:2 * HEAD_DIM, :]], axis=1)


def _from_chains(y):
    m = jnp.concatenate([y[:, 0:HEAD_DIM], y[:, HEAD_DIM:2 * HEAD_DIM]], axis=0)
    mt = m.T
    return jnp.concatenate([mt[hp * BATCH_GROUP:(hp + 1) * BATCH_GROUP, :] for hp in range(HEAD_PAIRS)], axis=1)


def _zero_like_dep(x):
    u = lax.bitcast_convert_type(x, jnp.uint32)
    z = lax.shift_right_logical(lax.shift_right_logical(u, jnp.uint32(16)), jnp.uint32(16))
    return lax.bitcast_convert_type(z, F32)


def _after(x, zero):
    return x + zero


def _scan_body(*refs, tb, n_tb, zero_init, tiled):
    n_in = 6 if zero_init else 7
    raw = refs[:6]
    s0_ref = None if zero_init else refs[6]
    y_out, st_ref = refs[n_in:n_in + 2]
    scratch = refs[n_in + 2:]
    slots = (scratch[0:6], scratch[6:12])
    state, sa, g_run = scratch[12:]
    s = pl.program_id(1)
    n_ib = HEAD_DIM // SUBLANES

    def rows_at(t):
        if tiled:
            return (t // SUBLANES, slice(None), t % SUBLANES, slice(None))
        return (slice(None), t, slice(None))

    def stage(t, wr):
        c_r, c_d, c_k, c_v, c_a, c_b = (_to_chains(src[rows_at(t)]) for src in raw)
        g_prev = g_run[...]
        g = g_prev * c_d
        g_inv = 1.0 / g
        g_run[...] = g
        staged = (c_r * g, c_k * g_inv, c_v, c_a * g_prev, c_b * g_inv)
        for dst, val in zip(wr[:5], staged):
            dst[t] = val
        return staged

    @pl.when(s == 0)
    def _():
        if zero_init:
            state[...] = jnp.zeros(state.shape, F32)
        else:
            def load_i(i, _):
                state[:, i, :] = _to_chains(s0_ref[i])
                return 0

            lax.fori_loop(0, HEAD_DIM, load_i, 0, unroll=8)
        slots[1][5][...] = jnp.ones((HEAD_DIM, LANES), F32)
        g_run[...] = jnp.ones((HEAD_DIM, LANES), F32)

        def stage_only(t, _):
            stage(t, slots[0])
            return 0

        lax.fori_loop(0, tb, stage_only, 0, unroll=min(tb, 2))
        slots[0][5][...] = g_run[...]

    def run_block(rd, wr):
        r_ref, k_ref, v_ref, a_ref, b_ref, _ = rd
        wr_g = wr[5]
        n_grp = n_ib // IB_GROUP
        assert n_grp == 2

        for grp in range(n_grp):
            rows = [pl.ds((grp * IB_GROUP + u) * SUBLANES, SUBLANES) for u in range(IB_GROUP)]

            def init_j(j, accs, rows=rows):
                jr = pl.ds(j, 1)
                gj, aj = wr_g[jr, :], a_ref[0, jr, :]
                out = []
                for u in range(IB_GROUP):
                    s_full = state[j, rows[u], :] * gj
                    state[j, rows[u], :] = s_full
                    out.append(accs[u] + s_full * aj)
                return tuple(out)

            zero = tuple(jnp.zeros((SUBLANES, LANES), F32) for _ in range(IB_GROUP))
            accs = lax.fori_loop(0, HEAD_DIM, init_j, zero, unroll=8)
            for u in range(IB_GROUP):
                sa[rows[u], :] = accs[u]
        g_run[...] = jnp.ones((HEAD_DIM, LANES), F32)

        def step(t, y_prev):
            y_tok = _from_chains(y_prev)
            y_out[rows_at(jnp.maximum(t - 1, 0))] = y_tok
            staged = stage(t, wr)
            gates = {pos: _zero_like_dep(val[0:1, 0:LANES]) for pos, val in zip(RELAYOUT_GATES, (y_tok,) + staged)}
            y_new = []
            tn = jnp.minimum(t + 1, tb - 1)
            for grp in range(n_ib // IB_GROUP):
                rows = [pl.ds((grp * IB_GROUP + u) * SUBLANES, SUBLANES) for u in range(IB_GROUP)]
                sav = [sa[rw, :] for rw in rows]
                vv = [v_ref[t, rw, :] for rw in rows]
                yacc = [jnp.zeros((SUBLANES, LANES), F32) for _ in range(IB_GROUP)]
                sacc = [jnp.zeros((SUBLANES, LANES), F32) for _ in range(IB_GROUP)]
                for j in range(HEAD_DIM):
                    jr = pl.ds(j, 1)
                    bj = b_ref[t, jr, :]
                    if (grp, j) in gates:
                        bj = _after(bj, gates[(grp, j)])
                    kj = k_ref[t, jr, :]
                    rj = r_ref[t, jr, :]
                    aj = a_ref[tn, jr, :]
                    for u in range(IB_GROUP):
                        s_new = state[j, rows[u], :] + sav[u] * bj + vv[u] * kj
                        state[j, rows[u], :] = s_new
                        yacc[u] = yacc[u] + s_new * rj
                        sacc[u] = sacc[u] + s_new * aj
                for u in range(IB_GROUP):
                    sa[rows[u], :] = sacc[u]
                y_new.extend(yacc)
            return jnp.concatenate(y_new, axis=0)

        y_last = lax.fori_loop(0, tb, step, jnp.zeros((HEAD_DIM, LANES), F32))
        y_out[rows_at(tb - 1)] = _from_chains(y_last)
        wr_g[...] = g_run[...]

    @pl.when((s > 0) & (s % 2 == 0))
    def _():
        run_block(slots[1], slots[0])

    @pl.when(s % 2 == 1)
    def _():
        run_block(slots[0], slots[1])

    @pl.when(s == n_tb)
    def _():
        g_last = slots[(n_tb - 1) % 2][5][...]

        def store_i(i, _):
            st_ref[i] = _from_chains(state[:, i, :] * g_last)
            return 0

        lax.fori_loop(0, HEAD_DIM, store_i, 0, unroll=8)


def _scan_call(r, d, k, v, a, b, s0, *, tb):
    tiled = r.ndim == 4
    bsz, t_len = (r.shape[1], r.shape[0] * SUBLANES) if tiled else (r.shape[0], r.shape[1])
    bg = BATCH_GROUP
    n_tb = t_len // tb
    if tiled:
        blk = (tb // SUBLANES, bg, SUBLANES, RWKV_WIDTH)
        op_in = pl.BlockSpec(blk, lambda g, s: (jnp.minimum(s, n_tb - 1), g, 0, 0))
        op_out = pl.BlockSpec(blk, lambda g, s: (jnp.maximum(s - 1, 0), g, 0, 0))
    else:
        op_in = pl.BlockSpec((bg, tb, RWKV_WIDTH), lambda g, s: (g, jnp.minimum(s, n_tb - 1), 0))
        op_out = pl.BlockSpec((bg, tb, RWKV_WIDTH), lambda g, s: (g, jnp.maximum(s - 1, 0), 0))
    st = pl.BlockSpec((HEAD_DIM, bg, RWKV_WIDTH), lambda g, s: (0, g, 0))
    zero_init = s0 is None
    body = functools.partial(_scan_body, tb=tb, n_tb=n_tb, zero_init=zero_init, tiled=tiled)
    return pl.pallas_call(
        body,
        grid=(bsz // bg, n_tb + 1),
        in_specs=[op_in] * 6 + ([] if zero_init else [st]),
        out_specs=[op_out, st],
        out_shape=[jax.ShapeDtypeStruct(r.shape, F32),
                   jax.ShapeDtypeStruct((HEAD_DIM, bsz, RWKV_WIDTH), F32)],
        scratch_shapes=([pltpu.VMEM((tb, HEAD_DIM, LANES), F32)] * 5 + [pltpu.VMEM((HEAD_DIM, LANES), F32)]) * 2
        + [pltpu.VMEM((HEAD_DIM, HEAD_DIM, LANES), F32), pltpu.VMEM((HEAD_DIM, LANES), F32),
           pltpu.VMEM((HEAD_DIM, LANES), F32)],
        compiler_params=pltpu.CompilerParams(dimension_semantics=("arbitrary", "arbitrary"),
                                             vmem_limit_bytes=VMEM_LIMIT),
        name="wkv_scan",
    )(r, d, k, v, a, b, *([] if zero_init else [s0]))


def _post_body(x_ref, at_ref, y_ref, g_ref, bon_ref, j512_ref, lng_ref, lnb_ref, wout_ref, g2_ref, wup_ref,
               wdn_ref, o_ref, *, ff_chunk):
    j512 = j512_ref[...]
    y = y_ref[...].reshape(g_ref.shape)
    mu = _seg_sum(y, j512) * (1.0 / HEAD_DIM)
    yc = y - mu
    var = _seg_sum(yc * yc, j512) * (1.0 / HEAD_DIM)
    yn = yc * lax.rsqrt(var + GN_EPS) * lng_ref[...] + lnb_ref[...]
    rw = (yn + bon_ref[...]) * g_ref[...]
    x1 = (x_ref[...]
          + jnp.dot(at_ref[...].astype(BF16), wout_ref[0:ATTN_WIDTH, :], preferred_element_type=F32)
          + jnp.dot(rw.astype(BF16), wout_ref[ATTN_WIDTH:D_MODEL, :], preferred_element_type=F32))
    ms = jnp.mean(x1 * x1, axis=-1, keepdims=True)
    h2 = (x1 * lax.rsqrt(ms + RMS_EPS) * g2_ref[...]).astype(BF16)
    o_ref[...] = x1
    for c in range(D_FF // ff_chunk):
        u = jnp.dot(h2, wup_ref[:, c * ff_chunk:(c + 1) * ff_chunk], preferred_element_type=F32)
        u = jnp.maximum(u, 0.0)
        o_ref[...] += jnp.dot((u * u).astype(BF16), wdn_ref[c * ff_chunk:(c + 1) * ff_chunk, :],
                              preferred_element_type=F32)


def _post_call(x2, attn, y, g, bon, consts, *, tm, ff_chunk=1024):
    n = x2.shape[0]
    tm = min(tm, n)
    tok = lambda w: pl.BlockSpec((tm, w), lambda i: (i, 0))
    full = lambda a: pl.BlockSpec(a.shape, lambda i: (0,) * a.ndim, pipeline_mode=pl.Buffered(1))
    if y.ndim == 4:
        tiles_per_seq = y.shape[0] * SUBLANES // tm
        y_spec = pl.BlockSpec((tm // SUBLANES, 1, SUBLANES, RWKV_WIDTH),
                              lambda i: (i % tiles_per_seq, i // tiles_per_seq, 0, 0))
    else:
        y_spec = tok(RWKV_WIDTH)
    return pl.pallas_call(
        functools.partial(_post_body, ff_chunk=ff_chunk),
        grid=(n // tm,),
        in_specs=[tok(D_MODEL), tok(ATTN_WIDTH), y_spec, tok(RWKV_WIDTH), tok(RWKV_WIDTH)]
        + [full(c) for c in consts],
        out_specs=tok(D_MODEL),
        out_shape=jax.ShapeDtypeStruct((n, D_MODEL), F32),
        compiler_params=pltpu.CompilerParams(dimension_semantics=("arbitrary",), vmem_limit_bytes=VMEM_LIMIT),
        name="post",
    )(x2, attn, y, g, bon, *consts)


def _layer(x, p_prev_rows, s0, attn_fn, w, *, tm_pre, tm_post, scan_tb):
    b, t, _ = x.shape
    x2 = x.reshape(b * t, D_MODEL)
    outs = _pre_call(x2, p_prev_rows, w["pre"], tm=tm_pre, seq_len=t)
    q, k, v, r, d, kh, vh, av, bv, g, bon, h_last = outs
    attn, new_k, new_v = attn_fn(q, k, v)
    ops = [a if a.ndim == 4 else a.reshape(b, t, RWKV_WIDTH) for a in (r, d, kh, vh, av, bv)]
    s0r = None if s0 is None else s0.astype(F32).transpose(2, 0, 1, 3).reshape(HEAD_DIM, b, RWKV_WIDTH)
    y, s_fin = _scan_call(*ops, s0r, tb=scan_tb)
    if y.ndim == 3:
        y = y.reshape(b * t, RWKV_WIDTH)
    s_new = s_fin.reshape(HEAD_DIM, b, N_RWKV_HEADS, HEAD_DIM).transpose(1, 2, 0, 3)
    out = _post_call(x2, attn, y, g, bon, w["post"], tm=tm_post)
    return out.reshape(b, t, D_MODEL), new_k, new_v, s_new, h_last


def _prep_weights(norm1_g, w_in, q_norm_g, k_norm_g, rwkv_mu, w_decay_0, w_decay_up, a_0, a_up, g_up, k_k, k_a,
                  r_k, ln_x_g, ln_x_b, w_out, norm2_g, w_ff_up, w_ff_down):
    o = RWKV_OFF
    pieces = ((0, o + 512), (o + 576, o + 1088), (o + 1088, o + 1600), (o + 512, o + 576), (o + 1600, o + 1664),
              (o + 1664, o + 1792))
    w_in_p = jnp.concatenate([w_in[:, lo:hi] for lo, hi in pieces], axis=1).astype(BF16)
    mu_p = jnp.concatenate([rwkv_mu[max(lo - o, 0):hi - o] for lo, hi in pieces]).reshape(1, RWKV_PROJ)
    row = lambda a: a.reshape(1, -1).astype(F32)
    eye = lambda n: jnp.kron(jnp.eye(n, dtype=F32), jnp.ones((HEAD_DIM, HEAD_DIM), F32)).astype(BF16)
    j512, j128 = eye(N_RWKV_HEADS), eye(N_KV_HEADS)
    zeros = jnp.zeros((DECAY_LORA, RWKV_WIDTH), F32)
    wcomb = jnp.concatenate([jnp.concatenate([w_decay_up, zeros], axis=1),
                             jnp.concatenate([zeros, a_up], axis=1)], axis=0).astype(BF16)
    pre = [row(norm1_g), w_in_p, mu_p, row(jnp.tile(q_norm_g, N_Q_HEADS)), row(jnp.tile(k_norm_g, N_KV_HEADS)),
           j512, j128, wcomb, row(w_decay_0), row(a_0), g_up.astype(BF16), row(k_k), row(k_a), row(r_k)]
    post = [j512, row(ln_x_g), row(ln_x_b), w_out.astype(BF16), row(norm2_g), w_ff_up.astype(BF16),
            w_ff_down.astype(BF16)]
    return {"pre": pre, "post": post, "w_rwkv": w_in_p[:, RWKV_OFF:]}


def _forward(x_prompt, x_sample, cache_k, cache_v, state_wkv, state_shift, norm1_g, w_in, q_norm_g, k_norm_g,
             attn_sinks, rwkv_mu, w_decay_0, w_decay_up, a_0, a_up, g_up, k_k, k_a, r_k, ln_x_g, ln_x_b, w_out,
             norm2_g, w_ff_up, w_ff_down, *, tm_pre=512, tm_post=512, scan_tb=64):
    depth = norm1_g.shape[0]
    assert depth == 1
    l = 0
    w = _prep_weights(norm1_g[l], w_in[l], q_norm_g[l], k_norm_g[l], rwkv_mu[l], w_decay_0[l], w_decay_up[l],
                      a_0[l], a_up[l], g_up[l], k_k[l], k_a[l], r_k[l], ln_x_g[l], ln_x_b[l], w_out[l], norm2_g[l],
                      w_ff_up[l], w_ff_down[l])
    sinks = attn_sinks[l].astype(F32)
    bp, tp, _ = x_prompt.shape
    bs, ts, _ = x_sample.shape

    def attn_prompt(q, k, v):
        q3, k3, v3 = (a.reshape(bp, tp, -1) for a in (q, k, v))
        out = _attn_prompt_call(sinks, q3, k3, v3).reshape(bp * tp, ATTN_WIDTH)
        tail = lambda a: a[:, tp - WINDOW:].reshape(bp, WINDOW, N_KV_HEADS, HEAD_DIM)
        return out, tail(k3), tail(v3)

    pprev_p = jnp.zeros((bp, 1, RWKV_PROJ), F32)
    yp, k1, v1, w1, s1 = _layer(x_prompt, pprev_p, None, attn_prompt, w, tm_pre=tm_pre, tm_post=tm_post,
                                scan_tb=scan_tb)
    s1 = s1.reshape(bp, D_MODEL)

    ck, cv = cache_k[l].astype(F32), cache_v[l].astype(F32)
    group = N_Q_HEADS // N_KV_HEADS
    rows = group * ts
    keys = WINDOW + ts

    def attn_sample(q, k, v):
        k4 = k.reshape(bs, ts, N_KV_HEADS, HEAD_DIM)
        v4 = v.reshape(bs, ts, N_KV_HEADS, HEAD_DIM)
        kall = jnp.concatenate([ck, k4], axis=1)
        vall = jnp.concatenate([cv, v4], axis=1)
        dense = lambda a: a.reshape(bs, keys, KV_WIDTH)
        q4 = q.reshape(bs, ts, N_KV_HEADS, group, HEAD_DIM).transpose(0, 2, 3, 1, 4).reshape(
            bs, N_KV_HEADS, rows, HEAD_DIM)
        eye = jnp.eye(N_KV_HEADS, dtype=F32)
        qbd = (q4[:, :, :, None, :] * eye[None, :, None, :, None]).reshape(bs, N_KV_HEADS * rows, KV_WIDTH)
        sink_rows = jnp.repeat(sinks, ts)[:, None]
        o = _attn_sample_call(sink_rows, qbd, dense(kall), dense(vall), dec_seq=ts)
        o = o.reshape(bs, N_KV_HEADS, rows, N_KV_HEADS, HEAD_DIM)
        o = jnp.stack([o[:, hk, :, hk, :] for hk in range(N_KV_HEADS)], axis=1)
        o = o.reshape(bs, N_KV_HEADS, group, ts, HEAD_DIM).transpose(0, 3, 1, 2, 4)
        return o.reshape(bs * ts, ATTN_WIDTH), kall[:, ts:], vall[:, ts:]

    p_prev = _matmul_call(state_shift[l], w["w_rwkv"])
    pprev_s = jnp.repeat(p_prev, ts, axis=0)
    ys, k2, v2, w2, h_all = _layer(x_sample, pprev_s, state_wkv[l], attn_sample, w, tm_pre=tm_pre,
                                   tm_post=tm_post, scan_tb=ts)
    s2 = h_all.reshape(bs, ts, D_MODEL)[:, -1]
    st = lambda a: a[None]
    return (yp, ys, st(k1), st(v1), st(w1), st(s1), st(k2), st(v2), st(w2), st(s2))


def kernel(x_prompt, x_sample, cache_k, cache_v, state_wkv, state_shift, norm1_g, w_in, q_norm_g, k_norm_g, attn_sinks, rwkv_mu, w_decay_0, w_decay_up, a_0, a_up, g_up, k_k, k_a, r_k, ln_x_g, ln_x_b, w_out, norm2_g, w_ff_up, w_ff_down):
    return _forward(x_prompt, x_sample, cache_k, cache_v, state_wkv, state_shift, norm1_g, w_in, q_norm_g, k_norm_g,
                    attn_sinks, rwkv_mu, w_decay_0, w_decay_up, a_0, a_up, g_up, k_k, k_a, r_k, ln_x_g, ln_x_b,
                    w_out, norm2_g, w_ff_up, w_ff_down)
```

```python
import functools

import jax
import jax.numpy as jnp
from jax import lax
from jax.experimental import pallas as pl
from jax.experimental.pallas import tpu as pltpu

D_MODEL = 1024
HEAD_DIM = 64
ATTN_WIDTH = 512
N_Q_HEADS = 8
N_KV_HEADS = 2
KV_WIDTH = 128
RWKV_WIDTH = 512
N_RWKV_HEADS = 8
WINDOW = 128
SCALE = HEAD_DIM ** -0.5
DECAY_LORA = 64
AAA_LORA = 64
GATE_LORA = 128
D_FF = 4 * D_MODEL
RMS_EPS = 1e-6
GN_EPS = 64e-5
RWKV_OFF = ATTN_WIDTH + 2 * KV_WIDTH
RWKV_PROJ = 3 * RWKV_WIDTH + DECAY_LORA + AAA_LORA + GATE_LORA
IN_WIDTH = RWKV_OFF + RWKV_PROJ
LANES = 128
SUBLANES = 8
VMEM_LIMIT = 56 * 1024 * 1024

F32 = jnp.float32
BF16 = jnp.bfloat16


def _seg_sum(x, ones_blk):
    return jnp.dot(x.astype(BF16), ones_blk, preferred_element_type=F32)


def _pre_body(x_ref, pprev_ref, g1_ref, win_ref, mu_ref, qg_ref, kg_ref, j512_ref, j128_ref,
              wcomb_ref, w0_ref, a0_ref, gup_ref, kk_ref, ka_ref, rk_ref,
              q_o, k_o, v_o, r_o, d_o, kh_o, vh_o, a_o, b_o, g_o, bon_o, h_o,
              carry, *, tm, tiles_per_seq, seq_len):
    x = x_ref[...]
    ms = jnp.mean(x * x, axis=-1, keepdims=True)
    h = x * lax.rsqrt(ms + RMS_EPS) * g1_ref[...]
    proj = jnp.dot(h.astype(BF16), win_ref[...], preferred_element_type=F32)

    j512 = j512_ref[...]
    q = proj[:, 0:ATTN_WIDTH]
    qs = _seg_sum(q * q, j512) * (1.0 / HEAD_DIM)
    q_o[...] = q * lax.rsqrt(qs + RMS_EPS) * qg_ref[...]
    k = proj[:, ATTN_WIDTH:ATTN_WIDTH + KV_WIDTH]
    ks = _seg_sum(k * k, j128_ref[...]) * (1.0 / HEAD_DIM)
    k_o[...] = k * lax.rsqrt(ks + RMS_EPS) * kg_ref[...]
    v_o[...] = proj[:, ATTN_WIDTH + KV_WIDTH:RWKV_OFF]

    p = proj[:, RWKV_OFF:]
    rolled = pltpu.roll(p, shift=1, axis=0)
    row = lax.broadcasted_iota(jnp.int32, (tm, 1), 0)
    if tiles_per_seq is not None:
        t = pl.program_id(0) % tiles_per_seq
        first = jnp.where(t == 0, pprev_ref[0], carry[...])
        p_shift = jnp.where(row == 0, first, rolled)
        carry[...] = p[tm - 1:tm, :]

        @pl.when(t == tiles_per_seq - 1)
        def _():
            h_o[0] = h[tm - 1:tm, :]
    else:
        p_shift = jnp.where(row % seq_len == 0, pprev_ref[...], rolled)
        h_o[...] = h
    pm = p + (p_shift - p) * mu_ref[...]

    xr = pm[:, 0:512]
    xk = pm[:, 512:1024]
    xv = pm[:, 1024:1536]
    xwa = pm[:, 1536:1664]
    xg = pm[:, 1664:1792]
    lane = lax.broadcasted_iota(jnp.int32, (1, LANES), 1)
    wa_in = jnp.where(lane < DECAY_LORA, jnp.tanh(xwa), xwa)
    pre = jnp.dot(wa_in.astype(BF16), wcomb_ref[...], preferred_element_type=F32)
    w_pre = pre[:, 0:512] + w0_ref[...]
    a_pre = pre[:, 512:1024] + a0_ref[...]
    w_log = -jnp.log(1.0 + jnp.exp(-w_pre)) - 0.5
    d_o[...] = jnp.exp(-jnp.exp(w_log))
    a_gate = 1.0 / (1.0 + jnp.exp(-a_pre))
    sg = 1.0 / (1.0 + jnp.exp(-xg))
    g_o[...] = jnp.dot(sg.astype(BF16), gup_ref[...], preferred_element_type=F32)
    kkv = xk * kk_ref[...]
    n2 = _seg_sum(kkv * kkv, j512)
    kkn = kkv * lax.rsqrt(jnp.maximum(n2, 1e-24))
    k_h = xk * (1.0 + (a_gate - 1.0) * ka_ref[...])
    r_o[...] = xr
    kh_o[...] = k_h
    vh_o[...] = xv
    a_o[...] = -kkn
    b_o[...] = kkn * a_gate
    bon_o[...] = _seg_sum(xr * k_h * rk_ref[...], j512) * xv


def _pre_call(x2, pprev, consts, *, tm, seq_len):
    n = x2.shape[0]
    tm = min(tm, n)
    n_tiles = n // tm
    per_seq = seq_len >= tm
    tiles_per_seq = seq_len // tm if per_seq else None
    n_seq = n // seq_len
    full = lambda a: pl.BlockSpec(a.shape, lambda i: (0,) * a.ndim)
    tok = lambda w: pl.BlockSpec((tm, w), lambda i: (i, 0))
    if per_seq:
        pprev_spec = pl.BlockSpec((1, 1, RWKV_PROJ), lambda i: (i // tiles_per_seq, 0, 0))
        h_spec = pl.BlockSpec((1, 1, D_MODEL), lambda i: (i // tiles_per_seq, 0, 0))
        h_shape = jax.ShapeDtypeStruct((n_seq, 1, D_MODEL), F32)
    else:
        pprev_spec = tok(RWKV_PROJ)
        h_spec = tok(D_MODEL)
        h_shape = jax.ShapeDtypeStruct((n, D_MODEL), F32)
    widths = [ATTN_WIDTH, KV_WIDTH, KV_WIDTH] + [RWKV_WIDTH] * 8
    out_shape = [jax.ShapeDtypeStruct((n, w), F32) for w in widths] + [h_shape]
    out_specs = [tok(w) for w in widths] + [h_spec]
    body = functools.partial(_pre_body, tm=tm, tiles_per_seq=tiles_per_seq, seq_len=seq_len)
    return pl.pallas_call(
        body,
        grid=(n_tiles,),
        in_specs=[tok(D_MODEL), pprev_spec] + [full(c) for c in consts],
        out_specs=out_specs,
        out_shape=out_shape,
        scratch_shapes=[pltpu.VMEM((1, RWKV_PROJ), F32)],
        compiler_params=pltpu.CompilerParams(dimension_semantics=("arbitrary",), vmem_limit_bytes=VMEM_LIMIT),
        name="pre",
    )(x2, pprev, *consts)


def _matmul_body(x_ref, w_ref, o_ref):
    o_ref[...] = jnp.dot(x_ref[...].astype(BF16), w_ref[...], preferred_element_type=F32)


def _matmul_call(x, w):
    return pl.pallas_call(_matmul_body, out_shape=jax.ShapeDtypeStruct((x.shape[0], w.shape[1]), F32),
                          name="shift_proj")(x, w)


def _softmax_sink(s, valid, sink):
    s = jnp.where(valid, s, -jnp.inf)
    m = jnp.maximum(jnp.max(s, axis=-1, keepdims=True), sink)
    p = jnp.exp(s - m)
    den = jnp.sum(p, axis=-1, keepdims=True) + jnp.exp(sink - m)
    return p / den


def _attn_prompt_body(sink_ref, q_ref, kc_ref, kp_ref, vc_ref, vp_ref, o_ref, *, nq):
    for sub in range(nq):
        rows = slice(sub * WINDOW, (sub + 1) * WINDOW)
        prev = slice((sub - 1) * WINDOW, sub * WINDOW)
        k_prev, v_prev = (kp_ref[0], vp_ref[0]) if sub == 0 else (kc_ref[0, prev], vc_ref[0, prev])
        first = (pl.program_id(1) == 0) if sub == 0 else None
        _attn_block(sink_ref, q_ref[0, rows], k_prev, kc_ref[0, rows], v_prev, vc_ref[0, rows], first,
                    o_ref.at[0, rows])


def _attn_block(sink_ref, q, k_prev, k_cur, v_prev, v_cur, first, o_ref):
    kcat = jnp.concatenate([k_prev, k_cur], axis=0)
    vcat = jnp.concatenate([v_prev, v_cur], axis=0)
    krot = pltpu.roll(kcat, shift=HEAD_DIM, axis=1)
    vrot = pltpu.roll(vcat, shift=HEAD_DIM, axis=1)
    lane = lax.broadcasted_iota(jnp.int32, (1, LANES), 1)
    lo = lane < HEAD_DIM
    row = lax.broadcasted_iota(jnp.int32, (WINDOW, WINDOW), 0)
    col = lax.broadcasted_iota(jnp.int32, (WINDOW, WINDOW), 1)
    upper = col > row
    no_prev = None if first is None else upper & first
    for hk in range(N_KV_HEADS):
        src_lo, src_hi = (kcat, krot) if hk == 0 else (krot, kcat)
        k_pads = (jnp.where(lo, src_lo, 0.0).astype(BF16), jnp.where(lo, 0.0, src_hi).astype(BF16))
        vsrc_lo, vsrc_hi = (vcat, vrot) if hk == 0 else (vrot, vcat)
        ones = jnp.ones((2 * WINDOW, LANES), BF16)
        v_pads = tuple(jnp.concatenate([vp.astype(BF16), ones], axis=1)
                       for vp in (jnp.where(lo, vsrc_lo, 0.0), jnp.where(lo, 0.0, vsrc_hi)))
        for pr in range(2):
            pair = hk * 2 + pr
            lanes = slice(pair * LANES, (pair + 1) * LANES)
            qp = (q[:, lanes] * SCALE).astype(BF16)
            out = None
            for e in range(2):
                s = lax.dot_general(qp, k_pads[e], (((1,), (1,)), ((), ())),
                                    preferred_element_type=F32)
                s = jnp.where(upper, s[:, 0:WINDOW], s[:, WINDOW:2 * WINDOW])
                if no_prev is not None:
                    s = jnp.where(no_prev, -jnp.inf, s)
                sink = sink_ref[pair * 2 + e]
                m = jnp.maximum(jnp.max(s, axis=-1, keepdims=True), sink)
                p = jnp.exp(s - m)
                p2 = jnp.concatenate([jnp.where(upper, p, 0.0), jnp.where(upper, 0.0, p)], axis=1)
                pv = jnp.dot(p2.astype(BF16), v_pads[e], preferred_element_type=F32)
                scaled = pv[:, 0:LANES] / (pv[:, LANES:2 * LANES] + jnp.exp(sink - m))
                out = scaled if out is None else out + scaled
            o_ref[:, lanes] = out


def _attn_prompt_call(sinks, q, k, v, *, nq=8):
    b, t, _ = q.shape
    nb = t // WINDOW
    nq = min(nq, nb)
    assert nb % nq == 0
    cur = lambda w: pl.BlockSpec((1, nq * WINDOW, w), lambda i, j: (i, j, 0))
    prev = lambda w: pl.BlockSpec((1, WINDOW, w), lambda i, j: (i, jnp.maximum(nq * j - 1, 0), 0))
    return pl.pallas_call(
        functools.partial(_attn_prompt_body, nq=nq),
        grid=(b, nb // nq),
        in_specs=[pl.BlockSpec(memory_space=pltpu.SMEM), cur(ATTN_WIDTH), cur(KV_WIDTH), prev(KV_WIDTH),
                  cur(KV_WIDTH), prev(KV_WIDTH)],
        out_specs=cur(ATTN_WIDTH),
        out_shape=jax.ShapeDtypeStruct((b, t, ATTN_WIDTH), F32),
        compiler_params=pltpu.CompilerParams(dimension_semantics=("arbitrary", "arbitrary")),
        name="attn_prompt",
    )(sinks, q, k, k, v, v)


def _attn_sample_body(sink_ref, q_ref, k_ref, v_ref, o_ref, *, bb, rows, keys, dec_seq):
    r = lax.broadcasted_iota(jnp.int32, (rows, keys), 0)
    c = lax.broadcasted_iota(jnp.int32, (rows, keys), 1)
    t = r % dec_seq
    valid = (c > t) & (c <= t + WINDOW)
    sink = sink_ref[...]

    def one(i, _):
        q = q_ref[i].astype(BF16)
        k = k_ref[i].astype(BF16)
        s = lax.dot_general(q, k, (((1,), (1,)), ((), ())), preferred_element_type=F32) * SCALE
        probs = _softmax_sink(s, valid, sink)
        o_ref[i] = jnp.dot(probs.astype(BF16), v_ref[i].astype(BF16), preferred_element_type=F32)
        return 0

    lax.fori_loop(0, bb, one, 0, unroll=8)


def _attn_sample_call(sink_rows, q, kall, vall, *, dec_seq, bb=32):
    bsz, rows, _ = q.shape
    keys = kall.shape[1]
    blk = lambda r: pl.BlockSpec((bb, r, KV_WIDTH), lambda i: (i, 0, 0))
    body = functools.partial(_attn_sample_body, bb=bb, rows=rows, keys=keys, dec_seq=dec_seq)
    return pl.pallas_call(
        body,
        grid=(bsz // bb,),
        in_specs=[pl.BlockSpec((rows, 1), lambda i: (0, 0)), blk(rows), blk(keys), blk(keys)],
        out_specs=blk(rows),
        out_shape=jax.ShapeDtypeStruct((bsz, rows, KV_WIDTH), F32),
        compiler_params=pltpu.CompilerParams(dimension_semantics=("arbitrary",)),
        name="attn_sample",
    )(sink_rows, q, kall, vall)


IB_GROUP = 4
RELAYOUT_GATES = ((0, 8), (0, 24), (0, 40), (0, 56), (1, 32), (1, 48))


BATCH_GROUP = LANES // N_RWKV_HEADS
HEAD_PAIRS = N_RWKV_HEADS // 2


def _to_chains(x):
    m = jnp.concatenate([x[:, hp * LANES:(hp + 1) * LANES] for hp in range(HEAD_PAIRS)], axis=0)
    mt = m.T
    return jnp.concatenate([mt[0:HEAD_DIM, :], mt[HEAD_DIM:2 * HEAD_DIM, :]], axis=1)


def _from_chains(y):
    m = jnp.concatenate([y[:, 0:HEAD_DIM], y[:, HEAD_DIM:2 * HEAD_DIM]], axis=0)
    mt = m.T
    return jnp.concatenate([mt[hp * BATCH_GROUP:(hp + 1) * BATCH_GROUP, :] for hp in range(HEAD_PAIRS)], axis=1)


def _zero_like_dep(x):
    u = lax.bitcast_convert_type(x, jnp.uint32)
    z = lax.shift_right_logical(lax.shift_right_logical(u, jnp.uint32(16)), jnp.uint32(16))
    return lax.bitcast_convert_type(z, F32)


def _after(x, zero):
    return x + zero


def _scan_body(*refs, tb, n_tb, zero_init):
    n_in = 6 if zero_init else 7
    raw = refs[:6]
    s0_ref = None if zero_init else refs[6]
    y_out, st_ref = refs[n_in:n_in + 2]
    scratch = refs[n_in + 2:]
    slots = (scratch[0:6], scratch[6:12])
    state, sa, g_run = scratch[12:]
    s = pl.program_id(1)
    n_ib = HEAD_DIM // SUBLANES

    def stage(t, wr):
        c_r, c_d, c_k, c_v, c_a, c_b = (_to_chains(src[:, t, :]) for src in raw)
        g_prev = g_run[...]
        g = g_prev * c_d
        g_inv = 1.0 / g
        g_run[...] = g
        staged = (c_r * g, c_k * g_inv, c_v, c_a * g_prev, c_b * g_inv)
        for dst, val in zip(wr[:5], staged):
            dst[t] = val
        return staged

    @pl.when(s == 0)
    def _():
        if zero_init:
            state[...] = jnp.zeros(state.shape, F32)
        else:
            def load_i(i, _):
                state[:, i, :] = _to_chains(s0_ref[i])
                return 0

            lax.fori_loop(0, HEAD_DIM, load_i, 0, unroll=8)
        slots[1][5][...] = jnp.ones((HEAD_DIM, LANES), F32)
        g_run[...] = jnp.ones((HEAD_DIM, LANES), F32)

        def stage_only(t, _):
            stage(t, slots[0])
            return 0

        lax.fori_loop(0, tb, stage_only, 0, unroll=min(tb, 2))
        slots[0][5][...] = g_run[...]

    def run_block(rd, wr):
        r_ref, k_ref, v_ref, a_ref, b_ref, _ = rd
        wr_g = wr[5]
        n_grp = n_ib // IB_GROUP
        assert n_grp == 2

        for grp in range(n_grp):
            rows = [pl.ds((grp * IB_GROUP + u) * SUBLANES, SUBLANES) for u in range(IB_GROUP)]

            def init_j(j, accs, rows=rows):
                jr = pl.ds(j, 1)
                gj, aj = wr_g[jr, :], a_ref[0, jr, :]
                out = []
                for u in range(IB_GROUP):
                    s_full = state[j, rows[u], :] * gj
                    state[j, rows[u], :] = s_full
                    out.append(accs[u] + s_full * aj)
                return tuple(out)

            zero = tuple(jnp.zeros((SUBLANES, LANES), F32) for _ in range(IB_GROUP))
            accs = lax.fori_loop(0, HEAD_DIM, init_j, zero, unroll=8)
            for u in range(IB_GROUP):
                sa[rows[u], :] = accs[u]
        g_run[...] = jnp.ones((HEAD_DIM, LANES), F32)

        def step(t, y_prev):
            y_tok = _from_chains(y_prev)
            y_out[:, jnp.maximum(t - 1, 0), :] = y_tok
            staged = stage(t, wr)
            gates = {pos: _zero_like_dep(val[0:1, 0:LANES]) for pos, val in zip(RELAYOUT_GATES, (y_tok,) + staged)}
            y_new = []
            tn = jnp.minimum(t + 1, tb - 1)
            for grp in range(n_ib // IB_GROUP):
                rows = [pl.ds((grp * IB_GROUP + u) * SUBLANES, SUBLANES) for u in range(IB_GROUP)]
                sav = [sa[rw, :] for rw in rows]
                vv = [v_ref[t, rw, :] for rw in rows]
                yacc = [jnp.zeros((SUBLANES, LANES), F32) for _ in range(IB_GROUP)]
                sacc = [jnp.zeros((SUBLANES, LANES), F32) for _ in range(IB_GROUP)]
                for j in range(HEAD_DIM):
                    jr = pl.ds(j, 1)
                    bj = b_ref[t, jr, :]
                    if (grp, j) in gates:
                        bj = _after(bj, gates[(grp, j)])
                    kj = k_ref[t, jr, :]
                    rj = r_ref[t, jr, :]
                    aj = a_ref[tn, jr, :]
                    for u in range(IB_GROUP):
                        s_new = state[j, rows[u], :] + sav[u] * bj + vv[u] * kj
                        state[j, rows[u], :] = s_new
                        yacc[u] = yacc[u] + s_new * rj
                        sacc[u] = sacc[u] + s_new * aj
                for u in range(IB_GROUP):
                    sa[rows[u], :] = sacc[u]
                y_new.extend(yacc)
            return jnp.concatenate(y_new, axis=0)

        y_last = lax.fori_loop(0, tb, step, jnp.zeros((HEAD_DIM, LANES), F32), unroll=min(tb, 2))
        y_out[:, tb - 1, :] = _from_chains(y_last)
        wr_g[...] = g_run[...]

    @pl.when((s > 0) & (s % 2 == 0))
    def _():
        run_block(slots[1], slots[0])

    @pl.when(s % 2 == 1)
    def _():
        run_block(slots[0], slots[1])

    @pl.when(s == n_tb)
    def _():
        g_last = slots[(n_tb - 1) % 2][5][...]

        def store_i(i, _):
            st_ref[i] = _from_chains(state[:, i, :] * g_last)
            return 0

        lax.fori_loop(0, HEAD_DIM, store_i, 0, unroll=8)


def _scan_call(r, d, k, v, a, b, s0, *, tb):
    bsz, t_len = r.shape[0], r.shape[1]
    bg = BATCH_GROUP
    n_tb = t_len // tb
    op_in = pl.BlockSpec((bg, tb, RWKV_WIDTH), lambda g, s: (g, jnp.minimum(s, n_tb - 1), 0))
    op_out = pl.BlockSpec((bg, tb, RWKV_WIDTH), lambda g, s: (g, jnp.maximum(s - 1, 0), 0))
    st = pl.BlockSpec((HEAD_DIM, bg, RWKV_WIDTH), lambda g, s: (0, g, 0))
    zero_init = s0 is None
    body = functools.partial(_scan_body, tb=tb, n_tb=n_tb, zero_init=zero_init)
    return pl.pallas_call(
        body,
        grid=(bsz // bg, n_tb + 1),
        in_specs=[op_in] * 6 + ([] if zero_init else [st]),
        out_specs=[op_out, st],
        out_shape=[jax.ShapeDtypeStruct(r.shape, F32),
                   jax.ShapeDtypeStruct((HEAD_DIM, bsz, RWKV_WIDTH), F32)],
        scratch_shapes=([pltpu.VMEM((tb, HEAD_DIM, LANES), F32)] * 5 + [pltpu.VMEM((HEAD_DIM, LANES), F32)]) * 2
        + [pltpu.VMEM((HEAD_DIM, HEAD_DIM, LANES), F32), pltpu.VMEM((HEAD_DIM, LANES), F32),
           pltpu.VMEM((HEAD_DIM, LANES), F32)],
        compiler_params=pltpu.CompilerParams(dimension_semantics=("arbitrary", "arbitrary"),
                                             vmem_limit_bytes=VMEM_LIMIT),
        name="wkv_scan",
    )(r, d, k, v, a, b, *([] if zero_init else [s0]))


def _post_body(x_ref, at_ref, y_ref, g_ref, bon_ref, j512_ref, lng_ref, lnb_ref, wout_ref, g2_ref, wup_ref,
               wdn_ref, o_ref, *, ff_chunk):
    j512 = j512_ref[...]
    y = y_ref[...]
    mu = _seg_sum(y, j512) * (1.0 / HEAD_DIM)
    yc = y - mu
    var = _seg_sum(yc * yc, j512) * (1.0 / HEAD_DIM)
    yn = yc * lax.rsqrt(var + GN_EPS) * lng_ref[...] + lnb_ref[...]
    rw = (yn + bon_ref[...]) * g_ref[...]
    x1 = (x_ref[...]
          + jnp.dot(at_ref[...].astype(BF16), wout_ref[0:ATTN_WIDTH, :], preferred_element_type=F32)
          + jnp.dot(rw.astype(BF16), wout_ref[ATTN_WIDTH:D_MODEL, :], preferred_element_type=F32))
    ms = jnp.mean(x1 * x1, axis=-1, keepdims=True)
    h2 = (x1 * lax.rsqrt(ms + RMS_EPS) * g2_ref[...]).astype(BF16)
    o_ref[...] = x1
    for c in range(D_FF // ff_chunk):
        u = jnp.dot(h2, wup_ref[:, c * ff_chunk:(c + 1) * ff_chunk], preferred_element_type=F32)
        u = jnp.maximum(u, 0.0)
        o_ref[...] += jnp.dot((u * u).astype(BF16), wdn_ref[c * ff_chunk:(c + 1) * ff_chunk, :],
                              preferred_element_type=F32)


def _post_call(x2, attn, y, g, bon, consts, *, tm, ff_chunk=1024):
    n = x2.shape[0]
    tm = min(tm, n)
    tok = lambda w: pl.BlockSpec((tm, w), lambda i: (i, 0))
    full = lambda a: pl.BlockSpec(a.shape, lambda i: (0,) * a.ndim, pipeline_mode=pl.Buffered(1))
    return pl.pallas_call(
        functools.partial(_post_body, ff_chunk=ff_chunk),
        grid=(n // tm,),
        in_specs=[tok(D_MODEL), tok(ATTN_WIDTH), tok(RWKV_WIDTH), tok(RWKV_WIDTH), tok(RWKV_WIDTH)]
        + [full(c) for c in consts],
        out_specs=tok(D_MODEL),
        out_shape=jax.ShapeDtypeStruct((n, D_MODEL), F32),
        compiler_params=pltpu.CompilerParams(dimension_semantics=("arbitrary",), vmem_limit_bytes=VMEM_LIMIT),
        name="post",
    )(x2, attn, y, g, bon, *consts)


def _layer(x, p_prev_rows, s0, attn_fn, w, *, tm_pre, tm_post, scan_tb):
    b, t, _ = x.shape
    x2 = x.reshape(b * t, D_MODEL)
    outs = _pre_call(x2, p_prev_rows, w["pre"], tm=tm_pre, seq_len=t)
    q, k, v, r, d, kh, vh, av, bv, g, bon, h_last = outs
    attn, new_k, new_v = attn_fn(q, k, v)
    ops = [a.reshape(b, t, RWKV_WIDTH) for a in (r, d, kh, vh, av, bv)]
    s0r = None if s0 is None else s0.astype(F32).transpose(2, 0, 1, 3).reshape(HEAD_DIM, b, RWKV_WIDTH)
    y3, s_fin = _scan_call(*ops, s0r, tb=scan_tb)
    y = y3.reshape(b * t, RWKV_WIDTH)
    s_new = s_fin.reshape(HEAD_DIM, b, N_RWKV_HEADS, HEAD_DIM).transpose(1, 2, 0, 3)
    out = _post_call(x2, attn, y, g, bon, w["post"], tm=tm_post)
    return out.reshape(b, t, D_MODEL), new_k, new_v, s_new, h_last


def _prep_weights(norm1_g, w_in, q_norm_g, k_norm_g, rwkv_mu, w_decay_0, w_decay_up, a_0, a_up, g_up, k_k, k_a,
                  r_k, ln_x_g, ln_x_b, w_out, norm2_g, w_ff_up, w_ff_down):
    o = RWKV_OFF
    pieces = ((0, o + 512), (o + 576, o + 1088), (o + 1088, o + 1600), (o + 512, o + 576), (o + 1600, o + 1664),
              (o + 1664, o + 1792))
    w_in_p = jnp.concatenate([w_in[:, lo:hi] for lo, hi in pieces], axis=1).astype(BF16)
    mu_p = jnp.concatenate([rwkv_mu[max(lo - o, 0):hi - o] for lo, hi in pieces]).reshape(1, RWKV_PROJ)
    row = lambda a: a.reshape(1, -1).astype(F32)
    eye = lambda n: jnp.kron(jnp.eye(n, dtype=F32), jnp.ones((HEAD_DIM, HEAD_DIM), F32)).astype(BF16)
    j512, j128 = eye(N_RWKV_HEADS), eye(N_KV_HEADS)
    zeros = jnp.zeros((DECAY_LORA, RWKV_WIDTH), F32)
    wcomb = jnp.concatenate([jnp.concatenate([w_decay_up, zeros], axis=1),
                             jnp.concatenate([zeros, a_up], axis=1)], axis=0).astype(BF16)
    pre = [row(norm1_g), w_in_p, mu_p, row(jnp.tile(q_norm_g, N_Q_HEADS)), row(jnp.tile(k_norm_g, N_KV_HEADS)),
           j512, j128, wcomb, row(w_decay_0), row(a_0), g_up.astype(BF16), row(k_k), row(k_a), row(r_k)]
    post = [j512, row(ln_x_g), row(ln_x_b), w_out.astype(BF16), row(norm2_g), w_ff_up.astype(BF16),
            w_ff_down.astype(BF16)]
    return {"pre": pre, "post": post, "w_rwkv": w_in_p[:, RWKV_OFF:]}


def _forward(x_prompt, x_sample, cache_k, cache_v, state_wkv, state_shift, norm1_g, w_in, q_norm_g, k_norm_g,
             attn_sinks, rwkv_mu, w_decay_0, w_decay_up, a_0, a_up, g_up, k_k, k_a, r_k, ln_x_g, ln_x_b, w_out,
             norm2_g, w_ff_up, w_ff_down, *, tm_pre=512, tm_post=512, scan_tb=64):
    depth = norm1_g.shape[0]
    assert depth == 1
    l = 0
    w = _prep_weights(norm1_g[l], w_in[l], q_norm_g[l], k_norm_g[l], rwkv_mu[l], w_decay_0[l], w_decay_up[l],
                      a_0[l], a_up[l], g_up[l], k_k[l], k_a[l], r_k[l], ln_x_g[l], ln_x_b[l], w_out[l], norm2_g[l],
                      w_ff_up[l], w_ff_down[l])
    sinks = attn_sinks[l].astype(F32)
    bp, tp, _ = x_prompt.shape
    bs, ts, _ = x_sample.shape

    def attn_prompt(q, k, v):
        q3, k3, v3 = (a.reshape(bp, tp, -1) for a in (q, k, v))
        out = _attn_prompt_call(sinks, q3, k3, v3).reshape(bp * tp, ATTN_WIDTH)
        tail = lambda a: a[:, tp - WINDOW:].reshape(bp, WINDOW, N_KV_HEADS, HEAD_DIM)
        return out, tail(k3), tail(v3)

    pprev_p = jnp.zeros((bp, 1, RWKV_PROJ), F32)
    yp, k1, v1, w1, s1 = _layer(x_prompt, pprev_p, None, attn_prompt, w, tm_pre=tm_pre, tm_post=tm_post,
                                scan_tb=scan_tb)
    s1 = s1.reshape(bp, D_MODEL)

    ck, cv = cache_k[l].astype(F32), cache_v[l].astype(F32)
    group = N_Q_HEADS // N_KV_HEADS
    rows = group * ts
    keys = WINDOW + ts

    def attn_sample(q, k, v):
        k4 = k.reshape(bs, ts, N_KV_HEADS, HEAD_DIM)
        v4 = v.reshape(bs, ts, N_KV_HEADS, HEAD_DIM)
        kall = jnp.concatenate([ck, k4], axis=1)
        vall = jnp.concatenate([cv, v4], axis=1)
        dense = lambda a: a.reshape(bs, keys, KV_WIDTH)
        q4 = q.reshape(bs, ts, N_KV_HEADS, group, HEAD_DIM).transpose(0, 2, 3, 1, 4).reshape(
            bs, N_KV_HEADS, rows, HEAD_DIM)
        eye = jnp.eye(N_KV_HEADS, dtype=F32)
        qbd = (q4[:, :, :, None, :] * eye[None, :, None, :, None]).reshape(bs, N_KV_HEADS * rows, KV_WIDTH)
        sink_rows = jnp.repeat(sinks, ts)[:, None]
        o = _attn_sample_call(sink_rows, qbd, dense(kall), dense(vall), dec_seq=ts)
        o = o.reshape(bs, N_KV_HEADS, rows, N_KV_HEADS, HEAD_DIM)
        o = jnp.stack([o[:, hk, :, hk, :] for hk in range(N_KV_HEADS)], axis=1)
        o = o.reshape(bs, N_KV_HEADS, group, ts, HEAD_DIM).transpose(0, 3, 1, 2, 4)
        return o.reshape(bs * ts, ATTN_WIDTH), kall[:, ts:], vall[:, ts:]

    p_prev = _matmul_call(state_shift[l], w["w_rwkv"])
    pprev_s = jnp.repeat(p_prev, ts, axis=0)
    ys, k2, v2, w2, h_all = _layer(x_sample, pprev_s, state_wkv[l], attn_sample, w, tm_pre=tm_pre,
                                   tm_post=tm_post, scan_tb=ts)
    s2 = h_all.reshape(bs, ts, D_MODEL)[:, -1]
    st = lambda a: a[None]
    return (yp, ys, st(k1), st(v1), st(w1), st(s1), st(k2), st(v2), st(w2), st(s2))


def kernel(x_prompt, x_sample, cache_k, cache_v, state_wkv, state_shift, norm1_g, w_in, q_norm_g, k_norm_g, attn_sinks, rwkv_mu, w_decay_0, w_decay_up, a_0, a_up, g_up, k_k, k_a, r_k, ln_x_g, ln_x_b, w_out, norm2_g, w_ff_up, w_ff_down):
    return _forward(x_prompt, x_sample, cache_k, cache_v, state_wkv, state_shift, norm1_g, w_in, q_norm_g, k_norm_g,
                    attn_sinks, rwkv_mu, w_decay_0, w_decay_up, a_0, a_up, g_up, k_k, k_a, r_k, ln_x_g, ln_x_b,
                    w_out, norm2_g, w_ff_up, w_ff_down)
```
